```python
import math
import jax, jax.numpy as jnp
from jax import lax
import numpy as np

D_MODEL = 1024
BATCH = 8
SEQ = 2048
DEPTH = 2
DEC_BATCH = 128
DEC_SEQ = 4
PAST_LEN = 16384
PAGE_SIZE = 128

D_MIX = D_MODEL
SSD_WIDTH = D_MIX // 2
SSD_HEAD_DIM = 64
SSD_HEADS = SSD_WIDTH // SSD_HEAD_DIM
SSD_GROUPS = 2
SSD_STATE = 128
SSD_CONV = 4
SSD_CHUNK = 64
SSD_CONV_DIM = SSD_WIDTH + 2 * SSD_GROUPS * SSD_STATE
RWKV_WIDTH = D_MIX // 4
RWKV_HEAD_DIM = 64
RWKV_HEADS = RWKV_WIDTH // RWKV_HEAD_DIM
RWKV_DECAY_LORA = 64
RWKV_A_LORA = 64
RWKV_GATE_LORA = 128
RWKV_PROJ = 3 * RWKV_WIDTH + RWKV_DECAY_LORA + RWKV_A_LORA + RWKV_GATE_LORA
RWKV_LN_EPS = 64e-5
S5_WIDTH = D_MIX - SSD_WIDTH - RWKV_WIDTH
S5_GROUP_CH = 16
S5_GROUPS = S5_WIDTH // S5_GROUP_CH
S5_STATE = 64
D_FF = 4 * D_MODEL
NORM_EPS = 1e-6
IN_SPLITS = (SSD_WIDTH, SSD_WIDTH + SSD_CONV_DIM, SSD_WIDTH + SSD_CONV_DIM + SSD_HEADS, SSD_WIDTH + SSD_CONV_DIM + SSD_HEADS + RWKV_PROJ)
IN_COLS = IN_SPLITS[-1] + S5_WIDTH

kernel_name = 'hybrid_ssd_rwkv7_s5_adaln_step'


def _rmsnorm(x, g):
    xf = x.astype(jnp.float32)
    y = xf * lax.rsqrt(jnp.mean(jnp.square(xf), axis=-1, keepdims=True) + NORM_EPS)
    return (y * g.astype(jnp.float32)).astype(x.dtype)


def _causal_conv(u, buf, w, b):
    seqlen = u.shape[1]
    full = jnp.concatenate([buf.astype(u.dtype), u], axis=1)
    out = b + sum(full[:, j:j + seqlen] * w[j] for j in range(SSD_CONV))
    return out, full[:, seqlen:]


def _segsum(x):
    T = x.shape[-1]
    xr = jnp.broadcast_to(x[..., :, None], x.shape + (T,))
    strict = jnp.tril(jnp.ones((T, T), bool), -1)
    ss = jnp.cumsum(jnp.where(strict, xr, 0.0), axis=-2)
    return jnp.where(jnp.tril(jnp.ones((T, T), bool)), ss, -jnp.inf)


def _ssd_chunked(xdt, a, bm, cm, h0):
    bsz, seqlen, nh, hp = xdt.shape
    ng, ns = bm.shape[2], bm.shape[3]
    rg = nh // ng
    T = math.gcd(seqlen, SSD_CHUNK)
    nc = seqlen // T
    x = xdt.reshape(bsz, nc, T, ng, rg, hp)
    a = a.reshape(bsz, nc, T, ng, rg).transpose(0, 3, 4, 1, 2)
    bm = bm.reshape(bsz, nc, T, ng, ns)
    cm = cm.reshape(bsz, nc, T, ng, ns)
    a_cs = jnp.cumsum(a, axis=-1)
    decay_in = jnp.exp(_segsum(a))
    cb = jnp.einsum('bclgn,bcsgn->bgcls', cm, bm)
    y_diag = jnp.einsum('bgcls,bgrcls,bcsgrp->bclgrp', cb, decay_in, x)
    decay_to_end = jnp.exp(a_cs[..., -1:] - a_cs)
    chunk_states = jnp.einsum('bcsgn,bgrcs,bcsgrp->bcgrpn', bm, decay_to_end, x)
    states = jnp.concatenate([h0.reshape(bsz, 1, ng, rg, hp, ns), chunk_states], axis=1)
    chunk_tot = jnp.pad(a_cs[..., -1], ((0, 0), (0, 0), (0, 0), (1, 0)))
    decay_chunk = jnp.exp(_segsum(chunk_tot))
    states = jnp.einsum('bgrzc,bcgrpn->bzgrpn', decay_chunk, states)
    enter, final = states[:, :-1], states[:, -1]
    y_off = jnp.einsum('bclgn,bcgrpn,bgrcl->bclgrp', cm, enter, jnp.exp(a_cs))
    y = (y_diag + y_off).reshape(bsz, seqlen, nh, hp)
    return y, final.reshape(bsz, nh, hp, ns)


def _ssd_mixer(z, xbc, dt, st_ssd, st_conv, l, P):
    f32 = jnp.float32
    bsz, seqlen = z.shape[:2]
    xbc, new_conv = _causal_conv(xbc.astype(f32), st_conv, P['ssd_conv_w'][l], P['ssd_conv_b'][l])
    xbc = jax.nn.silu(xbc)
    xs, bm, cm = jnp.split(xbc, [SSD_WIDTH, SSD_WIDTH + SSD_GROUPS * SSD_STATE], axis=-1)
    dt = jax.nn.softplus(dt.astype(f32) + P['ssd_dt_bias'][l])
    a = -jnp.exp(P['ssd_a_log'][l].astype(f32))
    xh = xs.reshape(bsz, seqlen, SSD_HEADS, SSD_HEAD_DIM)
    y, new_state = _ssd_chunked(xh * dt[..., None], dt * a,
                                bm.reshape(bsz, seqlen, SSD_GROUPS, SSD_STATE),
                                cm.reshape(bsz, seqlen, SSD_GROUPS, SSD_STATE),
                                st_ssd.astype(f32))
    y = y + xh * P['ssd_d'][l][:, None]
    y = y.reshape(bsz, seqlen, SSD_WIDTH) * jax.nn.silu(z.astype(f32))
    y = _rmsnorm(y, P['ssd_norm_g'][l])
    return y, new_state.astype(st_ssd.dtype), new_conv.astype(st_conv.dtype)


def _rwkv_step(S, inp):
    r_t, w_t, k_t, v_t, a_t, b_t = inp
    sa = jnp.einsum('bhvk,bhk->bhv', S, a_t)
    S = S * w_t[:, :, None, :] + sa[..., None] * b_t[:, :, None, :] + v_t[..., None] * k_t[:, :, None, :]
    return S, jnp.einsum('bhvk,bhk->bhv', S, r_t)


def _rwkv_mixer(f, st_rwkv, st_shift, l, P):
    f32 = jnp.float32
    f = f.astype(f32)
    bsz, seqlen = f.shape[:2]
    prev = jnp.concatenate([st_shift.astype(f32)[:, None], f[:, :-1]], axis=1)
    fm = f + (prev - f) * P['rwkv_mu'][l]
    r, k, v, wl, al, gl = jnp.split(fm, [RWKV_WIDTH, 2 * RWKV_WIDTH, 3 * RWKV_WIDTH,
                                         3 * RWKV_WIDTH + RWKV_DECAY_LORA,
                                         3 * RWKV_WIDTH + RWKV_DECAY_LORA + RWKV_A_LORA], axis=-1)
    w_log = -jax.nn.softplus(-(P['rwkv_w0'][l] + jnp.tanh(wl) @ P['rwkv_w2'][l])) - 0.5
    decay = jnp.exp(-jnp.exp(w_log))
    a = jax.nn.sigmoid(P['rwkv_a0'][l] + al @ P['rwkv_a2'][l])
    g = jax.nn.sigmoid(gl) @ P['rwkv_g2'][l]
    hs = lambda t: t.reshape(bsz, seqlen, RWKV_HEADS, RWKV_HEAD_DIM)
    r, k, v, a, decay = hs(r), hs(k), hs(v), hs(a), hs(decay)
    k_k = P['rwkv_k_k'][l].reshape(RWKV_HEADS, RWKV_HEAD_DIM)
    k_a = P['rwkv_k_a'][l].reshape(RWKV_HEADS, RWKV_HEAD_DIM)
    kk = k * k_k
    kk = kk * lax.rsqrt(jnp.maximum(jnp.sum(jnp.square(kk), axis=-1, keepdims=True), 1e-24))
    k = k * (1.0 + (a - 1.0) * k_a)
    seq_first = lambda t: jnp.moveaxis(t, 1, 0)
    s_last, o = lax.scan(_rwkv_step, st_rwkv.astype(f32),
                         (seq_first(r), seq_first(decay), seq_first(k), seq_first(v),
                          seq_first(-kk), seq_first(kk * a)))
    o = jnp.moveaxis(o, 0, 1)
    mu = jnp.mean(o, axis=-1, keepdims=True)
    var = jnp.mean(jnp.square(o - mu), axis=-1, keepdims=True)
    o = ((o - mu) * lax.rsqrt(var + RWKV_LN_EPS)).reshape(bsz, seqlen, RWKV_WIDTH)
    o = o * P['rwkv_ln_g'][l] + P['rwkv_ln_b'][l]
    bonus = jnp.sum(r * k * P['rwkv_r_k'][l], axis=-1, keepdims=True) * v
    o = o + bonus.reshape(bsz, seqlen, RWKV_WIDTH)
    return o * g, s_last.astype(st_rwkv.dtype), f[:, -1].astype(st_shift.dtype)


def _complex_affine_combine(e1, e2):
    a1r, a1i, b1r, b1i = e1
    a2r, a2i, b2r, b2i = e2
    return (a2r * a1r - a2i * a1i, a2r * a1i + a2i * a1r,
            a2r * b1r - a2i * b1i + b2r, a2r * b1i + a2i * b1r + b2i)


def _s5_mixer(u, st_re, st_im, l, P):
    f32 = jnp.float32
    u = u.astype(f32)
    bsz, seqlen = u.shape[:2]
    ug = u.reshape(bsz, seqlen, S5_GROUPS, S5_GROUP_CH)
    lre = P['s5_a_re'][l].astype(f32)
    lim = P['s5_a_im'][l].astype(f32)
    dt = jnp.exp(P['s5_log_dt'][l].astype(f32))[:, None]
    mag = jnp.exp(lre * dt)
    ang = lim * dt
    ab_re, ab_im = mag * jnp.cos(ang), mag * jnp.sin(ang)
    den = jnp.square(lre) + jnp.square(lim)
    q_re = ((ab_re - 1.0) * lre + ab_im * lim) / den
    q_im = (ab_im * lre - (ab_re - 1.0) * lim) / den
    b_re, b_im = P['s5_b_re'][l], P['s5_b_im'][l]
    bb_re = q_re[..., None] * b_re - q_im[..., None] * b_im
    bb_im = q_re[..., None] * b_im + q_im[..., None] * b_re
    bu_re = jnp.einsum('blgc,gpc->blgp', ug, bb_re)
    bu_im = jnp.einsum('blgc,gpc->blgp', ug, bb_im)
    s_re, s_im = st_re.astype(f32), st_im.astype(f32)
    bu_re = bu_re.at[:, 0].add(ab_re * s_re - ab_im * s_im)
    bu_im = bu_im.at[:, 0].add(ab_re * s_im + ab_im * s_re)
    _, _, h_re, h_im = lax.associative_scan(
        _complex_affine_combine,
        (jnp.broadcast_to(ab_re, bu_re.shape), jnp.broadcast_to(ab_im, bu_re.shape), bu_re, bu_im), axis=1)
    y = (jnp.einsum('gcp,blgp->blgc', P['s5_c_re'][l], h_re)
         - jnp.einsum('gcp,blgp->blgc', P['s5_c_im'][l], h_im))
    y = y.reshape(bsz, seqlen, S5_WIDTH) + P['s5_d'][l] * u
    y = jax.nn.gelu(y)
    y = y * jax.nn.sigmoid(y @ P['s5_glu_w'][l] + P['s5_glu_b'][l])
    return y, h_re[:, -1].astype(st_re.dtype), h_im[:, -1].astype(st_im.dtype)


def _layer(x, c, l, st_ssd, st_conv, st_rwkv, st_shift, st_re, st_im, P):
    mod = jax.nn.silu(c) @ P['ada_w'][l] + P['ada_b'][l]
    sh1, sc1, g1, sh2, sc2, g2 = jnp.split(mod[:, None, :], 6, axis=-1)
    h = _rmsnorm(x, P['norm1_g'][l]) * (1.0 + sc1) + sh1
    proj = h @ P['w_in'][l]
    z, xbc, dt, frw, us5 = jnp.split(proj, list(IN_SPLITS), axis=-1)
    y_ssd, n_ssd, n_conv = _ssd_mixer(z, xbc, dt, st_ssd, st_conv, l, P)
    y_rw, n_rwkv, n_shift = _rwkv_mixer(frw, st_rwkv, st_shift, l, P)
    y_s5, n_re, n_im = _s5_mixer(us5, st_re, st_im, l, P)
    mix = jnp.concatenate([y_ssd, y_rw.astype(x.dtype), y_s5.astype(x.dtype)], axis=-1).astype(x.dtype) @ P['w_out'][l]
    x = x + g1 * mix
    h2 = _rmsnorm(x, P['norm2_g'][l]) * (1.0 + sc2) + sh2
    ff = jnp.square(jax.nn.relu(h2 @ P['mlp_w1'][l])) @ P['mlp_w2'][l]
    x = x + g2 * ff
    return x, (n_ssd, n_conv, n_rwkv, n_shift, n_re, n_im)


def _trunk(x, c, st_ssd, st_conv, st_rwkv, st_shift, st_re, st_im, P):
    new = ([], [], [], [], [], [])
    for l in range(DEPTH):
        x, states = _layer(x, c, l, st_ssd[l], st_conv[l], st_rwkv[l], st_shift[l], st_re[l], st_im[l], P)
        for lst, s in zip(new, states):
            lst.append(s)
    y = _rmsnorm(x, P['final_g'])
    return y, [jnp.stack(lst) for lst in new]


def setup_inputs(seed: int = 0) -> dict:
    key = jax.random.key(seed)
    ks = iter(jax.random.split(key, 64))

    def nrm(shape, scale):
        return scale * jax.random.normal(next(ks), shape, jnp.float32)

    def unif(shape, lo, hi):
        return jax.random.uniform(next(ks), shape, jnp.float32, lo, hi)

    L = DEPTH
    dt0 = jnp.exp(unif((L, SSD_HEADS), math.log(1e-3), math.log(1e-1)))
    return {
        'x_prompt': nrm((BATCH, SEQ, D_MODEL), 1.0),
        'x_sample': nrm((DEC_BATCH, DEC_SEQ, D_MODEL), 1.0),
        'c_prompt': nrm((BATCH, D_MODEL), 1.0),
        'c_sample': nrm((DEC_BATCH, D_MODEL), 1.0),
        'state_ssd': nrm((L, DEC_BATCH, SSD_HEADS, SSD_HEAD_DIM, SSD_STATE), 0.1),
        'state_ssd_conv': nrm((L, DEC_BATCH, SSD_CONV - 1, SSD_CONV_DIM), 1.0),
        'state_rwkv': nrm((L, DEC_BATCH, RWKV_HEADS, RWKV_HEAD_DIM, RWKV_HEAD_DIM), 0.1),
        'state_rwkv_shift': nrm((L, DEC_BATCH, RWKV_PROJ), 1.0),
        'state_s5_re': nrm((L, DEC_BATCH, S5_GROUPS, S5_STATE), 0.05),
        'state_s5_im': nrm((L, DEC_BATCH, S5_GROUPS, S5_STATE), 0.05),
        'ada_w': nrm((L, D_MODEL, 6 * D_MODEL), 0.5 * D_MODEL ** -0.5),
        'ada_b': nrm((L, 6 * D_MODEL), 0.01),
        'norm1_g': 1.0 + nrm((L, D_MODEL), 0.1),
        'norm2_g': 1.0 + nrm((L, D_MODEL), 0.1),
        'w_in': nrm((L, D_MODEL, IN_COLS), D_MODEL ** -0.5),
        'ssd_conv_w': nrm((L, SSD_CONV, SSD_CONV_DIM), 0.5),
        'ssd_conv_b': nrm((L, SSD_CONV_DIM), 0.01),
        'ssd_dt_bias': dt0 + jnp.log(-jnp.expm1(-dt0)),
        'ssd_a_log': jnp.log(unif((L, SSD_HEADS), 1.0, 16.0)),
        'ssd_d': 1.0 + nrm((L, SSD_HEADS), 0.1),
        'ssd_norm_g': 1.0 + nrm((L, SSD_WIDTH), 0.1),
        'rwkv_mu': unif((L, RWKV_PROJ), 0.0, 1.0),
        'rwkv_w0': unif((L, RWKV_WIDTH), -6.0, 0.0),
        'rwkv_w2': nrm((L, RWKV_DECAY_LORA, RWKV_WIDTH), 0.1),
        'rwkv_a0': nrm((L, RWKV_WIDTH), 0.1),
        'rwkv_a2': nrm((L, RWKV_A_LORA, RWKV_WIDTH), 0.5 * RWKV_A_LORA ** -0.5),
        'rwkv_g2': nrm((L, RWKV_GATE_LORA, RWKV_WIDTH), RWKV_GATE_LORA ** -0.5),
        'rwkv_k_k': 1.0 + nrm((L, RWKV_WIDTH), 0.1),
        'rwkv_k_a': 1.0 + nrm((L, RWKV_WIDTH), 0.1),
        'rwkv_r_k': nrm((L, RWKV_HEADS, RWKV_HEAD_DIM), 0.1),
        'rwkv_ln_g': 1.0 + nrm((L, RWKV_WIDTH), 0.1),
        'rwkv_ln_b': nrm((L, RWKV_WIDTH), 0.01),
        's5_a_re': -0.5 + nrm((L, S5_GROUPS, S5_STATE), 0.01),
        's5_a_im': jnp.pi * jnp.arange(S5_STATE, dtype=jnp.float32) + nrm((L, S5_GROUPS, S5_STATE), 0.01),
        's5_log_dt': unif((L, S5_GROUPS), math.log(1e-3), math.log(1e-1)),
        's5_b_re': nrm((L, S5_GROUPS, S5_STATE, S5_GROUP_CH), (2 * S5_GROUP_CH) ** -0.5),
        's5_b_im': nrm((L, S5_GROUPS, S5_STATE, S5_GROUP_CH), (2 * S5_GROUP_CH) ** -0.5),
        's5_c_re': nrm((L, S5_GROUPS, S5_GROUP_CH, S5_STATE), 0.25),
        's5_c_im': nrm((L, S5_GROUPS, S5_GROUP_CH, S5_STATE), 0.25),
        's5_d': nrm((L, S5_WIDTH), 1.0),
        's5_glu_w': nrm((L, S5_WIDTH, S5_WIDTH), S5_WIDTH ** -0.5),
        's5_glu_b': nrm((L, S5_WIDTH), 0.01),
        'w_out': nrm((L, D_MIX, D_MODEL), D_MIX ** -0.5),
        'mlp_w1': nrm((L, D_MODEL, D_FF), D_MODEL ** -0.5),
        'mlp_w2': nrm((L, D_FF, D_MODEL), D_FF ** -0.5),
        'final_g': 1.0 + nrm((D_MODEL,), 0.1),
    }


def reference(x_prompt, x_sample, c_prompt, c_sample, state_ssd, state_ssd_conv, state_rwkv, state_rwkv_shift,
              state_s5_re, state_s5_im, ada_w, ada_b, norm1_g, norm2_g, w_in, ssd_conv_w, ssd_conv_b, ssd_dt_bias,
              ssd_a_log, ssd_d, ssd_norm_g, rwkv_mu, rwkv_w0, rwkv_w2, rwkv_a0, rwkv_a2, rwkv_g2, rwkv_k_k, rwkv_k_a,
              rwkv_r_k, rwkv_ln_g, rwkv_ln_b, s5_a_re, s5_a_im, s5_log_dt, s5_b_re, s5_b_im, s5_c_re, s5_c_im, s5_d,
              s5_glu_w, s5_glu_b, w_out, mlp_w1, mlp_w2, final_g):
    P = dict(ada_w=ada_w, ada_b=ada_b, norm1_g=norm1_g, norm2_g=norm2_g, w_in=w_in,
             ssd_conv_w=ssd_conv_w, ssd_conv_b=ssd_conv_b, ssd_dt_bias=ssd_dt_bias, ssd_a_log=ssd_a_log,
             ssd_d=ssd_d, ssd_norm_g=ssd_norm_g, rwkv_mu=rwkv_mu, rwkv_w0=rwkv_w0, rwkv_w2=rwkv_w2,
             rwkv_a0=rwkv_a0, rwkv_a2=rwkv_a2, rwkv_g2=rwkv_g2, rwkv_k_k=rwkv_k_k, rwkv_k_a=rwkv_k_a,
             rwkv_r_k=rwkv_r_k, rwkv_ln_g=rwkv_ln_g, rwkv_ln_b=rwkv_ln_b, s5_a_re=s5_a_re, s5_a_im=s5_a_im,
             s5_log_dt=s5_log_dt, s5_b_re=s5_b_re, s5_b_im=s5_b_im, s5_c_re=s5_c_re, s5_c_im=s5_c_im,
             s5_d=s5_d, s5_glu_w=s5_glu_w, s5_glu_b=s5_glu_b, w_out=w_out, mlp_w1=mlp_w1, mlp_w2=mlp_w2,
             final_g=final_g)
    bp = x_prompt.shape[0]
    zeros = lambda shape: jnp.zeros((DEPTH, bp) + shape, x_prompt.dtype)
    y_prompt, sp = _trunk(x_prompt, c_prompt,
                          zeros((SSD_HEADS, SSD_HEAD_DIM, SSD_STATE)), zeros((SSD_CONV - 1, SSD_CONV_DIM)),
                          zeros((RWKV_HEADS, RWKV_HEAD_DIM, RWKV_HEAD_DIM)), zeros((RWKV_PROJ,)),
                          zeros((S5_GROUPS, S5_STATE)), zeros((S5_GROUPS, S5_STATE)), P)
    y_sample, ss = _trunk(x_sample, c_sample, state_ssd, state_ssd_conv, state_rwkv, state_rwkv_shift,
                          state_s5_re, state_s5_im, P)
    p_ssd, p_conv, p_rwkv, p_shift, p_s5_re, p_s5_im = sp
    s_ssd, s_conv, s_rwkv, s_shift, s_s5_re, s_s5_im = ss
    return (y_prompt, y_sample, p_ssd, p_conv, p_rwkv, p_shift, p_s5_re, p_s5_im,
            s_ssd, s_conv, s_rwkv, s_shift, s_s5_re, s_s5_im)
```

```python
import functools

import jax
import jax.numpy as jnp
from jax import lax
from jax.experimental import pallas as pl
from jax.experimental.pallas import tpu as pltpu

D_MODEL = 1024
DEPTH = 2
SSD_WIDTH = 512
SSD_HEAD_DIM = 64
SSD_HEADS = 8
SSD_GROUPS = 2
SSD_HEADS_PER_GROUP = SSD_HEADS // SSD_GROUPS
SSD_STATE = 128
SSD_CONV = 4
SSD_CONV_DIM = SSD_WIDTH + 2 * SSD_GROUPS * SSD_STATE
RWKV_WIDTH = 256
RWKV_HEAD_DIM = 64
RWKV_HEADS = 4
RWKV_DECAY_LORA = 64
RWKV_A_LORA = 64
RWKV_GATE_LORA = 128
RWKV_PROJ = 3 * RWKV_WIDTH + RWKV_DECAY_LORA + RWKV_A_LORA + RWKV_GATE_LORA
RWKV_LN_EPS = 64e-5
S5_WIDTH = 256
S5_GROUP_CH = 16
S5_GROUPS = 16
S5_STATE = 64
S5_FLAT = S5_GROUPS * S5_STATE
D_FF = 4 * D_MODEL
NORM_EPS = 1e-6
DT_PAD = 128
SUBLANES = 8

_Z0, _XBC0, _DT0, _FRW0, _US50 = 0, SSD_WIDTH, SSD_WIDTH + SSD_CONV_DIM, SSD_WIDTH + SSD_CONV_DIM + SSD_HEADS, \
    SSD_WIDTH + SSD_CONV_DIM + SSD_HEADS + RWKV_PROJ
_IN_COLS = _US50 + S5_WIDTH
_PZ, _PXBC, _PFRW, _PUS5, _PDT = 0, 512, 1536, 2560, 2816
_P_COLS = _PDT + DT_PAD

SSD_CHUNK = 128
RWKV_CHUNK = 64
S5_TILE = 256
ROW_TILE = 512
FF_TILE = 1024
VMEM_LIMIT = 56 * 1024 * 1024

_HI = lax.Precision.HIGHEST
_F32 = jnp.float32
_BF16 = jnp.bfloat16


def _dot(a, b, precision=None):
    return jnp.dot(a, b, preferred_element_type=_F32, precision=precision)


def _dot_nt(a, b, precision=None):
    return lax.dot_general(a, b, (((1,), (1,)), ((), ())), preferred_element_type=_F32, precision=precision)


def _dot_tn(a, b, precision=None):
    return lax.dot_general(a, b, (((0,), (0,)), ((), ())), preferred_element_type=_F32, precision=precision)


def _iota(shape, dim):
    return lax.broadcasted_iota(jnp.int32, shape, dim)


def _sigmoid(x):
    return 1.0 / (1.0 + jnp.exp(-x))


def _softplus(x):
    return jnp.maximum(x, 0.0) + jnp.log1p(jnp.exp(-jnp.abs(x)))


def _silu(x):
    return x * _sigmoid(x)


def _params(*sems):
    return pltpu.CompilerParams(dimension_semantics=sems, vmem_limit_bytes=VMEM_LIMIT)


def _ada_kernel(c_ref, w_ref, b_ref, o_ref):
    c = c_ref[...]
    s = _silu(c).astype(_BF16)
    o_ref[0] = _dot(s, w_ref[0].astype(_BF16)) + b_ref[0]


def _ada(c_all, ada_w, ada_b):
    rows = c_all.shape[0]
    tn = 1536
    return pl.pallas_call(
        _ada_kernel,
        grid=(DEPTH, 6 * D_MODEL // tn),
        in_specs=[pl.BlockSpec((rows, D_MODEL), lambda l, j: (0, 0)),
                  pl.BlockSpec((1, D_MODEL, tn), lambda l, j: (l, 0, j)),
                  pl.BlockSpec((1, 1, tn), lambda l, j: (l, 0, j))],
        out_specs=pl.BlockSpec((1, rows, tn), lambda l, j: (l, 0, j)),
        out_shape=jax.ShapeDtypeStruct((DEPTH, rows, 6 * D_MODEL), _F32),
        compiler_params=_params("arbitrary", "arbitrary"),
    )(c_all, ada_w, ada_b.reshape(DEPTH, 1, 6 * D_MODEL))


def _in_proj_kernel(x_ref, sh_ref, sc_ref, g_ref, w_ref, z_ref, xbc_ref, frw_ref, us5_ref, dt_ref):
    x = x_ref[...]
    bb, tl, _ = x.shape
    xn = x * lax.rsqrt(jnp.mean(x * x, axis=-1, keepdims=True) + NORM_EPS)
    h = xn * g_ref[...] * (1.0 + sc_ref[...]) + sh_ref[...]
    h = h.reshape(bb * tl, D_MODEL).astype(_BF16)
    for o_ref, lo, hi in ((z_ref, _PZ, _PXBC), (xbc_ref, _PXBC, _PFRW), (frw_ref, _PFRW, _PUS5),
                          (us5_ref, _PUS5, _PDT), (dt_ref, _PDT, _P_COLS)):
        o_ref[...] = _dot(h, w_ref[:, lo:hi]).reshape(o_ref.shape)


def _in_proj(x, mod, norm_g, w_in_p, *, bb, tl, time_major_s5):
    nb, L, _ = x.shape
    grid = (nb // bb, L // tl)
    xmap = lambda i, t: (i, t, 0)

    def modspec(j):
        return pl.BlockSpec((bb, 1, D_MODEL), lambda i, t: (i, 0, j))

    def out(n):
        return jax.ShapeDtypeStruct((nb, L, n), _F32), pl.BlockSpec((bb, tl, n), xmap)

    outs = [out(SSD_WIDTH), out(SSD_CONV_DIM), out(RWKV_PROJ), out(S5_WIDTH), out(DT_PAD)]
    if time_major_s5:
        assert bb == 1
        outs[3] = (jax.ShapeDtypeStruct((L, nb * S5_WIDTH), _F32), pl.BlockSpec((tl, S5_WIDTH), lambda i, t: (t, i)))
    return pl.pallas_call(
        _in_proj_kernel,
        grid=grid,
        in_specs=[pl.BlockSpec((bb, tl, D_MODEL), xmap), modspec(0), modspec(1),
                  pl.BlockSpec((1, D_MODEL), lambda i, t: (0, 0)),
                  pl.BlockSpec((D_MODEL, _P_COLS), lambda i, t: (0, 0))],
        out_specs=[o[1] for o in outs],
        out_shape=[o[0] for o in outs],
        compiler_params=_params("arbitrary", "arbitrary"),
    )(x, mod, mod, norm_g.reshape(1, D_MODEL), w_in_p)


def _ssd_seq(j, z_ref, xbc_ref, dt_ref, cw_ref, cb_ref, dtb_ref, alog_ref, dexp_ref, ng_ref, eexp_ref,
             y_ref, sto_ref, cvo_ref, ext_scr, ybuf_scr, T, valid):
    u = xbc_ref[j]
    ext_scr[0:SUBLANES, :] = cvo_ref[j]
    ext_scr[SUBLANES:SUBLANES + T, :] = u
    conv = cb_ref[...] + cw_ref[3:4, :] * u
    for back in (1, 2, 3):
        conv = conv + cw_ref[3 - back:4 - back, :] * ext_scr[SUBLANES - back:SUBLANES - back + T, :]
    cvo_ref[j] = ext_scr[valid:valid + SUBLANES, :]
    xbc = _silu(conv)
    xs = xbc[:, :SSD_WIDTH]
    row = _iota((T, 1), 0)
    dtv = _softplus(dt_ref[j] + dtb_ref[...])
    if valid < T:
        dtv = jnp.where(row < valid, dtv, 0.0)
    a_row = jnp.where(_iota((1, DT_PAD), 1) < SSD_HEADS, -jnp.exp(alog_ref[...]), 0.0)
    a = dtv * a_row
    tri_b = _iota((T, T), 0) >= _iota((T, T), 1)
    tri = tri_b.astype(_F32)
    cs = _dot(tri, a, _HI)
    eye = (_iota((DT_PAD, DT_PAD), 0) == _iota((DT_PAD, DT_PAD), 1)).astype(_F32)
    cs_t = _dot_nt(eye, cs, _HI)
    xdt = xs * _dot(dtv, eexp_ref[...], _HI)
    ecs = jnp.exp(cs)
    cs_last = cs[T - 1:T, :]
    dte = jnp.exp(cs_last - cs)
    elast = jnp.exp(cs_last)
    for g in range(SSD_GROUPS):
        bm = xbc[:, SSD_WIDTH + g * SSD_STATE:SSD_WIDTH + (g + 1) * SSD_STATE]
        cm = xbc[:, SSD_WIDTH + (SSD_GROUPS + g) * SSD_STATE:SSD_WIDTH + (SSD_GROUPS + g + 1) * SSD_STATE]
        cb = _dot_nt(cm, bm)
        for r in range(SSD_HEADS_PER_GROUP):
            h = g * SSD_HEADS_PER_GROUP + r
            lo, hi = h * SSD_HEAD_DIM, (h + 1) * SSD_HEAD_DIM
            decay = jnp.exp(jnp.where(tri_b, cs[:, h:h + 1] - cs_t[h:h + 1, :], -jnp.inf))
            xdt_h = xdt[:, lo:hi]
            st = sto_ref[j, h]
            y_h = _dot(cb * decay, xdt_h) + ecs[:, h:h + 1] * _dot_nt(cm, st)
            ybuf_scr[:, lo:hi] = y_h
            sto_ref[j, h] = elast[:, h:h + 1] * st + _dot_tn(xdt_h * dte[:, h:h + 1], bm)
    y = ybuf_scr[...] + xs * dexp_ref[...]
    y = y * _silu(z_ref[j])
    y = y * lax.rsqrt(jnp.mean(y * y, axis=-1, keepdims=True) + NORM_EPS) * ng_ref[...]
    y_ref[j] = y


def _ssd_kernel(z_ref, xbc_ref, dt_ref, st_ref, cv_ref, cw_ref, cb_ref, dtb_ref, alog_ref, dexp_ref, ng_ref,
                eexp_ref, y_ref, sto_ref, cvo_ref, ext_scr, ybuf_scr, *, bb, T, valid):
    @pl.when(pl.program_id(1) == 0)
    def _():
        sto_ref[...] = st_ref[...]
        cvo_ref[...] = cv_ref[...]

    seq = functools.partial(_ssd_seq, z_ref=z_ref, xbc_ref=xbc_ref, dt_ref=dt_ref, cw_ref=cw_ref, cb_ref=cb_ref,
                            dtb_ref=dtb_ref, alog_ref=alog_ref, dexp_ref=dexp_ref, ng_ref=ng_ref, eexp_ref=eexp_ref,
                            y_ref=y_ref, sto_ref=sto_ref, cvo_ref=cvo_ref, ext_scr=ext_scr, ybuf_scr=ybuf_scr,
                            T=T, valid=valid)
    if bb == 1:
        seq(0)
    else:
        def body(j, carry):
            seq(j)
            return carry
        lax.fori_loop(0, bb, body, 0)


def _ssd(z, xbc, dt, st_ssd, st_conv8, conv_w, conv_b, dt_bias, a_log, d, norm_g, *, bb, T, valid):
    nb, L, _ = z.shape
    grid = (nb // bb, L // T)
    tmap = lambda i, c: (i, c, 0)
    cmap2 = lambda i, c: (0, 0)
    smap = lambda i, c: (i, 0, 0, 0)
    cvmap = lambda i, c: (i, 0, 0)
    pad = DT_PAD - SSD_HEADS
    eexp = jnp.repeat(jnp.eye(DT_PAD, SSD_HEADS, dtype=_F32), SSD_HEAD_DIM, axis=1)
    return pl.pallas_call(
        functools.partial(_ssd_kernel, bb=bb, T=T, valid=valid),
        grid=grid,
        in_specs=[pl.BlockSpec((bb, T, SSD_WIDTH), tmap), pl.BlockSpec((bb, T, SSD_CONV_DIM), tmap),
                  pl.BlockSpec((bb, T, DT_PAD), tmap),
                  pl.BlockSpec((bb, SSD_HEADS, SSD_HEAD_DIM, SSD_STATE), smap),
                  pl.BlockSpec((bb, SUBLANES, SSD_CONV_DIM), cvmap),
                  pl.BlockSpec((SSD_CONV, SSD_CONV_DIM), cmap2), pl.BlockSpec((1, SSD_CONV_DIM), cmap2),
                  pl.BlockSpec((1, DT_PAD), cmap2), pl.BlockSpec((1, DT_PAD), cmap2),
                  pl.BlockSpec((1, SSD_WIDTH), cmap2), pl.BlockSpec((1, SSD_WIDTH), cmap2),
                  pl.BlockSpec((DT_PAD, SSD_WIDTH), cmap2)],
        out_specs=[pl.BlockSpec((bb, T, SSD_WIDTH), tmap),
                   pl.BlockSpec((bb, SSD_HEADS, SSD_HEAD_DIM, SSD_STATE), smap),
                   pl.BlockSpec((bb, SUBLANES, SSD_CONV_DIM), cvmap)],
        out_shape=[jax.ShapeDtypeStruct((nb, L, SSD_WIDTH), _F32),
                   jax.ShapeDtypeStruct(st_ssd.shape, _F32),
                   jax.ShapeDtypeStruct(st_conv8.shape, _F32)],
        scratch_shapes=[pltpu.VMEM((T + SUBLANES, SSD_CONV_DIM), _F32), pltpu.VMEM((T, SSD_WIDTH), _F32)],
        compiler_params=_params("arbitrary", "arbitrary"),
    )(z, xbc, dt, st_ssd, st_conv8, conv_w, conv_b.reshape(1, -1),
      jnp.pad(dt_bias, (0, pad)).reshape(1, DT_PAD), jnp.pad(a_log, (0, pad)).reshape(1, DT_PAD),
      jnp.repeat(d, SSD_HEAD_DIM).reshape(1, SSD_WIDTH), norm_g.reshape(1, SSD_WIDTH), eexp)


def _stack_heads(x):
    return jnp.concatenate([x[:, h * RWKV_HEAD_DIM:(h + 1) * RWKV_HEAD_DIM] for h in range(RWKV_HEADS)], axis=0)


def _unstack_heads(x, T):
    return jnp.concatenate([x[h * T:(h + 1) * T, :] for h in range(RWKV_HEADS)], axis=1)


def _rwkv_seq(j, f_ref, mu_ref, w0_ref, w2_ref, a0_ref, a2_ref, g2_ref, kk_ref, ka_ref, rk_ref, lng_ref, lnb_ref,
              y_ref, so_ref, sho_ref, ext_scr, T, valid):
    HT = RWKV_HEADS * T
    f = f_ref[j]
    ext_scr[0:SUBLANES, :] = sho_ref[j]
    ext_scr[SUBLANES:SUBLANES + T, :] = f
    prev = ext_scr[SUBLANES - 1:SUBLANES - 1 + T, :]
    sho_ref[j] = ext_scr[valid:valid + SUBLANES, :]
    fm = f + (prev - f) * mu_ref[...]
    r = fm[:, 0:256]
    k = fm[:, 256:512]
    v = fm[:, 512:768]
    lora = fm[:, 768:896]
    gl = fm[:, 896:1024]
    w_log = -_softplus(-(w0_ref[...] + _dot(jnp.tanh(lora), w2_ref[...]))) - 0.5
    logw = -jnp.exp(w_log)
    a = _sigmoid(a0_ref[...] + _dot(lora, a2_ref[...]))
    g = _dot(_sigmoid(gl), g2_ref[...])
    k2 = k * (1.0 + (a - 1.0) * ka_ref[...])
    kk4 = _stack_heads(k * kk_ref[...])
    kk4 = kk4 * lax.rsqrt(jnp.maximum(jnp.sum(kk4 * kk4, axis=-1, keepdims=True), 1e-24))
    kkn = _unstack_heads(kk4, T)
    if valid < T:
        live = _iota((T, 1), 0) < valid
        logw = jnp.where(live, logw, 0.0)
        kkn = jnp.where(live, kkn, 0.0)
        k2 = jnp.where(live, k2, 0.0)
    tri = (_iota((T, T), 0) >= _iota((T, T), 1)).astype(_F32)
    cw = _dot(tri, logw, _HI)
    w_inc = jnp.exp(cw)
    w_exc = jnp.exp(cw - logw)
    w_inv = jnp.exp(-cw)
    w_end = w_inc[T - 1:T, :]
    at = -kkn * w_exc
    rt = r * w_inc
    bt = kkn * a * w_inv
    kt = k2 * w_inv
    a4, r4, b4, k4, v4 = (_stack_heads(t) for t in (at, rt, bt, kt, v))

    rr = _iota((HT, HT), 0)
    cc = _iota((HT, HT), 1)
    same = (rr // T) == (cc // T)
    strict = jnp.logical_and(same, rr > cc)
    incl = jnp.logical_and(same, rr >= cc)
    n_ab = jnp.where(strict, _dot_nt(a4, b4, _HI), 0.0)
    n_ak = jnp.where(strict, _dot_nt(a4, k4, _HI), 0.0)
    m_rb = jnp.where(incl, _dot_nt(r4, b4, _HI), 0.0)
    m_rk = jnp.where(incl, _dot_nt(r4, k4, _HI), 0.0)
    inv = (rr == cc).astype(_F32) + n_ab
    p = n_ab
    for _ in range(max((T - 1).bit_length() - 1, 0)):
        p = _dot(p, p, _HI)
        inv = inv + _dot(inv, p, _HI)

    head_of_row = _iota((HT, RWKV_WIDTH), 0) // T
    head_of_col = _iota((HT, RWKV_WIDTH), 1) // RWKV_HEAD_DIM
    hmask = head_of_row == head_of_col

    def wide(x):
        return jnp.where(hmask, jnp.concatenate([x] * RWKV_HEADS, axis=0), 0.0)

    s0 = so_ref[j]
    u4 = _dot(inv, _dot_nt(wide(at), s0, _HI) + _dot(n_ak, v4, _HI), _HI)
    o4 = _dot_nt(wide(rt), s0, _HI) + _dot(m_rb, u4, _HI) + _dot(m_rk, v4, _HI)
    so_ref[j] = s0 * w_end + _dot_tn(u4, wide(bt * w_end), _HI) + _dot_tn(v4, wide(kt * w_end), _HI)

    mean = jnp.mean(o4, axis=-1, keepdims=True)
    var = jnp.mean(jnp.square(o4 - mean), axis=-1, keepdims=True)
    o4 = (o4 - mean) * lax.rsqrt(var + RWKV_LN_EPS)
    bonus4 = jnp.sum(_stack_heads(r * k2 * rk_ref[...]), axis=-1, keepdims=True) * v4
    o = _unstack_heads(o4, T) * lng_ref[...] + lnb_ref[...] + _unstack_heads(bonus4, T)
    y_ref[j] = o * g


def _rwkv_kernel(f_ref, s_ref, sh_ref, mu_ref, w0_ref, w2_ref, a0_ref, a2_ref, g2_ref, kk_ref, ka_ref, rk_ref,
                 lng_ref, lnb_ref, y_ref, so_ref, sho_ref, ext_scr, *, bb, T, valid):
    @pl.when(pl.program_id(1) == 0)
    def _():
        so_ref[...] = s_ref[...]
        sho_ref[...] = sh_ref[...]

    seq = functools.partial(_rwkv_seq, f_ref=f_ref, mu_ref=mu_ref, w0_ref=w0_ref, w2_ref=w2_ref, a0_ref=a0_ref,
                            a2_ref=a2_ref, g2_ref=g2_ref, kk_ref=kk_ref, ka_ref=ka_ref, rk_ref=rk_ref,
                            lng_ref=lng_ref, lnb_ref=lnb_ref, y_ref=y_ref, so_ref=so_ref, sho_ref=sho_ref,
                            ext_scr=ext_scr, T=T, valid=valid)
    if bb == 1:
        seq(0)
    else:
        def body(j, carry):
            seq(j)
            return carry
        lax.fori_loop(0, bb, body, 0)


def _rwkv(f, s_wide, shift8, mu, w0, w2, a0, a2, g2, k_k, k_a, r_k, ln_g, ln_b, *, bb, T, valid):
    nb, L, _ = f.shape
    grid = (nb // bb, L // T)
    tmap = lambda i, c: (i, c, 0)
    bmap = lambda i, c: (i, 0, 0)
    cmap2 = lambda i, c: (0, 0)
    row = lambda n: pl.BlockSpec((1, n), cmap2)
    lora_w = RWKV_DECAY_LORA + RWKV_A_LORA
    w2_pad = jnp.concatenate([w2, jnp.zeros((RWKV_A_LORA, RWKV_WIDTH), _F32)], axis=0)
    a2_pad = jnp.concatenate([jnp.zeros((RWKV_DECAY_LORA, RWKV_WIDTH), _F32), a2], axis=0)
    return pl.pallas_call(
        functools.partial(_rwkv_kernel, bb=bb, T=T, valid=valid),
        grid=grid,
        in_specs=[pl.BlockSpec((bb, T, RWKV_PROJ), tmap),
                  pl.BlockSpec((bb, RWKV_HEAD_DIM, RWKV_WIDTH), bmap),
                  pl.BlockSpec((bb, SUBLANES, RWKV_PROJ), bmap),
                  row(RWKV_PROJ), row(RWKV_WIDTH), pl.BlockSpec((lora_w, RWKV_WIDTH), cmap2),
                  row(RWKV_WIDTH), pl.BlockSpec((lora_w, RWKV_WIDTH), cmap2),
                  pl.BlockSpec((RWKV_GATE_LORA, RWKV_WIDTH), cmap2),
                  row(RWKV_WIDTH), row(RWKV_WIDTH), row(RWKV_WIDTH), row(RWKV_WIDTH), row(RWKV_WIDTH)],
        out_specs=[pl.BlockSpec((bb, T, RWKV_WIDTH), tmap),
                   pl.BlockSpec((bb, RWKV_HEAD_DIM, RWKV_WIDTH), bmap),
                   pl.BlockSpec((bb, SUBLANES, RWKV_PROJ), bmap)],
        out_shape=[jax.ShapeDtypeStruct((nb, L, RWKV_WIDTH), _F32),
                   jax.ShapeDtypeStruct(s_wide.shape, _F32),
                   jax.ShapeDtypeStruct(shift8.shape, _F32)],
        scratch_shapes=[pltpu.VMEM((T + SUBLANES, RWKV_PROJ), _F32)],
        compiler_params=_params("arbitrary", "arbitrary"),
    )(f, s_wide, shift8, mu.reshape(1, -1), w0.reshape(1, -1), w2_pad, a0.reshape(1, -1), a2_pad, g2,
      k_k.reshape(1, -1), k_a.reshape(1, -1), r_k.reshape(1, -1), ln_g.reshape(1, -1), ln_b.reshape(1, -1))


def _s5_kernel(u_ref, sre_ref, sim_ref, lre_ref, lim_ref, dt_ref, bre_ref, bim_ref, cre_ref, cim_ref, d_ref,
               gw_ref, gb_ref, y_ref, hre_ref, him_ref, hre_scr, him_scr, *, nb, tl, final_step):
    @pl.when(pl.program_id(0) == 0)
    def _():
        hre_ref[...] = sre_ref[...]
        him_ref[...] = sim_ref[...]

    lre, lim, dt = lre_ref[...], lim_ref[...], jnp.exp(dt_ref[...])
    mag = jnp.exp(lre * dt)
    ang = lim * dt
    ab_re, ab_im = mag * jnp.cos(ang), mag * jnp.sin(ang)
    den = jnp.square(lre) + jnp.square(lim)
    q_re = ((ab_re - 1.0) * lre + ab_im * lim) / den
    q_im = (ab_im * lre - (ab_re - 1.0) * lim) / den
    b_re, b_im = bre_ref[...], bim_ref[...]
    bb_re = q_re * b_re - q_im * b_im
    bb_im = q_re * b_im + q_im * b_re
    u = u_ref[...]
    hre_scr[0:nb, :] = hre_ref[...]
    him_scr[0:nb, :] = him_ref[...]
    hre_scr[nb:, :] = _dot(u, bb_re)
    him_scr[nb:, :] = _dot(u, bb_im)
    ar = jnp.broadcast_to(ab_re, (nb, S5_FLAT))
    ai = jnp.broadcast_to(ab_im, (nb, S5_FLAT))

    def step(s, carry):
        prev = pl.ds(pl.multiple_of(s * nb, nb), nb)
        cur = pl.ds(pl.multiple_of((s + 1) * nb, nb), nb)
        pr, pi = hre_scr[prev, :], him_scr[prev, :]
        hre_scr[cur, :] = ar * pr - ai * pi + hre_scr[cur, :]
        him_scr[cur, :] = ar * pi + ai * pr + him_scr[cur, :]
        return carry

    lax.fori_loop(0, tl, step, 0)
    hre_ref[...] = hre_scr[(final_step + 1) * nb:(final_step + 2) * nb, :]
    him_ref[...] = him_scr[(final_step + 1) * nb:(final_step + 2) * nb, :]
    y = _dot(hre_scr[nb:, :], cre_ref[...]) - _dot(him_scr[nb:, :], cim_ref[...]) + d_ref[...] * u
    y = 0.5 * y * (1.0 + jnp.tanh(0.7978845608028654 * (y + 0.044715 * (y * y * y))))
    y_ref[...] = y * _sigmoid(_dot(y, gw_ref[...]) + gb_ref[...])


def _s5(u_tm, st_re, st_im, a_re, a_im, log_dt, b_re, b_im, c_re, c_im, d, glu_w, glu_b, *, nb, tl, final_step):
    rows = u_tm.shape[0]
    L = rows // nb
    eye = jnp.eye(S5_GROUPS, dtype=_F32)
    bd_in = lambda b: jnp.einsum('gpc,gh->gchp', b, eye).reshape(S5_WIDTH, S5_FLAT)
    bd_out = lambda c: jnp.einsum('gcp,gh->gphc', c, eye).reshape(S5_FLAT, S5_WIDTH)
    cmap = lambda t: (0, 0)
    row = lambda n: pl.BlockSpec((1, n), cmap)
    full = lambda a, b: pl.BlockSpec((a, b), cmap)
    return pl.pallas_call(
        functools.partial(_s5_kernel, nb=nb, tl=tl, final_step=final_step),
        grid=(L // tl,),
        in_specs=[pl.BlockSpec((tl * nb, S5_WIDTH), lambda t: (t, 0)), full(nb, S5_FLAT), full(nb, S5_FLAT),
                  row(S5_FLAT), row(S5_FLAT), row(S5_FLAT),
                  full(S5_WIDTH, S5_FLAT), full(S5_WIDTH, S5_FLAT), full(S5_FLAT, S5_WIDTH), full(S5_FLAT, S5_WIDTH),
                  row(S5_WIDTH), full(S5_WIDTH, S5_WIDTH), row(S5_WIDTH)],
        out_specs=[pl.BlockSpec((tl * nb, S5_WIDTH), lambda t: (t, 0)), full(nb, S5_FLAT), full(nb, S5_FLAT)],
        out_shape=[jax.ShapeDtypeStruct((rows, S5_WIDTH), _F32),
                   jax.ShapeDtypeStruct((nb, S5_FLAT), _F32), jax.ShapeDtypeStruct((nb, S5_FLAT), _F32)],
        scratch_shapes=[pltpu.VMEM(((tl + 1) * nb, S5_FLAT), _F32), pltpu.VMEM(((tl + 1) * nb, S5_FLAT), _F32)],
        compiler_params=_params("arbitrary"),
    )(u_tm, st_re, st_im, a_re.reshape(1, S5_FLAT), a_im.reshape(1, S5_FLAT),
      jnp.repeat(log_dt, S5_STATE).reshape(1, S5_FLAT), bd_in(b_re), bd_in(b_im), bd_out(c_re), bd_out(c_im),
      d.reshape(1, S5_WIDTH), glu_w, glu_b.reshape(1, S5_WIDTH))


def _out_mlp_kernel(x_ref, yssd_ref, yrw_ref, ys5_ref, g1_ref, sh2_ref, sc2_ref, g2_ref, n2g_ref, fg_ref,
                    wo_ref, w1_ref, w2_ref, o_ref, x1_scr, h2_scr, acc_scr, *, final_norm):
    k = pl.program_id(2)
    bb, tl, _ = x_ref.shape
    rows = bb * tl

    @pl.when(k == 0)
    def _():
        flat = lambda ref: ref[...].reshape(rows, ref.shape[-1]).astype(_BF16)
        mix = (_dot(flat(yssd_ref), wo_ref[0:512, :]) + _dot(flat(yrw_ref), wo_ref[512:768, :])
               + _dot(flat(ys5_ref), wo_ref[768:1024, :]))
        x1 = x_ref[...] + g1_ref[...] * mix.reshape(bb, tl, D_MODEL)
        x1_scr[...] = x1
        xn = x1 * lax.rsqrt(jnp.mean(x1 * x1, axis=-1, keepdims=True) + NORM_EPS)
        h2 = xn * n2g_ref[...] * (1.0 + sc2_ref[...]) + sh2_ref[...]
        h2_scr[...] = h2.reshape(rows, D_MODEL).astype(_BF16)
        acc_scr[...] = jnp.zeros_like(acc_scr)

    hid = _dot(h2_scr[...], w1_ref[...])
    hid = jnp.square(jnp.maximum(hid, 0.0)).astype(_BF16)
    acc_scr[...] += _dot(hid, w2_ref[...])

    @pl.when(k == pl.num_programs(2) - 1)
    def _():
        out = x1_scr[...] + g2_ref[...] * acc_scr[...].reshape(bb, tl, D_MODEL)
        if final_norm:
            out = out * lax.rsqrt(jnp.mean(out * out, axis=-1, keepdims=True) + NORM_EPS) * fg_ref[...]
        o_ref[...] = out


def _out_mlp(x, y_ssd, y_rw, y_s5, mod, norm2_g, final_g, w_out_b, w1_b, w2_b, *, bb, tl, time_major_s5,
             final_norm):
    nb, L, _ = x.shape
    grid = (nb // bb, L // tl, D_FF // FF_TILE)
    xmap = lambda i, t, k: (i, t, 0)
    cmap = lambda i, t, k: (0, 0)

    def modspec(j):
        return pl.BlockSpec((bb, 1, D_MODEL), lambda i, t, k: (i, 0, j))

    s5_spec = (pl.BlockSpec((tl, S5_WIDTH), lambda i, t, k: (t, i)) if time_major_s5
               else pl.BlockSpec((bb, tl, S5_WIDTH), xmap))
    return pl.pallas_call(
        functools.partial(_out_mlp_kernel, final_norm=final_norm),
        grid=grid,
        in_specs=[pl.BlockSpec((bb, tl, D_MODEL), xmap), pl.BlockSpec((bb, tl, SSD_WIDTH), xmap),
                  pl.BlockSpec((bb, tl, RWKV_WIDTH), xmap), s5_spec,
                  modspec(2), modspec(3), modspec(4), modspec(5),
                  pl.BlockSpec((1, D_MODEL), cmap), pl.BlockSpec((1, D_MODEL), cmap),
                  pl.BlockSpec((D_MODEL, D_MODEL), cmap),
                  pl.BlockSpec((D_MODEL, FF_TILE), lambda i, t, k: (0, k)),
                  pl.BlockSpec((FF_TILE, D_MODEL), lambda i, t, k: (k, 0))],
        out_specs=pl.BlockSpec((bb, tl, D_MODEL), xmap),
        out_shape=jax.ShapeDtypeStruct(x.shape, _F32),
        scratch_shapes=[pltpu.VMEM((bb, tl, D_MODEL), _F32), pltpu.VMEM((bb * tl, D_MODEL), _BF16),
                        pltpu.VMEM((bb * tl, D_MODEL), _F32)],
        compiler_params=_params("arbitrary", "arbitrary", "arbitrary"),
    )(x, y_ssd, y_rw, y_s5, mod, mod, mod, mod, norm2_g.reshape(1, D_MODEL), final_g.reshape(1, D_MODEL),
      w_out_b, w1_b, w2_b)


def _trunk(x, mod, states, W, final_g, *, prompt):
    st_ssd, st_conv, st_rwkv, st_shift, st_re, st_im = states
    nb, L, _ = x.shape
    if prompt:
        bb, tl = 1, min(ROW_TILE, L)
        ssd_t, rwkv_t, s5_tl = min(SSD_CHUNK, L), min(RWKV_CHUNK, L), min(S5_TILE, L)
        mix_bb = 1
    else:
        bb, tl = min(nb, ROW_TILE // L), L
        ssd_t = rwkv_t = s5_tl = L
        mix_bb = min(nb, 8)
    new = ([], [], [], [], [], [])
    for l in range(DEPTH):
        z, xbc, frw, us5, dt = _in_proj(x, mod[l], W['norm1_g'][l], W['w_in_p'][l], bb=bb, tl=tl,
                                        time_major_s5=prompt)
        v = W['valid']
        y_ssd, n_ssd, n_conv = _ssd(z, xbc, dt, st_ssd[l], st_conv[l], W['ssd_conv_w'][l], W['ssd_conv_b'][l],
                                    W['ssd_dt_bias'][l], W['ssd_a_log'][l], W['ssd_d'][l], W['ssd_norm_g'][l],
                                    bb=mix_bb, T=ssd_t, valid=ssd_t if prompt else v)
        y_rw, n_rwkv, n_shift = _rwkv(frw, st_rwkv[l], st_shift[l], W['rwkv_mu'][l], W['rwkv_w0'][l],
                                      W['rwkv_w2'][l], W['rwkv_a0'][l], W['rwkv_a2'][l], W['rwkv_g2'][l],
                                      W['rwkv_k_k'][l], W['rwkv_k_a'][l], W['rwkv_r_k'][l], W['rwkv_ln_g'][l],
                                      W['rwkv_ln_b'][l], bb=mix_bb, T=rwkv_t, valid=rwkv_t if prompt else v)
        if prompt:
            u_tm = us5.reshape(L * nb, S5_WIDTH)
        else:
            u_tm = jnp.transpose(us5, (1, 0, 2)).reshape(L * nb, S5_WIDTH)
        y_s5, n_re, n_im = _s5(u_tm, st_re[l], st_im[l], W['s5_a_re'][l], W['s5_a_im'][l], W['s5_log_dt'][l],
                               W['s5_b_re'][l], W['s5_b_im'][l], W['s5_c_re'][l], W['s5_c_im'][l], W['s5_d'][l],
                               W['s5_glu_w'][l], W['s5_glu_b'][l], nb=nb, tl=s5_tl,
                               final_step=(s5_tl - 1) if prompt else (v - 1))
        if prompt:
            y_s5 = y_s5.reshape(L, nb * S5_WIDTH)
        else:
            y_s5 = jnp.transpose(y_s5.reshape(L, nb, S5_WIDTH), (1, 0, 2))
        x = _out_mlp(x, y_ssd, y_rw, y_s5, mod[l], W['norm2_g'][l], final_g, W['w_out_b'][l], W['w1_b'][l],
                     W['w2_b'][l], bb=bb, tl=tl, time_major_s5=prompt, final_norm=(l == DEPTH - 1))
        for lst, s in zip(new, (n_ssd, n_conv, n_rwkv, n_shift, n_re, n_im)):
            lst.append(s)
    return x, [jnp.stack(lst) for lst in new]


def _pack_states(st_ssd, st_conv, st_rwkv, st_shift, st_re, st_im):
    d, nb = st_ssd.shape[:2]
    conv8 = jnp.pad(st_conv, ((0, 0), (0, 0), (SUBLANES - (SSD_CONV - 1), 0), (0, 0)))
    shift8 = jnp.pad(st_shift[:, :, None, :], ((0, 0), (0, 0), (SUBLANES - 1, 0), (0, 0)))
    s_wide = jnp.transpose(st_rwkv, (0, 1, 3, 2, 4)).reshape(d, nb, RWKV_HEAD_DIM, RWKV_WIDTH)
    return (st_ssd, conv8, s_wide, shift8, st_re.reshape(d, nb, S5_FLAT), st_im.reshape(d, nb, S5_FLAT))


def _unpack_states(n_ssd, n_conv8, n_wide, n_shift8, n_re, n_im):
    d, nb = n_ssd.shape[:2]
    n_rwkv = jnp.transpose(n_wide.reshape(d, nb, RWKV_HEAD_DIM, RWKV_HEADS, RWKV_HEAD_DIM), (0, 1, 3, 2, 4))
    return (n_ssd, n_conv8[:, :, SUBLANES - (SSD_CONV - 1):, :], n_rwkv, n_shift8[:, :, SUBLANES - 1, :],
            n_re.reshape(d, nb, S5_GROUPS, S5_STATE), n_im.reshape(d, nb, S5_GROUPS, S5_STATE))


def kernel(x_prompt, x_sample, c_prompt, c_sample, state_ssd, state_ssd_conv, state_rwkv, state_rwkv_shift,
           state_s5_re, state_s5_im, ada_w, ada_b, norm1_g, norm2_g, w_in, ssd_conv_w, ssd_conv_b, ssd_dt_bias,
           ssd_a_log, ssd_d, ssd_norm_g, rwkv_mu, rwkv_w0, rwkv_w2, rwkv_a0, rwkv_a2, rwkv_g2, rwkv_k_k, rwkv_k_a,
           rwkv_r_k, rwkv_ln_g, rwkv_ln_b, s5_a_re, s5_a_im, s5_log_dt, s5_b_re, s5_b_im, s5_c_re, s5_c_im, s5_d,
           s5_glu_w, s5_glu_b, w_out, mlp_w1, mlp_w2, final_g):
    bp, lp, _ = x_prompt.shape
    bs, ls, _ = x_sample.shape
    ls_pad = -(-ls // SUBLANES) * SUBLANES
    w_in_p = jnp.concatenate(
        [w_in[:, :, _Z0:_DT0], w_in[:, :, _FRW0:_IN_COLS], w_in[:, :, _DT0:_FRW0],
         jnp.zeros((DEPTH, D_MODEL, DT_PAD - SSD_HEADS), w_in.dtype)], axis=-1).astype(_BF16)
    W = dict(norm1_g=norm1_g, norm2_g=norm2_g, w_in_p=w_in_p, ssd_conv_w=ssd_conv_w, ssd_conv_b=ssd_conv_b,
             ssd_dt_bias=ssd_dt_bias, ssd_a_log=ssd_a_log, ssd_d=ssd_d, ssd_norm_g=ssd_norm_g, rwkv_mu=rwkv_mu,
             rwkv_w0=rwkv_w0, rwkv_w2=rwkv_w2, rwkv_a0=rwkv_a0, rwkv_a2=rwkv_a2, rwkv_g2=rwkv_g2,
             rwkv_k_k=rwkv_k_k, rwkv_k_a=rwkv_k_a, rwkv_r_k=rwkv_r_k.reshape(DEPTH, RWKV_WIDTH),
             rwkv_ln_g=rwkv_ln_g, rwkv_ln_b=rwkv_ln_b, s5_a_re=s5_a_re, s5_a_im=s5_a_im, s5_log_dt=s5_log_dt,
             s5_b_re=s5_b_re, s5_b_im=s5_b_im, s5_c_re=s5_c_re, s5_c_im=s5_c_im, s5_d=s5_d, s5_glu_w=s5_glu_w,
             s5_glu_b=s5_glu_b, w_out_b=w_out.astype(_BF16), w1_b=mlp_w1.astype(_BF16), w2_b=mlp_w2.astype(_BF16),
             valid=ls)
    mod = _ada(jnp.concatenate([c_prompt, c_sample], axis=0), ada_w, ada_b)
    mod_p = mod[:, :bp].reshape(DEPTH, bp, 1, 6 * D_MODEL)
    mod_s = mod[:, bp:].reshape(DEPTH, bs, 1, 6 * D_MODEL)

    zeros = lambda shape: jnp.zeros((DEPTH, bp) + shape, _F32)
    p_states = _pack_states(zeros((SSD_HEADS, SSD_HEAD_DIM, SSD_STATE)), zeros((SSD_CONV - 1, SSD_CONV_DIM)),
                            zeros((RWKV_HEADS, RWKV_HEAD_DIM, RWKV_HEAD_DIM)), zeros((RWKV_PROJ,)),
                            zeros((S5_GROUPS, S5_STATE)), zeros((S5_GROUPS, S5_STATE)))
    y_prompt, sp = _trunk(x_prompt, mod_p, p_states, W, final_g, prompt=True)

    s_states = _pack_states(state_ssd, state_ssd_conv, state_rwkv, state_rwkv_shift, state_s5_re, state_s5_im)
    x_s = jnp.pad(x_sample, ((0, 0), (0, ls_pad - ls), (0, 0)))
    y_sample, ss = _trunk(x_s, mod_s, s_states, W, final_g, prompt=False)
    return (y_prompt, y_sample[:, :ls]) + _unpack_states(*sp) + _unpack_states(*ss)
```

```python
import functools

import jax
import jax.numpy as jnp
from jax import lax
from jax.experimental import pallas as pl
from jax.experimental.pallas import tpu as pltpu

D_MODEL = 1024
DEPTH = 2
SSD_WIDTH = 512
SSD_HEAD_DIM = 64
SSD_HEADS = 8
SSD_GROUPS = 2
SSD_HEADS_PER_GROUP = SSD_HEADS // SSD_GROUPS
SSD_STATE = 128
SSD_CONV = 4
SSD_CONV_DIM = SSD_WIDTH + 2 * SSD_GROUPS * SSD_STATE
RWKV_WIDTH = 256
RWKV_HEAD_DIM = 64
RWKV_HEADS = 4
RWKV_DECAY_LORA = 64
RWKV_A_LORA = 64
RWKV_GATE_LORA = 128
RWKV_PROJ = 3 * RWKV_WIDTH + RWKV_DECAY_LORA + RWKV_A_LORA + RWKV_GATE_LORA
RWKV_LN_EPS = 64e-5
S5_WIDTH = 256
S5_GROUP_CH = 16
S5_GROUPS = 16
S5_STATE = 64
S5_FLAT = S5_GROUPS * S5_STATE
D_FF = 4 * D_MODEL
NORM_EPS = 1e-6
DT_PAD = 128
SUBLANES = 8

_Z0, _XBC0, _DT0, _FRW0, _US50 = 0, SSD_WIDTH, SSD_WIDTH + SSD_CONV_DIM, SSD_WIDTH + SSD_CONV_DIM + SSD_HEADS, \
    SSD_WIDTH + SSD_CONV_DIM + SSD_HEADS + RWKV_PROJ
_IN_COLS = _US50 + S5_WIDTH
_PZ, _PXBC, _PFRW, _PUS5, _PDT = 0, 512, 1536, 2560, 2816
_P_COLS = _PDT + DT_PAD

SSD_CHUNK = 128
RWKV_CHUNK = 64
S5_TILE = 256
ROW_TILE = 512
FF_TILE = 1024
VMEM_LIMIT = 56 * 1024 * 1024

_HI = lax.Precision.HIGHEST
P_AMAT = 1
P_INV = 1
P_S0 = 1
P_INTRA = 1
P_SOLVE = 1
P_STATE = 1
_F32 = jnp.float32
_BF16 = jnp.bfloat16


_NN = (((1,), (0,)), ((), ()))
_NT = (((1,), (1,)), ((), ()))
_TN = (((0,), (0,)), ((), ()))


def _split(x):
    hi = x.astype(_BF16)
    return hi, (x - hi.astype(_F32)).astype(_BF16)


def _mm(a, b, dims, passes):
    dg = lambda x, y, prec=None: lax.dot_general(x, y, dims, preferred_element_type=_F32, precision=prec)
    if passes == 6:
        return dg(a.astype(_F32), b.astype(_F32), _HI)
    if passes == 3:
        ah, al = _split(a)
        bh, bl = _split(b)
        return dg(ah, bh) + (dg(ah, bl) + dg(al, bh))
    return dg(a.astype(_BF16), b.astype(_BF16))


def _dot(a, b, passes=1):
    return _mm(a, b, _NN, passes)


def _dot_nt(a, b, passes=1):
    return _mm(a, b, _NT, passes)


def _dot_tn(a, b, passes=1):
    return _mm(a, b, _TN, passes)


def _mm_sel(sel, x, dims, x_is_rhs=True):
    hi = x.astype(_BF16)
    rest = x - hi.astype(_F32)
    mid = rest.astype(_BF16)
    lo = (rest - mid.astype(_F32)).astype(_BF16)
    sel = sel.astype(_BF16)
    if x_is_rhs:
        dg = lambda y: lax.dot_general(sel, y, dims, preferred_element_type=_F32)
    else:
        dg = lambda y: lax.dot_general(y, sel, dims, preferred_element_type=_F32)
    return dg(hi) + (dg(mid) + dg(lo))


def _iota(shape, dim):
    return lax.broadcasted_iota(jnp.int32, shape, dim)


def _sigmoid(x):
    return 1.0 / (1.0 + jnp.exp(-x))


def _softplus(x):
    return jnp.maximum(x, 0.0) + jnp.log1p(jnp.exp(-jnp.abs(x)))


def _silu(x):
    return x * _sigmoid(x)


def _params(*sems):
    return pltpu.CompilerParams(dimension_semantics=sems, vmem_limit_bytes=VMEM_LIMIT)


def _ada_kernel(c_ref, w_ref, b_ref, o_ref):
    c = c_ref[...]
    s = _silu(c).astype(_BF16)
    o_ref[0] = _dot(s, w_ref[0].astype(_BF16)) + b_ref[0]


def _ada(c_all, ada_w, ada_b):
    rows = c_all.shape[0]
    tn = 1536
    return pl.pallas_call(
        _ada_kernel,
        grid=(DEPTH, 6 * D_MODEL // tn),
        in_specs=[pl.BlockSpec((rows, D_MODEL), lambda l, j: (0, 0)),
                  pl.BlockSpec((1, D_MODEL, tn), lambda l, j: (l, 0, j)),
                  pl.BlockSpec((1, 1, tn), lambda l, j: (l, 0, j))],
        out_specs=pl.BlockSpec((1, rows, tn), lambda l, j: (l, 0, j)),
        out_shape=jax.ShapeDtypeStruct((DEPTH, rows, 6 * D_MODEL), _F32),
        compiler_params=_params("arbitrary", "arbitrary"),
    )(c_all, ada_w, ada_b.reshape(DEPTH, 1, 6 * D_MODEL))


def _in_proj_kernel(x_ref, sh_ref, sc_ref, g_ref, w_ref, z_ref, xbc_ref, frw_ref, us5_ref, dt_ref):
    x = x_ref[...]
    bb, tl, _ = x.shape
    xn = x * lax.rsqrt(jnp.mean(x * x, axis=-1, keepdims=True) + NORM_EPS)
    h = xn * g_ref[...] * (1.0 + sc_ref[...]) + sh_ref[...]
    h = h.reshape(bb * tl, D_MODEL).astype(_BF16)
    for o_ref, lo, hi in ((z_ref, _PZ, _PXBC), (xbc_ref, _PXBC, _PFRW), (frw_ref, _PFRW, _PUS5),
                          (us5_ref, _PUS5, _PDT), (dt_ref, _PDT, _P_COLS)):
        o_ref[...] = _dot(h, w_ref[:, lo:hi]).reshape(o_ref.shape)


def _in_proj(x, mod, norm_g, w_in_p, *, bb, tl, time_major_s5):
    nb, L, _ = x.shape
    grid = (nb // bb, L // tl)
    xmap = lambda i, t: (i, t, 0)

    def modspec(j):
        return pl.BlockSpec((bb, 1, D_MODEL), lambda i, t: (i, 0, j))

    def out(n):
        return jax.ShapeDtypeStruct((nb, L, n), _F32), pl.BlockSpec((bb, tl, n), xmap)

    outs = [out(SSD_WIDTH), out(SSD_CONV_DIM), out(RWKV_PROJ), out(S5_WIDTH), out(DT_PAD)]
    if time_major_s5:
        assert bb == 1
        outs[3] = (jax.ShapeDtypeStruct((L, nb * S5_WIDTH), _F32), pl.BlockSpec((tl, S5_WIDTH), lambda i, t: (t, i)))
    return pl.pallas_call(
        _in_proj_kernel,
        grid=grid,
        in_specs=[pl.BlockSpec((bb, tl, D_MODEL), xmap), modspec(0), modspec(1),
                  pl.BlockSpec((1, D_MODEL), lambda i, t: (0, 0)),
                  pl.BlockSpec((D_MODEL, _P_COLS), lambda i, t: (0, 0))],
        out_specs=[o[1] for o in outs],
        out_shape=[o[0] for o in outs],
        compiler_params=_params("arbitrary", "arbitrary"),
    )(x, mod, mod, norm_g.reshape(1, D_MODEL), w_in_p)


def _ssd_seq(j, z_ref, xbc_ref, dt_ref, cw_ref, cb_ref, dtb_ref, alog_ref, dexp_ref, ng_ref, eexp_ref,
             y_ref, sto_ref, cvo_ref, ext_scr, ybuf_scr, T, valid):
    u = xbc_ref[j]
    ext_scr[j, 0:SUBLANES, :] = cvo_ref[j]
    ext_scr[j, SUBLANES:SUBLANES + T, :] = u
    conv = cb_ref[...] + cw_ref[3:4, :] * u
    for back in (1, 2, 3):
        conv = conv + cw_ref[3 - back:4 - back, :] * ext_scr[j, SUBLANES - back:SUBLANES - back + T, :]
    cvo_ref[j] = ext_scr[j, valid:valid + SUBLANES, :]
    xbc = _silu(conv)
    xs = xbc[:, :SSD_WIDTH]
    row = _iota((T, 1), 0)
    dtv = _softplus(dt_ref[j] + dtb_ref[...])
    if valid < T:
        dtv = jnp.where(row < valid, dtv, 0.0)
    a_row = jnp.where(_iota((1, DT_PAD), 1) < SSD_HEADS, -jnp.exp(alog_ref[...]), 0.0)
    a = dtv * a_row
    tri_b = _iota((T, T), 0) >= _iota((T, T), 1)
    tri = tri_b.astype(_F32)
    cs = _mm_sel(tri, a, _NN)
    eye = (_iota((DT_PAD, DT_PAD), 0) == _iota((DT_PAD, DT_PAD), 1)).astype(_F32)
    cs_t = _mm_sel(eye, cs, _NT)
    xdt = xs * _mm_sel(eexp_ref[...], dtv, _NN, x_is_rhs=False)
    ecs = jnp.exp(cs)
    cs_last = cs[T - 1:T, :]
    dte = jnp.exp(cs_last - cs)
    elast = jnp.exp(cs_last)
    for g in range(SSD_GROUPS):
        bm = xbc[:, SSD_WIDTH + g * SSD_STATE:SSD_WIDTH + (g + 1) * SSD_STATE]
        cm = xbc[:, SSD_WIDTH + (SSD_GROUPS + g) * SSD_STATE:SSD_WIDTH + (SSD_GROUPS + g + 1) * SSD_STATE]
        cb = _dot_nt(cm, bm)
        for r in range(SSD_HEADS_PER_GROUP):
            h = g * SSD_HEADS_PER_GROUP + r
            lo, hi = h * SSD_HEAD_DIM, (h + 1) * SSD_HEAD_DIM
            decay = jnp.exp(jnp.where(tri_b, cs[:, h:h + 1] - cs_t[h:h + 1, :], -jnp.inf))
            xdt_h = xdt[:, lo:hi]
            st = sto_ref[j, h]
            y_h = _dot(cb * decay, xdt_h) + ecs[:, h:h + 1] * _dot_nt(cm, st)
            ybuf_scr[j, :, lo:hi] = y_h
            sto_ref[j, h] = elast[:, h:h + 1] * st + _dot_tn(xdt_h * dte[:, h:h + 1], bm)
    y = ybuf_scr[j] + xs * dexp_ref[...]
    y = y * _silu(z_ref[j])
    y = y * lax.rsqrt(jnp.mean(y * y, axis=-1, keepdims=True) + NORM_EPS) * ng_ref[...]
    y_ref[j] = y


def _ssd_kernel(z_ref, xbc_ref, dt_ref, st_ref, cv_ref, cw_ref, cb_ref, dtb_ref, alog_ref, dexp_ref, ng_ref,
                eexp_ref, y_ref, sto_ref, cvo_ref, ext_scr, ybuf_scr, *, bb, T, valid):
    @pl.when(pl.program_id(1) == 0)
    def _():
        sto_ref[...] = st_ref[...]
        cvo_ref[...] = cv_ref[...]

    seq = functools.partial(_ssd_seq, z_ref=z_ref, xbc_ref=xbc_ref, dt_ref=dt_ref, cw_ref=cw_ref, cb_ref=cb_ref,
                            dtb_ref=dtb_ref, alog_ref=alog_ref, dexp_ref=dexp_ref, ng_ref=ng_ref, eexp_ref=eexp_ref,
                            y_ref=y_ref, sto_ref=sto_ref, cvo_ref=cvo_ref, ext_scr=ext_scr, ybuf_scr=ybuf_scr,
                            T=T, valid=valid)
    for j in range(bb):
        seq(j)


def _ssd(z, xbc, dt, st_ssd, st_conv8, l, conv_w, conv_b, dt_bias, a_log, d, norm_g, *, bb, T, valid):
    nb, L, _ = z.shape
    grid = (nb // bb, L // T)
    tmap = lambda i, c: (i, c, 0)
    cmap2 = lambda i, c: (0, 0)
    smap = lambda i, c: (i, 0, 0, 0)
    cvmap = lambda i, c: (i, 0, 0)
    pad = DT_PAD - SSD_HEADS
    eexp = jnp.repeat(jnp.eye(DT_PAD, SSD_HEADS, dtype=_F32), SSD_HEAD_DIM, axis=1)
    return pl.pallas_call(
        functools.partial(_ssd_kernel, bb=bb, T=T, valid=valid),
        grid=grid,
        in_specs=[pl.BlockSpec((bb, T, SSD_WIDTH), tmap), pl.BlockSpec((bb, T, SSD_CONV_DIM), tmap),
                  pl.BlockSpec((bb, T, DT_PAD), tmap),
                  pl.BlockSpec((None, bb, SSD_HEADS, SSD_HEAD_DIM, SSD_STATE), lambda i, c: (l, i, 0, 0, 0)),
                  pl.BlockSpec((None, bb, SUBLANES, SSD_CONV_DIM), lambda i, c: (l, i, 0, 0)),
                  pl.BlockSpec((SSD_CONV, SSD_CONV_DIM), cmap2), pl.BlockSpec((1, SSD_CONV_DIM), cmap2),
                  pl.BlockSpec((1, DT_PAD), cmap2), pl.BlockSpec((1, DT_PAD), cmap2),
                  pl.BlockSpec((1, SSD_WIDTH), cmap2), pl.BlockSpec((1, SSD_WIDTH), cmap2),
                  pl.BlockSpec((DT_PAD, SSD_WIDTH), cmap2)],
        out_specs=[pl.BlockSpec((bb, T, SSD_WIDTH), tmap),
                   pl.BlockSpec((bb, SSD_HEADS, SSD_HEAD_DIM, SSD_STATE), smap),
                   pl.BlockSpec((bb, SUBLANES, SSD_CONV_DIM), cvmap)],
        out_shape=[jax.ShapeDtypeStruct((nb, L, SSD_WIDTH), _F32),
                   jax.ShapeDtypeStruct(st_ssd.shape[1:], _F32),
                   jax.ShapeDtypeStruct(st_conv8.shape[1:], _F32)],
        scratch_shapes=[pltpu.VMEM((bb, T + SUBLANES, SSD_CONV_DIM), _F32), pltpu.VMEM((bb, T, SSD_WIDTH), _F32)],
        compiler_params=_params("arbitrary", "arbitrary"),
    )(z, xbc, dt, st_ssd, st_conv8, conv_w, conv_b.reshape(1, -1),
      jnp.pad(dt_bias, (0, pad)).reshape(1, DT_PAD), jnp.pad(a_log, (0, pad)).reshape(1, DT_PAD),
      jnp.repeat(d, SSD_HEAD_DIM).reshape(1, SSD_WIDTH), norm_g.reshape(1, SSD_WIDTH), eexp)


def _stack_heads(x):
    return jnp.concatenate([x[:, h * RWKV_HEAD_DIM:(h + 1) * RWKV_HEAD_DIM] for h in range(RWKV_HEADS)], axis=0)


def _unstack_heads(x, T):
    return jnp.concatenate([x[h * T:(h + 1) * T, :] for h in range(RWKV_HEADS)], axis=1)


def _rwkv_seq(j, f_ref, mu_ref, w0_ref, w2_ref, a0_ref, a2_ref, g2_ref, kk_ref, ka_ref, rk_ref, lng_ref, lnb_ref,
              y_ref, so_ref, sho_ref, ext_scr, T, valid):
    HT = RWKV_HEADS * T
    f = f_ref[j]
    ext_scr[j, 0:SUBLANES, :] = sho_ref[j]
    ext_scr[j, SUBLANES:SUBLANES + T, :] = f
    prev = ext_scr[j, SUBLANES - 1:SUBLANES - 1 + T, :]
    sho_ref[j] = ext_scr[j, valid:valid + SUBLANES, :]
    fm = f + (prev - f) * mu_ref[...]
    r = fm[:, 0:256]
    k = fm[:, 256:512]
    v = fm[:, 512:768]
    lora = fm[:, 768:896]
    gl = fm[:, 896:1024]
    w_log = -_softplus(-(w0_ref[...] + _dot(jnp.tanh(lora), w2_ref[...]))) - 0.5
    logw = -jnp.exp(w_log)
    a = _sigmoid(a0_ref[...] + _dot(lora, a2_ref[...]))
    g = _dot(_sigmoid(gl), g2_ref[...])
    k2 = k * (1.0 + (a - 1.0) * ka_ref[...])
    kk4 = _stack_heads(k * kk_ref[...])
    kk4 = kk4 * lax.rsqrt(jnp.maximum(jnp.sum(kk4 * kk4, axis=-1, keepdims=True), 1e-24))
    kkn = _unstack_heads(kk4, T)
    if valid < T:
        live = _iota((T, 1), 0) < valid
        logw = jnp.where(live, logw, 0.0)
        kkn = jnp.where(live, kkn, 0.0)
        k2 = jnp.where(live, k2, 0.0)
    tri = (_iota((T, T), 0) >= _iota((T, T), 1)).astype(_F32)
    cw = _mm_sel(tri, logw, _NN)
    w_inc = jnp.exp(cw)
    w_exc = jnp.exp(cw - logw)
    w_inv = jnp.exp(-cw)
    w_end = w_inc[T - 1:T, :]
    at = -kkn * w_exc
    rt = r * w_inc
    bt = kkn * a * w_inv
    kt = k2 * w_inv
    a4, r4, b4, k4, v4 = (_stack_heads(t) for t in (at, rt, bt, kt, v))

    rr = _iota((HT, HT), 0)
    cc = _iota((HT, HT), 1)
    same = (rr // T) == (cc // T)
    strict = jnp.logical_and(same, rr > cc)
    incl = jnp.logical_and(same, rr >= cc)
    n_ab = jnp.where(strict, _dot_nt(a4, b4, P_AMAT), 0.0)
    n_ak = jnp.where(strict, _dot_nt(a4, k4, P_AMAT), 0.0)
    m_rb = jnp.where(incl, _dot_nt(r4, b4, P_AMAT), 0.0)
    m_rk = jnp.where(incl, _dot_nt(r4, k4, P_AMAT), 0.0)
    inv = (rr == cc).astype(_F32) + n_ab
    p = n_ab
    for _ in range(max((T - 1).bit_length() - 1, 0)):
        p = _dot(p, p, P_INV)
        inv = inv + _dot(inv, p, P_INV)

    head_of_row = _iota((HT, RWKV_WIDTH), 0) // T
    head_of_col = _iota((HT, RWKV_WIDTH), 1) // RWKV_HEAD_DIM
    hmask = head_of_row == head_of_col

    def wide(x):
        return jnp.where(hmask, jnp.concatenate([x] * RWKV_HEADS, axis=0), 0.0)

    s0 = jnp.concatenate([so_ref[j, h] for h in range(RWKV_HEADS)], axis=1)
    u4 = _dot(inv, _dot_nt(wide(at), s0, P_S0) + _dot(n_ak, v4, P_INTRA), P_SOLVE)
    o4 = _dot_nt(wide(rt), s0, P_S0) + _dot(m_rb, u4, P_INTRA) + _dot(m_rk, v4, P_INTRA)
    s1 = (s0 * w_end + _dot_tn(u4, wide(bt * w_end), P_STATE) + _dot_tn(v4, wide(kt * w_end), P_STATE))
    for h in range(RWKV_HEADS):
        so_ref[j, h] = s1[:, h * RWKV_HEAD_DIM:(h + 1) * RWKV_HEAD_DIM]

    mean = jnp.mean(o4, axis=-1, keepdims=True)
    var = jnp.mean(jnp.square(o4 - mean), axis=-1, keepdims=True)
    o4 = (o4 - mean) * lax.rsqrt(var + RWKV_LN_EPS)
    bonus4 = jnp.sum(_stack_heads(r * k2 * rk_ref[...]), axis=-1, keepdims=True) * v4
    o = _unstack_heads(o4, T) * lng_ref[...] + lnb_ref[...] + _unstack_heads(bonus4, T)
    y_ref[j] = o * g


def _rwkv_kernel(f_ref, s_ref, sh_ref, mu_ref, w0_ref, w2_ref, a0_ref, a2_ref, g2_ref, kk_ref, ka_ref, rk_ref,
                 lng_ref, lnb_ref, y_ref, so_ref, sho_ref, ext_scr, *, bb, T, valid):
    @pl.when(pl.program_id(1) == 0)
    def _():
        so_ref[...] = s_ref[...]
        sho_ref[...] = sh_ref[...]

    seq = functools.partial(_rwkv_seq, f_ref=f_ref, mu_ref=mu_ref, w0_ref=w0_ref, w2_ref=w2_ref, a0_ref=a0_ref,
                            a2_ref=a2_ref, g2_ref=g2_ref, kk_ref=kk_ref, ka_ref=ka_ref, rk_ref=rk_ref,
                            lng_ref=lng_ref, lnb_ref=lnb_ref, y_ref=y_ref, so_ref=so_ref, sho_ref=sho_ref,
                            ext_scr=ext_scr, T=T, valid=valid)
    for j in range(bb):
        seq(j)


def _rwkv(f, st_rwkv, shift8, l, mu, w0, w2, a0, a2, g2, k_k, k_a, r_k, ln_g, ln_b, *, bb, T, valid):
    nb, L, _ = f.shape
    hd = RWKV_HEAD_DIM
    grid = (nb // bb, L // T)
    tmap = lambda i, c: (i, c, 0)
    bmap = lambda i, c: (i, 0, 0)
    cmap2 = lambda i, c: (0, 0)
    row = lambda n: pl.BlockSpec((1, n), cmap2)
    lora_w = RWKV_DECAY_LORA + RWKV_A_LORA
    w2_pad = jnp.concatenate([w2, jnp.zeros((RWKV_A_LORA, RWKV_WIDTH), _F32)], axis=0)
    a2_pad = jnp.concatenate([jnp.zeros((RWKV_DECAY_LORA, RWKV_WIDTH), _F32), a2], axis=0)
    return pl.pallas_call(
        functools.partial(_rwkv_kernel, bb=bb, T=T, valid=valid),
        grid=grid,
        in_specs=[pl.BlockSpec((bb, T, RWKV_PROJ), tmap),
                  pl.BlockSpec((None, bb, RWKV_HEADS, hd, hd), lambda i, c: (l, i, 0, 0, 0)),
                  pl.BlockSpec((None, bb, SUBLANES, RWKV_PROJ), lambda i, c: (l, i, 0, 0)),
                  row(RWKV_PROJ), row(RWKV_WIDTH), pl.BlockSpec((lora_w, RWKV_WIDTH), cmap2),
                  row(RWKV_WIDTH), pl.BlockSpec((lora_w, RWKV_WIDTH), cmap2),
                  pl.BlockSpec((RWKV_GATE_LORA, RWKV_WIDTH), cmap2),
                  row(RWKV_WIDTH), row(RWKV_WIDTH), row(RWKV_WIDTH), row(RWKV_WIDTH), row(RWKV_WIDTH)],
        out_specs=[pl.BlockSpec((bb, T, RWKV_WIDTH), tmap),
                   pl.BlockSpec((bb, RWKV_HEADS, hd, hd), lambda i, c: (i, 0, 0, 0)),
                   pl.BlockSpec((bb, SUBLANES, RWKV_PROJ), bmap)],
        out_shape=[jax.ShapeDtypeStruct((nb, L, RWKV_WIDTH), _F32),
                   jax.ShapeDtypeStruct(st_rwkv.shape[1:], _F32),
                   jax.ShapeDtypeStruct(shift8.shape[1:], _F32)],
        scratch_shapes=[pltpu.VMEM((bb, T + SUBLANES, RWKV_PROJ), _F32)],
        compiler_params=_params("arbitrary", "arbitrary"),
    )(f, st_rwkv, shift8, mu.reshape(1, -1), w0.reshape(1, -1), w2_pad, a0.reshape(1, -1), a2_pad, g2,
      k_k.reshape(1, -1), k_a.reshape(1, -1), r_k.reshape(1, -1), ln_g.reshape(1, -1), ln_b.reshape(1, -1))


def _s5_kernel(u_ref, sre_ref, sim_ref, lre_ref, lim_ref, dt_ref, bre_ref, bim_ref, cre_ref, cim_ref, d_ref,
               gw_ref, gb_ref, y_ref, hre_ref, him_ref, hre_scr, him_scr, *, nb, tl, final_step):
    @pl.when(pl.program_id(0) == 0)
    def _():
        hre_ref[...] = sre_ref[...]
        him_ref[...] = sim_ref[...]

    lre, lim, dt = lre_ref[...], lim_ref[...], jnp.exp(dt_ref[...])
    mag = jnp.exp(lre * dt)
    ang = lim * dt
    ab_re, ab_im = mag * jnp.cos(ang), mag * jnp.sin(ang)
    den = jnp.square(lre) + jnp.square(lim)
    q_re = ((ab_re - 1.0) * lre + ab_im * lim) / den
    q_im = (ab_im * lre - (ab_re - 1.0) * lim) / den
    b_re, b_im = bre_ref[...], bim_ref[...]
    bb_re = q_re * b_re - q_im * b_im
    bb_im = q_re * b_im + q_im * b_re
    u = u_ref[...]
    hre_scr[0:nb, :] = hre_ref[...]
    him_scr[0:nb, :] = him_ref[...]
    hre_scr[nb:, :] = _dot(u, bb_re)
    him_scr[nb:, :] = _dot(u, bb_im)
    ar = jnp.broadcast_to(ab_re, (nb, S5_FLAT))
    ai = jnp.broadcast_to(ab_im, (nb, S5_FLAT))

    def step(s, carry):
        prev = pl.ds(pl.multiple_of(s * nb, nb), nb)
        cur = pl.ds(pl.multiple_of((s + 1) * nb, nb), nb)
        pr, pi = hre_scr[prev, :], him_scr[prev, :]
        hre_scr[cur, :] = ar * pr - ai * pi + hre_scr[cur, :]
        him_scr[cur, :] = ar * pi + ai * pr + him_scr[cur, :]
        return carry

    lax.fori_loop(0, tl, step, 0)
    hre_ref[...] = hre_scr[(final_step + 1) * nb:(final_step + 2) * nb, :]
    him_ref[...] = him_scr[(final_step + 1) * nb:(final_step + 2) * nb, :]
    y = _dot(hre_scr[nb:, :], cre_ref[...]) - _dot(him_scr[nb:, :], cim_ref[...]) + d_ref[...] * u
    y = 0.5 * y * (1.0 + jnp.tanh(0.7978845608028654 * (y + 0.044715 * (y * y * y))))
    y_ref[...] = y * _sigmoid(_dot(y, gw_ref[...]) + gb_ref[...])


def _s5(u_tm, st_re, st_im, l, a_re, a_im, log_dt, b_re, b_im, c_re, c_im, d, glu_w, glu_b, *, nb, tl,
        final_step):
    rows = u_tm.shape[0]
    L = rows // nb
    eye = jnp.eye(S5_GROUPS, dtype=_F32)
    bd_in = lambda b: jnp.einsum('gpc,gh->gchp', b, eye).reshape(S5_WIDTH, S5_FLAT)
    bd_out = lambda c: jnp.einsum('gcp,gh->gphc', c, eye).reshape(S5_FLAT, S5_WIDTH)
    cmap = lambda t: (0, 0)
    row = lambda n: pl.BlockSpec((1, n), cmap)
    full = lambda a, b: pl.BlockSpec((a, b), cmap)
    state = pl.BlockSpec((None, nb, S5_FLAT), lambda t: (l, 0, 0))
    return pl.pallas_call(
        functools.partial(_s5_kernel, nb=nb, tl=tl, final_step=final_step),
        grid=(L // tl,),
        in_specs=[pl.BlockSpec((tl * nb, S5_WIDTH), lambda t: (t, 0)), state, state,
                  row(S5_FLAT), row(S5_FLAT), row(S5_FLAT),
                  full(S5_WIDTH, S5_FLAT), full(S5_WIDTH, S5_FLAT), full(S5_FLAT, S5_WIDTH), full(S5_FLAT, S5_WIDTH),
                  row(S5_WIDTH), full(S5_WIDTH, S5_WIDTH), row(S5_WIDTH)],
        out_specs=[pl.BlockSpec((tl * nb, S5_WIDTH), lambda t: (t, 0)), full(nb, S5_FLAT), full(nb, S5_FLAT)],
        out_shape=[jax.ShapeDtypeStruct((rows, S5_WIDTH), _F32),
                   jax.ShapeDtypeStruct((nb, S5_FLAT), _F32), jax.ShapeDtypeStruct((nb, S5_FLAT), _F32)],
        scratch_shapes=[pltpu.VMEM(((tl + 1) * nb, S5_FLAT), _F32), pltpu.VMEM(((tl + 1) * nb, S5_FLAT), _F32)],
        compiler_params=_params("arbitrary"),
    )(u_tm, st_re, st_im, a_re.reshape(1, S5_FLAT), a_im.reshape(1, S5_FLAT),
      jnp.repeat(log_dt, S5_STATE).reshape(1, S5_FLAT), bd_in(b_re), bd_in(b_im), bd_out(c_re), bd_out(c_im),
      d.reshape(1, S5_WIDTH), glu_w, glu_b.reshape(1, S5_WIDTH))


def _out_mlp_kernel(x_ref, yssd_ref, yrw_ref, ys5_ref, g1_ref, sh2_ref, sc2_ref, g2_ref, n2g_ref, fg_ref,
                    wo_ref, w1_ref, w2_ref, o_ref, x1_scr, h2_scr, acc_scr, *, final_norm):
    k = pl.program_id(2)
    bb, tl, _ = x_ref.shape
    rows = bb * tl

    @pl.when(k == 0)
    def _():
        flat = lambda ref: ref[...].reshape(rows, ref.shape[-1]).astype(_BF16)
        mix = (_dot(flat(yssd_ref), wo_ref[0:512, :]) + _dot(flat(yrw_ref), wo_ref[512:768, :])
               + _dot(flat(ys5_ref), wo_ref[768:1024, :]))
        x1 = x_ref[...] + g1_ref[...] * mix.reshape(bb, tl, D_MODEL)
        x1_scr[...] = x1
        xn = x1 * lax.rsqrt(jnp.mean(x1 * x1, axis=-1, keepdims=True) + NORM_EPS)
        h2 = xn * n2g_ref[...] * (1.0 + sc2_ref[...]) + sh2_ref[...]
        h2_scr[...] = h2.reshape(rows, D_MODEL).astype(_BF16)
        acc_scr[...] = jnp.zeros_like(acc_scr)

    hid = _dot(h2_scr[...], w1_ref[...])
    hid = jnp.square(jnp.maximum(hid, 0.0)).astype(_BF16)
    acc_scr[...] += _dot(hid, w2_ref[...])

    @pl.when(k == pl.num_programs(2) - 1)
    def _():
        out = x1_scr[...] + g2_ref[...] * acc_scr[...].reshape(bb, tl, D_MODEL)
        if final_norm:
            out = out * lax.rsqrt(jnp.mean(out * out, axis=-1, keepdims=True) + NORM_EPS) * fg_ref[...]
        o_ref[...] = out


def _out_mlp(x, y_ssd, y_rw, y_s5, mod, norm2_g, final_g, w_out_b, w1_b, w2_b, *, bb, tl, time_major_s5,
             final_norm):
    nb, L, _ = x.shape
    grid = (nb // bb, L // tl, D_FF // FF_TILE)
    xmap = lambda i, t, k: (i, t, 0)
    cmap = lambda i, t, k: (0, 0)

    def modspec(j):
        return pl.BlockSpec((bb, 1, D_MODEL), lambda i, t, k: (i, 0, j))

    s5_spec = (pl.BlockSpec((tl, S5_WIDTH), lambda i, t, k: (t, i)) if time_major_s5
               else pl.BlockSpec((bb, tl, S5_WIDTH), xmap))
    return pl.pallas_call(
        functools.partial(_out_mlp_kernel, final_norm=final_norm),
        grid=grid,
        in_specs=[pl.BlockSpec((bb, tl, D_MODEL), xmap), pl.BlockSpec((bb, tl, SSD_WIDTH), xmap),
                  pl.BlockSpec((bb, tl, RWKV_WIDTH), xmap), s5_spec,
                  modspec(2), modspec(3), modspec(4), modspec(5),
                  pl.BlockSpec((1, D_MODEL), cmap), pl.BlockSpec((1, D_MODEL), cmap),
                  pl.BlockSpec((D_MODEL, D_MODEL), cmap),
                  pl.BlockSpec((D_MODEL, FF_TILE), lambda i, t, k: (0, k)),
                  pl.BlockSpec((FF_TILE, D_MODEL), lambda i, t, k: (k, 0))],
        out_specs=pl.BlockSpec((bb, tl, D_MODEL), xmap),
        out_shape=jax.ShapeDtypeStruct(x.shape, _F32),
        scratch_shapes=[pltpu.VMEM((bb, tl, D_MODEL), _F32), pltpu.VMEM((bb * tl, D_MODEL), _BF16),
                        pltpu.VMEM((bb * tl, D_MODEL), _F32)],
        compiler_params=_params("arbitrary", "arbitrary", "arbitrary"),
    )(x, y_ssd, y_rw, y_s5, mod, mod, mod, mod, norm2_g.reshape(1, D_MODEL), final_g.reshape(1, D_MODEL),
      w_out_b, w1_b, w2_b)


def _trunk(x, mod, states, W, final_g, *, prompt):
    st_ssd, st_conv, st_rwkv, st_shift, st_re, st_im = states
    nb, L, _ = x.shape
    if prompt:
        bb, tl = 1, min(ROW_TILE, L)
        ssd_t, rwkv_t, s5_tl = min(SSD_CHUNK, L), min(RWKV_CHUNK, L), min(S5_TILE, L)
        mix_bb = 1
    else:
        bb, tl = min(nb, ROW_TILE // L), L
        ssd_t = rwkv_t = s5_tl = L
        mix_bb = min(nb, 8)
    new = ([], [], [], [], [], [])
    for l in range(DEPTH):
        z, xbc, frw, us5, dt = _in_proj(x, mod[l], W['norm1_g'][l], W['w_in_p'][l], bb=bb, tl=tl,
                                        time_major_s5=prompt)
        v = W['valid']
        y_ssd, n_ssd, n_conv = _ssd(z, xbc, dt, st_ssd, st_conv, l, W['ssd_conv_w'][l], W['ssd_conv_b'][l],
                                    W['ssd_dt_bias'][l], W['ssd_a_log'][l], W['ssd_d'][l], W['ssd_norm_g'][l],
                                    bb=mix_bb, T=ssd_t, valid=ssd_t if prompt else v)
        y_rw, n_rwkv, n_shift = _rwkv(frw, st_rwkv, st_shift, l, W['rwkv_mu'][l], W['rwkv_w0'][l],
                                      W['rwkv_w2'][l], W['rwkv_a0'][l], W['rwkv_a2'][l], W['rwkv_g2'][l],
                                      W['rwkv_k_k'][l], W['rwkv_k_a'][l], W['rwkv_r_k'][l], W['rwkv_ln_g'][l],
                                      W['rwkv_ln_b'][l], bb=mix_bb, T=rwkv_t, valid=rwkv_t if prompt else v)
        if prompt:
            u_tm = us5.reshape(L * nb, S5_WIDTH)
        else:
            u_tm = jnp.transpose(us5, (1, 0, 2)).reshape(L * nb, S5_WIDTH)
        y_s5, n_re, n_im = _s5(u_tm, st_re, st_im, l, W['s5_a_re'][l], W['s5_a_im'][l], W['s5_log_dt'][l],
                               W['s5_b_re'][l], W['s5_b_im'][l], W['s5_c_re'][l], W['s5_c_im'][l], W['s5_d'][l],
                               W['s5_glu_w'][l], W['s5_glu_b'][l], nb=nb, tl=s5_tl,
                               final_step=(s5_tl - 1) if prompt else (v - 1))
        if prompt:
            y_s5 = y_s5.reshape(L, nb * S5_WIDTH)
        else:
            y_s5 = jnp.transpose(y_s5.reshape(L, nb, S5_WIDTH), (1, 0, 2))
        x = _out_mlp(x, y_ssd, y_rw, y_s5, mod[l], W['norm2_g'][l], final_g, W['w_out_b'][l], W['w1_b'][l],
                     W['w2_b'][l], bb=bb, tl=tl, time_major_s5=prompt, final_norm=(l == DEPTH - 1))
        for lst, s in zip(new, (n_ssd, n_conv, n_rwkv, n_shift, n_re, n_im)):
            lst.append(s)
    return x, [jnp.stack(lst) for lst in new]


def _pack_states(st_ssd, st_conv, st_rwkv, st_shift, st_re, st_im):
    d, nb = st_ssd.shape[:2]
    conv8 = jnp.pad(st_conv, ((0, 0), (0, 0), (SUBLANES - (SSD_CONV - 1), 0), (0, 0)))
    shift8 = jnp.pad(st_shift[:, :, None, :], ((0, 0), (0, 0), (SUBLANES - 1, 0), (0, 0)))
    return (st_ssd, conv8, st_rwkv, shift8, st_re.reshape(d, nb, S5_FLAT), st_im.reshape(d, nb, S5_FLAT))


def _unpack_states(n_ssd, n_conv8, n_rwkv, n_shift8, n_re, n_im):
    d, nb = n_ssd.shape[:2]
    return (n_ssd, n_conv8[:, :, SUBLANES - (SSD_CONV - 1):, :], n_rwkv, n_shift8[:, :, SUBLANES - 1, :],
            n_re.reshape(d, nb, S5_GROUPS, S5_STATE), n_im.reshape(d, nb, S5_GROUPS, S5_STATE))


def kernel(x_prompt, x_sample, c_prompt, c_sample, state_ssd, state_ssd_conv, state_rwkv, state_rwkv_shift,
           state_s5_re, state_s5_im, ada_w, ada_b, norm1_g, norm2_g, w_in, ssd_conv_w, ssd_conv_b, ssd_dt_bias,
           ssd_a_log, ssd_d, ssd_norm_g, rwkv_mu, rwkv_w0, rwkv_w2, rwkv_a0, rwkv_a2, rwkv_g2, rwkv_k_k, rwkv_k_a,
           rwkv_r_k, rwkv_ln_g, rwkv_ln_b, s5_a_re, s5_a_im, s5_log_dt, s5_b_re, s5_b_im, s5_c_re, s5_c_im, s5_d,
           s5_glu_w, s5_glu_b, w_out, mlp_w1, mlp_w2, final_g):
    bp, lp, _ = x_prompt.shape
    bs, ls, _ = x_sample.shape
    ls_pad = -(-ls // SUBLANES) * SUBLANES
    w_in_p = jnp.concatenate(
        [w_in[:, :, _Z0:_DT0], w_in[:, :, _FRW0:_IN_COLS], w_in[:, :, _DT0:_FRW0],
         jnp.zeros((DEPTH, D_MODEL, DT_PAD - SSD_HEADS), w_in.dtype)], axis=-1).astype(_BF16)
    W = dict(norm1_g=norm1_g, norm2_g=norm2_g, w_in_p=w_in_p, ssd_conv_w=ssd_conv_w, ssd_conv_b=ssd_conv_b,
             ssd_dt_bias=ssd_dt_bias, ssd_a_log=ssd_a_log, ssd_d=ssd_d, ssd_norm_g=ssd_norm_g, rwkv_mu=rwkv_mu,
             rwkv_w0=rwkv_w0, rwkv_w2=rwkv_w2, rwkv_a0=rwkv_a0, rwkv_a2=rwkv_a2, rwkv_g2=rwkv_g2,
             rwkv_k_k=rwkv_k_k, rwkv_k_a=rwkv_k_a, rwkv_r_k=rwkv_r_k.reshape(DEPTH, RWKV_WIDTH),
             rwkv_ln_g=rwkv_ln_g, rwkv_ln_b=rwkv_ln_b, s5_a_re=s5_a_re, s5_a_im=s5_a_im, s5_log_dt=s5_log_dt,
             s5_b_re=s5_b_re, s5_b_im=s5_b_im, s5_c_re=s5_c_re, s5_c_im=s5_c_im, s5_d=s5_d, s5_glu_w=s5_glu_w,
             s5_glu_b=s5_glu_b, w_out_b=w_out.astype(_BF16), w1_b=mlp_w1.astype(_BF16), w2_b=mlp_w2.astype(_BF16),
             valid=ls)
    mod = _ada(jnp.concatenate([c_prompt, c_sample], axis=0), ada_w, ada_b)
    mod_p = mod[:, :bp].reshape(DEPTH, bp, 1, 6 * D_MODEL)
    mod_s = mod[:, bp:].reshape(DEPTH, bs, 1, 6 * D_MODEL)

    zeros = lambda shape: jnp.zeros((DEPTH, bp) + shape, _F32)
    p_states = _pack_states(zeros((SSD_HEADS, SSD_HEAD_DIM, SSD_STATE)), zeros((SSD_CONV - 1, SSD_CONV_DIM)),
                            zeros((RWKV_HEADS, RWKV_HEAD_DIM, RWKV_HEAD_DIM)), zeros((RWKV_PROJ,)),
                            zeros((S5_GROUPS, S5_STATE)), zeros((S5_GROUPS, S5_STATE)))
    y_prompt, sp = _trunk(x_prompt, mod_p, p_states, W, final_g, prompt=True)

    s_states = _pack_states(state_ssd, state_ssd_conv, state_rwkv, state_rwkv_shift, state_s5_re, state_s5_im)
    x_s = jnp.pad(x_sample, ((0, 0), (0, ls_pad - ls), (0, 0)))
    y_sample, ss = _trunk(x_s, mod_s, s_states, W, final_g, prompt=False)
    return (y_prompt, y_sample[:, :ls]) + _unpack_states(*sp) + _unpack_states(*ss)
```

```python
import functools

import jax
import jax.numpy as jnp
from jax import lax
from jax.experimental import pallas as pl
from jax.experimental.pallas import tpu as pltpu

D_MODEL = 1024
DEPTH = 2
SSD_WIDTH = 512
SSD_HEAD_DIM = 64
SSD_HEADS = 8
SSD_GROUPS = 2
SSD_HEADS_PER_GROUP = SSD_HEADS // SSD_GROUPS
SSD_STATE = 128
SSD_CONV = 4
SSD_CONV_DIM = SSD_WIDTH + 2 * SSD_GROUPS * SSD_STATE
RWKV_WIDTH = 256
RWKV_HEAD_DIM = 64
RWKV_HEADS = 4
RWKV_DECAY_LORA = 64
RWKV_A_LORA = 64
RWKV_GATE_LORA = 128
RWKV_PROJ = 3 * RWKV_WIDTH + RWKV_DECAY_LORA + RWKV_A_LORA + RWKV_GATE_LORA
RWKV_LN_EPS = 64e-5
S5_WIDTH = 256
S5_GROUP_CH = 16
S5_GROUPS = 16
S5_STATE = 64
S5_FLAT = S5_GROUPS * S5_STATE
D_FF = 4 * D_MODEL
NORM_EPS = 1e-6
DT_PAD = 128
SUBLANES = 8

_Z0, _XBC0, _DT0, _FRW0, _US50 = 0, SSD_WIDTH, SSD_WIDTH + SSD_CONV_DIM, SSD_WIDTH + SSD_CONV_DIM + SSD_HEADS, \
    SSD_WIDTH + SSD_CONV_DIM + SSD_HEADS + RWKV_PROJ
_IN_COLS = _US50 + S5_WIDTH
_PZ, _PXBC, _PFRW, _PUS5, _PDT = 0, 512, 1536, 2560, 2816
_P_COLS = _PDT + DT_PAD

SSD_CHUNK = 128
RWKV_CHUNK = 64
SSD_PROMPT_SEQS = 2
RWKV_PROMPT_SEQS = 4
SAMPLE_SEQS = 8
S5_TILE = 256
ROW_TILE = 512
MLP_ROW_TILE = 1024
FF_TILE = 512
VMEM_LIMIT = 56 * 1024 * 1024

_HI = lax.Precision.HIGHEST
P_AMAT = 1
P_INV = 1
P_S0 = 1
P_INTRA = 1
P_SOLVE = 1
P_STATE = 1
_F32 = jnp.float32
_BF16 = jnp.bfloat16


_NN = (((1,), (0,)), ((), ()))
_NT = (((1,), (1,)), ((), ()))
_TN = (((0,), (0,)), ((), ()))


def _split(x):
    hi = x.astype(_BF16)
    return hi, (x - hi.astype(_F32)).astype(_BF16)


def _mm(a, b, dims, passes):
    dg = lambda x, y, prec=None: lax.dot_general(x, y, dims, preferred_element_type=_F32, precision=prec)
    if passes == 6:
        return dg(a.astype(_F32), b.astype(_F32), _HI)
    if passes == 3:
        ah, al = _split(a)
        bh, bl = _split(b)
        return dg(ah, bh) + (dg(ah, bl) + dg(al, bh))
    return dg(a.astype(_BF16), b.astype(_BF16))


def _dot(a, b, passes=1):
    return _mm(a, b, _NN, passes)


def _dot_nt(a, b, passes=1):
    return _mm(a, b, _NT, passes)


def _dot_tn(a, b, passes=1):
    return _mm(a, b, _TN, passes)


def _mm_sel(sel, x, dims, x_is_rhs=True):
    hi = x.astype(_BF16)
    rest = x - hi.astype(_F32)
    mid = rest.astype(_BF16)
    lo = (rest - mid.astype(_F32)).astype(_BF16)
    sel = sel.astype(_BF16)
    if x_is_rhs:
        dg = lambda y: lax.dot_general(sel, y, dims, preferred_element_type=_F32)
    else:
        dg = lambda y: lax.dot_general(y, sel, dims, preferred_element_type=_F32)
    return dg(hi) + (dg(mid) + dg(lo))


def _iota(shape, dim):
    return lax.broadcasted_iota(jnp.int32, shape, dim)


def _sigmoid(x):
    return 1.0 / (1.0 + jnp.exp(-x))


def _softplus(x):
    return jnp.maximum(x, 0.0) + jnp.log1p(jnp.exp(-jnp.abs(x)))


def _silu(x):
    return x * _sigmoid(x)


def _params(*sems):
    return pltpu.CompilerParams(dimension_semantics=sems, vmem_limit_bytes=VMEM_LIMIT)


def _ada_kernel(c_ref, w_ref, b_ref, o_ref):
    c = c_ref[...]
    s = _silu(c).astype(_BF16)
    o_ref[0] = _dot(s, w_ref[0].astype(_BF16)) + b_ref[0]


def _ada(c_all, ada_w, ada_b):
    rows = c_all.shape[0]
    tn = 1536
    return pl.pallas_call(
        _ada_kernel,
        grid=(DEPTH, 6 * D_MODEL // tn),
        in_specs=[pl.BlockSpec((rows, D_MODEL), lambda l, j: (0, 0)),
                  pl.BlockSpec((1, D_MODEL, tn), lambda l, j: (l, 0, j)),
                  pl.BlockSpec((1, 1, tn), lambda l, j: (l, 0, j))],
        out_specs=pl.BlockSpec((1, rows, tn), lambda l, j: (l, 0, j)),
        out_shape=jax.ShapeDtypeStruct((DEPTH, rows, 6 * D_MODEL), _F32),
        compiler_params=_params("arbitrary", "arbitrary"),
    )(c_all, ada_w, ada_b.reshape(DEPTH, 1, 6 * D_MODEL))


def _in_proj_kernel(x_ref, sh_ref, sc_ref, g_ref, w_ref, z_ref, xbc_ref, frw_ref, us5_ref, dt_ref):
    x = x_ref[...]
    bb, tl, _ = x.shape
    xn = x * lax.rsqrt(jnp.mean(x * x, axis=-1, keepdims=True) + NORM_EPS)
    h = xn * g_ref[...] * (1.0 + sc_ref[...]) + sh_ref[...]
    h = h.reshape(bb * tl, D_MODEL).astype(_BF16)
    for o_ref, lo, hi in ((z_ref, _PZ, _PXBC), (xbc_ref, _PXBC, _PFRW), (frw_ref, _PFRW, _PUS5),
                          (us5_ref, _PUS5, _PDT), (dt_ref, _PDT, _P_COLS)):
        o_ref[...] = _dot(h, w_ref[:, lo:hi]).reshape(o_ref.shape)


def _in_proj(x, mod, norm_g, w_in_p, l, *, bb, tl, time_major_s5):
    nb, L, _ = x.shape
    grid = (nb // bb, L // tl)
    xmap = lambda i, t: (i, t, 0)

    def modspec(j):
        return pl.BlockSpec((bb, 1, D_MODEL), lambda i, t: (i, 0, j))

    def out(n):
        return jax.ShapeDtypeStruct((nb, L, n), _F32), pl.BlockSpec((bb, tl, n), xmap)

    outs = [out(SSD_WIDTH), out(SSD_CONV_DIM), out(RWKV_PROJ), out(S5_WIDTH), out(DT_PAD)]
    if time_major_s5:
        assert bb == 1
        outs[3] = (jax.ShapeDtypeStruct((L, nb * S5_WIDTH), _F32), pl.BlockSpec((tl, S5_WIDTH), lambda i, t: (t, i)))
    return pl.pallas_call(
        _in_proj_kernel,
        grid=grid,
        in_specs=[pl.BlockSpec((bb, tl, D_MODEL), xmap), modspec(0), modspec(1),
                  pl.BlockSpec((1, D_MODEL), lambda i, t: (0, 0)),
                  pl.BlockSpec((None, D_MODEL, _P_COLS), lambda i, t: (l, 0, 0))],
        out_specs=[o[1] for o in outs],
        out_shape=[o[0] for o in outs],
        compiler_params=_params("arbitrary", "arbitrary"),
    )(x, mod, mod, norm_g.reshape(1, D_MODEL), w_in_p)


def _ssd_seq(j, z_ref, xbc_ref, dt_ref, cw_ref, cb_ref, dtb_ref, alog_ref, dexp_ref, ng_ref, eexp_ref,
             y_ref, sto_ref, cvo_ref, ext_scr, ybuf_scr, T, valid):
    u = xbc_ref[j]
    ext_scr[j, 0:SUBLANES, :] = cvo_ref[j]
    ext_scr[j, SUBLANES:SUBLANES + T, :] = u
    conv = cb_ref[...] + cw_ref[3:4, :] * u
    for back in (1, 2, 3):
        conv = conv + cw_ref[3 - back:4 - back, :] * ext_scr[j, SUBLANES - back:SUBLANES - back + T, :]
    cvo_ref[j] = ext_scr[j, valid:valid + SUBLANES, :]
    xbc = _silu(conv)
    xs = xbc[:, :SSD_WIDTH]
    row = _iota((T, 1), 0)
    dtv = _softplus(dt_ref[j] + dtb_ref[...])
    if valid < T:
        dtv = jnp.where(row < valid, dtv, 0.0)
    a_row = jnp.where(_iota((1, DT_PAD), 1) < SSD_HEADS, -jnp.exp(alog_ref[...]), 0.0)
    a = dtv * a_row
    yield
    tri_b = _iota((T, T), 0) >= _iota((T, T), 1)
    cs = _mm_sel(tri_b, a, _NN)
    eye = _iota((DT_PAD, DT_PAD), 0) == _iota((DT_PAD, DT_PAD), 1)
    xdt = xs * _mm_sel(eexp_ref[...], dtv, _NN, x_is_rhs=False)
    yield
    cs_t = _mm_sel(eye, cs, _NT)
    ecs = jnp.exp(cs)
    cs_last = cs[T - 1:T, :]
    dte = jnp.exp(cs_last - cs)
    elast = jnp.exp(cs_last)
    for g in range(SSD_GROUPS):
        bm = xbc[:, SSD_WIDTH + g * SSD_STATE:SSD_WIDTH + (g + 1) * SSD_STATE]
        cm = xbc[:, SSD_WIDTH + (SSD_GROUPS + g) * SSD_STATE:SSD_WIDTH + (SSD_GROUPS + g + 1) * SSD_STATE]
        cb = _dot_nt(cm, bm)
        for r in range(SSD_HEADS_PER_GROUP):
            yield
            h = g * SSD_HEADS_PER_GROUP + r
            lo, hi = h * SSD_HEAD_DIM, (h + 1) * SSD_HEAD_DIM
            decay = jnp.exp(jnp.where(tri_b, cs[:, h:h + 1] - cs_t[h:h + 1, :], -jnp.inf))
            xdt_h = xdt[:, lo:hi]
            st = sto_ref[j, h]
            y_h = _dot(cb * decay, xdt_h) + ecs[:, h:h + 1] * _dot_nt(cm, st)
            ybuf_scr[j, :, lo:hi] = y_h
            sto_ref[j, h] = elast[:, h:h + 1] * st + _dot_tn(xdt_h * dte[:, h:h + 1], bm)
    yield
    y = ybuf_scr[j] + xs * dexp_ref[...]
    y = y * _silu(z_ref[j])
    y = y * lax.rsqrt(jnp.mean(y * y, axis=-1, keepdims=True) + NORM_EPS) * ng_ref[...]
    y_ref[j] = y


def _ssd_kernel(z_ref, xbc_ref, dt_ref, st_ref, cv_ref, cw_ref, cb_ref, dtb_ref, alog_ref, dexp_ref, ng_ref,
                eexp_ref, y_ref, sto_ref, cvo_ref, ext_scr, ybuf_scr, *, bb, T, valid):
    @pl.when(pl.program_id(1) == 0)
    def _():
        sto_ref[...] = st_ref[...]
        cvo_ref[...] = cv_ref[...]

    seq = functools.partial(_ssd_seq, z_ref=z_ref, xbc_ref=xbc_ref, dt_ref=dt_ref, cw_ref=cw_ref, cb_ref=cb_ref,
                            dtb_ref=dtb_ref, alog_ref=alog_ref, dexp_ref=dexp_ref, ng_ref=ng_ref, eexp_ref=eexp_ref,
                            y_ref=y_ref, sto_ref=sto_ref, cvo_ref=cvo_ref, ext_scr=ext_scr, ybuf_scr=ybuf_scr,
                            T=T, valid=valid)
    _interleave(seq(j) for j in range(bb))


def _ssd(z, xbc, dt, st_ssd, st_conv8, l, conv_w, conv_b, dt_bias, a_log, d, norm_g, *, bb, T, valid):
    nb, L, _ = z.shape
    grid = (nb // bb, L // T)
    tmap = lambda i, c: (i, c, 0)
    cmap2 = lambda i, c: (0, 0)
    smap = lambda i, c: (i, 0, 0, 0)
    cvmap = lambda i, c: (i, 0, 0)
    pad = DT_PAD - SSD_HEADS
    eexp = jnp.repeat(jnp.eye(DT_PAD, SSD_HEADS, dtype=_F32), SSD_HEAD_DIM, axis=1)
    return pl.pallas_call(
        functools.partial(_ssd_kernel, bb=bb, T=T, valid=valid),
        grid=grid,
        in_specs=[pl.BlockSpec((bb, T, SSD_WIDTH), tmap), pl.BlockSpec((bb, T, SSD_CONV_DIM), tmap),
                  pl.BlockSpec((bb, T, DT_PAD), tmap),
                  pl.BlockSpec((None, bb, SSD_HEADS, SSD_HEAD_DIM, SSD_STATE), lambda i, c: (l, i, 0, 0, 0)),
                  pl.BlockSpec((None, bb, SUBLANES, SSD_CONV_DIM), lambda i, c: (l, i, 0, 0)),
                  pl.BlockSpec((SSD_CONV, SSD_CONV_DIM), cmap2), pl.BlockSpec((1, SSD_CONV_DIM), cmap2),
                  pl.BlockSpec((1, DT_PAD), cmap2), pl.BlockSpec((1, DT_PAD), cmap2),
                  pl.BlockSpec((1, SSD_WIDTH), cmap2), pl.BlockSpec((1, SSD_WIDTH), cmap2),
                  pl.BlockSpec((DT_PAD, SSD_WIDTH), cmap2)],
        out_specs=[pl.BlockSpec((bb, T, SSD_WIDTH), tmap),
                   pl.BlockSpec((bb, SSD_HEADS, SSD_HEAD_DIM, SSD_STATE), smap),
                   pl.BlockSpec((bb, SUBLANES, SSD_CONV_DIM), cvmap)],
        out_shape=[jax.ShapeDtypeStruct((nb, L, SSD_WIDTH), _F32),
                   jax.ShapeDtypeStruct(st_ssd.shape[1:], _F32),
                   jax.ShapeDtypeStruct(st_conv8.shape[1:], _F32)],
        scratch_shapes=[pltpu.VMEM((bb, T + SUBLANES, SSD_CONV_DIM), _F32), pltpu.VMEM((bb, T, SSD_WIDTH), _F32)],
        compiler_params=_params("arbitrary", "arbitrary"),
    )(z, xbc, dt, st_ssd, st_conv8, conv_w, conv_b.reshape(1, -1),
      jnp.pad(dt_bias, (0, pad)).reshape(1, DT_PAD), jnp.pad(a_log, (0, pad)).reshape(1, DT_PAD),
      jnp.repeat(d, SSD_HEAD_DIM).reshape(1, SSD_WIDTH), norm_g.reshape(1, SSD_WIDTH), eexp)


def _stack_heads(x):
    return jnp.concatenate([x[:, h * RWKV_HEAD_DIM:(h + 1) * RWKV_HEAD_DIM] for h in range(RWKV_HEADS)], axis=0)


def _unstack_heads(x, T):
    return jnp.concatenate([x[h * T:(h + 1) * T, :] for h in range(RWKV_HEADS)], axis=1)


def _interleave(gens):
    gens = list(gens)
    while gens:
        alive = []
        for g in gens:
            try:
                next(g)
                alive.append(g)
            except StopIteration:
                pass
        gens = alive


def _rwkv_masks(T):
    HT = RWKV_HEADS * T
    rr = _iota((HT, HT), 0)
    cc = _iota((HT, HT), 1)
    same = (rr // T) == (cc // T)
    head_of_row = _iota((HT, RWKV_WIDTH), 0) // T
    head_of_col = _iota((HT, RWKV_WIDTH), 1) // RWKV_HEAD_DIM
    return dict(strict=jnp.logical_and(same, rr > cc), incl=jnp.logical_and(same, rr >= cc),
                eye=(rr == cc).astype(_F32), hmask=head_of_row == head_of_col,
                tri=(_iota((T, T), 0) >= _iota((T, T), 1)).astype(_BF16))


def _rwkv_seq(j, masks, f_ref, mu_ref, w0_ref, w2_ref, a0_ref, a2_ref, g2_ref, kk_ref, ka_ref, rk_ref, lng_ref,
              lnb_ref, y_ref, so_ref, sho_ref, ext_scr, T, valid):
    strict, incl, hmask = masks['strict'], masks['incl'], masks['hmask']
    f = f_ref[j]
    ext_scr[j, 0:SUBLANES, :] = sho_ref[j]
    ext_scr[j, SUBLANES:SUBLANES + T, :] = f
    prev = ext_scr[j, SUBLANES - 1:SUBLANES - 1 + T, :]
    sho_ref[j] = ext_scr[j, valid:valid + SUBLANES, :]
    fm = f + (prev - f) * mu_ref[...]
    r = fm[:, 0:256]
    k = fm[:, 256:512]
    v = fm[:, 512:768]
    lora = fm[:, 768:896]
    gl = fm[:, 896:1024]
    w_log = -_softplus(-(w0_ref[...] + _dot(jnp.tanh(lora), w2_ref[...]))) - 0.5
    logw = -jnp.exp(w_log)
    a = _sigmoid(a0_ref[...] + _dot(lora, a2_ref[...]))
    g = _dot(_sigmoid(gl), g2_ref[...])
    k2 = k * (1.0 + (a - 1.0) * ka_ref[...])
    kk4 = _stack_heads(k * kk_ref[...])
    kk4 = kk4 * lax.rsqrt(jnp.maximum(jnp.sum(kk4 * kk4, axis=-1, keepdims=True), 1e-24))
    kkn = _unstack_heads(kk4, T)
    if valid < T:
        live = _iota((T, 1), 0) < valid
        logw = jnp.where(live, logw, 0.0)
        kkn = jnp.where(live, kkn, 0.0)
        k2 = jnp.where(live, k2, 0.0)
    yield
    cw = _mm_sel(masks['tri'], logw, _NN)
    w_inc = jnp.exp(cw)
    w_exc = jnp.exp(cw - logw)
    w_inv = jnp.exp(-cw)
    w_end = w_inc[T - 1:T, :]
    at = -kkn * w_exc
    rt = r * w_inc
    bt = kkn * a * w_inv
    kt = k2 * w_inv
    a4, r4, b4, k4, v4 = (_stack_heads(t) for t in (at, rt, bt, kt, v))
    yield
    n_ab = jnp.where(strict, _dot_nt(a4, b4, P_AMAT), 0.0)
    n_ak = jnp.where(strict, _dot_nt(a4, k4, P_AMAT), 0.0)
    m_rb = jnp.where(incl, _dot_nt(r4, b4, P_AMAT), 0.0)
    m_rk = jnp.where(incl, _dot_nt(r4, k4, P_AMAT), 0.0)
    inv = masks['eye'] + n_ab
    p = n_ab
    for _ in range(max((T - 1).bit_length() - 1, 0)):
        yield
        p = _dot(p, p, P_INV)
        yield
        inv = inv + _dot(inv, p, P_INV)

    def wide(x):
        return jnp.where(hmask, jnp.concatenate([x] * RWKV_HEADS, axis=0), 0.0)

    yield
    s0 = jnp.concatenate([so_ref[j, h] for h in range(RWKV_HEADS)], axis=1)
    rhs = _dot_nt(wide(at), s0, P_S0) + _dot(n_ak, v4, P_INTRA)
    o4 = _dot_nt(wide(rt), s0, P_S0) + _dot(m_rk, v4, P_INTRA)
    yield
    u4 = _dot(inv, rhs, P_SOLVE)
    yield
    o4 = o4 + _dot(m_rb, u4, P_INTRA)
    s1 = (s0 * w_end + _dot_tn(u4, wide(bt * w_end), P_STATE) + _dot_tn(v4, wide(kt * w_end), P_STATE))
    for h in range(RWKV_HEADS):
        so_ref[j, h] = s1[:, h * RWKV_HEAD_DIM:(h + 1) * RWKV_HEAD_DIM]
    yield

    mean = jnp.mean(o4, axis=-1, keepdims=True)
    var = jnp.mean(jnp.square(o4 - mean), axis=-1, keepdims=True)
    o4 = (o4 - mean) * lax.rsqrt(var + RWKV_LN_EPS)
    bonus4 = jnp.sum(_stack_heads(r * k2 * rk_ref[...]), axis=-1, keepdims=True) * v4
    o = _unstack_heads(o4, T) * lng_ref[...] + lnb_ref[...] + _unstack_heads(bonus4, T)
    y_ref[j] = o * g


def _rwkv_kernel(f_ref, s_ref, sh_ref, mu_ref, w0_ref, w2_ref, a0_ref, a2_ref, g2_ref, kk_ref, ka_ref, rk_ref,
                 lng_ref, lnb_ref, y_ref, so_ref, sho_ref, ext_scr, *, bb, T, valid):
    @pl.when(pl.program_id(1) == 0)
    def _():
        so_ref[...] = s_ref[...]
        sho_ref[...] = sh_ref[...]

    seq = functools.partial(_rwkv_seq, f_ref=f_ref, mu_ref=mu_ref, w0_ref=w0_ref, w2_ref=w2_ref, a0_ref=a0_ref,
                            a2_ref=a2_ref, g2_ref=g2_ref, kk_ref=kk_ref, ka_ref=ka_ref, rk_ref=rk_ref,
                            lng_ref=lng_ref, lnb_ref=lnb_ref, y_ref=y_ref, so_ref=so_ref, sho_ref=sho_ref,
                            ext_scr=ext_scr, T=T, valid=valid)
    masks = _rwkv_masks(T)
    _interleave(seq(j, masks) for j in range(bb))


def _rwkv(f, st_rwkv, shift8, l, mu, w0, w2, a0, a2, g2, k_k, k_a, r_k, ln_g, ln_b, *, bb, T, valid):
    nb, L, _ = f.shape
    hd = RWKV_HEAD_DIM
    grid = (nb // bb, L // T)
    tmap = lambda i, c: (i, c, 0)
    bmap = lambda i, c: (i, 0, 0)
    cmap2 = lambda i, c: (0, 0)
    row = lambda n: pl.BlockSpec((1, n), cmap2)
    lora_w = RWKV_DECAY_LORA + RWKV_A_LORA
    w2_pad = jnp.concatenate([w2, jnp.zeros((RWKV_A_LORA, RWKV_WIDTH), _F32)], axis=0)
    a2_pad = jnp.concatenate([jnp.zeros((RWKV_DECAY_LORA, RWKV_WIDTH), _F32), a2], axis=0)
    return pl.pallas_call(
        functools.partial(_rwkv_kernel, bb=bb, T=T, valid=valid),
        grid=grid,
        in_specs=[pl.BlockSpec((bb, T, RWKV_PROJ), tmap),
                  pl.BlockSpec((None, bb, RWKV_HEADS, hd, hd), lambda i, c: (l, i, 0, 0, 0)),
                  pl.BlockSpec((None, bb, SUBLANES, RWKV_PROJ), lambda i, c: (l, i, 0, 0)),
                  row(RWKV_PROJ), row(RWKV_WIDTH), pl.BlockSpec((lora_w, RWKV_WIDTH), cmap2),
                  row(RWKV_WIDTH), pl.BlockSpec((lora_w, RWKV_WIDTH), cmap2),
                  pl.BlockSpec((RWKV_GATE_LORA, RWKV_WIDTH), cmap2),
                  row(RWKV_WIDTH), row(RWKV_WIDTH), row(RWKV_WIDTH), row(RWKV_WIDTH), row(RWKV_WIDTH)],
        out_specs=[pl.BlockSpec((bb, T, RWKV_WIDTH), tmap),
                   pl.BlockSpec((bb, RWKV_HEADS, hd, hd), lambda i, c: (i, 0, 0, 0)),
                   pl.BlockSpec((bb, SUBLANES, RWKV_PROJ), bmap)],
        out_shape=[jax.ShapeDtypeStruct((nb, L, RWKV_WIDTH), _F32),
                   jax.ShapeDtypeStruct(st_rwkv.shape[1:], _F32),
                   jax.ShapeDtypeStruct(shift8.shape[1:], _F32)],
        scratch_shapes=[pltpu.VMEM((bb, T + SUBLANES, RWKV_PROJ), _F32)],
        compiler_params=_params("arbitrary", "arbitrary"),
    )(f, st_rwkv, shift8, mu.reshape(1, -1), w0.reshape(1, -1), w2_pad, a0.reshape(1, -1), a2_pad, g2,
      k_k.reshape(1, -1), k_a.reshape(1, -1), r_k.reshape(1, -1), ln_g.reshape(1, -1), ln_b.reshape(1, -1))


def _s5_kernel(u_ref, sre_ref, sim_ref, lre_ref, lim_ref, dt_ref, bre_ref, bim_ref, cre_ref, cim_ref, d_ref,
               gw_ref, gb_ref, y_ref, hre_ref, him_ref, hre_scr, him_scr, *, nb, tl, final_step):
    @pl.when(pl.program_id(0) == 0)
    def _():
        hre_ref[...] = sre_ref[...]
        him_ref[...] = sim_ref[...]

    lre, lim, dt = lre_ref[...], lim_ref[...], jnp.exp(dt_ref[...])
    mag = jnp.exp(lre * dt)
    ang = lim * dt
    ab_re, ab_im = mag * jnp.cos(ang), mag * jnp.sin(ang)
    den = jnp.square(lre) + jnp.square(lim)
    q_re = ((ab_re - 1.0) * lre + ab_im * lim) / den
    q_im = (ab_im * lre - (ab_re - 1.0) * lim) / den
    b_re, b_im = bre_ref[...], bim_ref[...]
    bb_re = q_re * b_re - q_im * b_im
    bb_im = q_re * b_im + q_im * b_re
    u = u_ref[...]
    hre_scr[0:nb, :] = hre_ref[...]
    him_scr[0:nb, :] = him_ref[...]
    hre_scr[nb:, :] = _dot(u, bb_re)
    him_scr[nb:, :] = _dot(u, bb_im)
    ar = jnp.broadcast_to(ab_re, (nb, S5_FLAT))
    ai = jnp.broadcast_to(ab_im, (nb, S5_FLAT))

    def step(s, carry):
        prev = pl.ds(pl.multiple_of(s * nb, nb), nb)
        cur = pl.ds(pl.multiple_of((s + 1) * nb, nb), nb)
        pr, pi = hre_scr[prev, :], him_scr[prev, :]
        hre_scr[cur, :] = ar * pr - ai * pi + hre_scr[cur, :]
        him_scr[cur, :] = ar * pi + ai * pr + him_scr[cur, :]
        return carry

    lax.fori_loop(0, tl, step, 0)
    hre_ref[...] = hre_scr[(final_step + 1) * nb:(final_step + 2) * nb, :]
    him_ref[...] = him_scr[(final_step + 1) * nb:(final_step + 2) * nb, :]
    y = _dot(hre_scr[nb:, :], cre_ref[...]) - _dot(him_scr[nb:, :], cim_ref[...]) + d_ref[...] * u
    y = 0.5 * y * (1.0 + jnp.tanh(0.7978845608028654 * (y + 0.044715 * (y * y * y))))
    y_ref[...] = y * _sigmoid(_dot(y, gw_ref[...]) + gb_ref[...])


def _s5(u_tm, st_re, st_im, l, a_re, a_im, log_dt, b_re, b_im, c_re, c_im, d, glu_w, glu_b, *, nb, tl,
        final_step):
    rows = u_tm.shape[0]
    L = rows // nb
    eye = jnp.eye(S5_GROUPS, dtype=_F32)
    bd_in = lambda b: jnp.einsum('gpc,gh->gchp', b, eye).reshape(S5_WIDTH, S5_FLAT)
    bd_out = lambda c: jnp.einsum('gcp,gh->gphc', c, eye).reshape(S5_FLAT, S5_WIDTH)
    cmap = lambda t: (0, 0)
    row = lambda n: pl.BlockSpec((1, n), cmap)
    full = lambda a, b: pl.BlockSpec((a, b), cmap)
    state = pl.BlockSpec((None, nb, S5_FLAT), lambda t: (l, 0, 0))
    return pl.pallas_call(
        functools.partial(_s5_kernel, nb=nb, tl=tl, final_step=final_step),
        grid=(L // tl,),
        in_specs=[pl.BlockSpec((tl * nb, S5_WIDTH), lambda t: (t, 0)), state, state,
                  row(S5_FLAT), row(S5_FLAT), row(S5_FLAT),
                  full(S5_WIDTH, S5_FLAT), full(S5_WIDTH, S5_FLAT), full(S5_FLAT, S5_WIDTH), full(S5_FLAT, S5_WIDTH),
                  row(S5_WIDTH), full(S5_WIDTH, S5_WIDTH), row(S5_WIDTH)],
        out_specs=[pl.BlockSpec((tl * nb, S5_WIDTH), lambda t: (t, 0)), full(nb, S5_FLAT), full(nb, S5_FLAT)],
        out_shape=[jax.ShapeDtypeStruct((rows, S5_WIDTH), _F32),
                   jax.ShapeDtypeStruct((nb, S5_FLAT), _F32), jax.ShapeDtypeStruct((nb, S5_FLAT), _F32)],
        scratch_shapes=[pltpu.VMEM(((tl + 1) * nb, S5_FLAT), _F32), pltpu.VMEM(((tl + 1) * nb, S5_FLAT), _F32)],
        compiler_params=_params("arbitrary"),
    )(u_tm, st_re, st_im, a_re.reshape(1, S5_FLAT), a_im.reshape(1, S5_FLAT),
      jnp.repeat(log_dt, S5_STATE).reshape(1, S5_FLAT), bd_in(b_re), bd_in(b_im), bd_out(c_re), bd_out(c_im),
      d.reshape(1, S5_WIDTH), glu_w, glu_b.reshape(1, S5_WIDTH))


def _out_mlp_kernel(x_ref, yssd_ref, yrw_ref, ys5_ref, g1_ref, sh2_ref, sc2_ref, g2_ref, n2g_ref, fg_ref,
                    wo_ref, w1_ref, w2_ref, o_ref, x1_scr, h2_scr, acc_scr, *, final_norm):
    k = pl.program_id(2)
    bb, tl, _ = x_ref.shape
    rows = bb * tl

    @pl.when(k == 0)
    def _():
        flat = lambda ref: ref[...].reshape(rows, ref.shape[-1]).astype(_BF16)
        mix = (_dot(flat(yssd_ref), wo_ref[0:512, :]) + _dot(flat(yrw_ref), wo_ref[512:768, :])
               + _dot(flat(ys5_ref), wo_ref[768:1024, :]))
        x1 = x_ref[...] + g1_ref[...] * mix.reshape(bb, tl, D_MODEL)
        x1_scr[...] = x1
        xn = x1 * lax.rsqrt(jnp.mean(x1 * x1, axis=-1, keepdims=True) + NORM_EPS)
        h2 = xn * n2g_ref[...] * (1.0 + sc2_ref[...]) + sh2_ref[...]
        h2_scr[...] = h2.reshape(rows, D_MODEL).astype(_BF16)
        acc_scr[...] = jnp.zeros_like(acc_scr)

    hid = _dot(h2_scr[...], w1_ref[...])
    hid = jnp.square(jnp.maximum(hid, 0.0)).astype(_BF16)
    acc_scr[...] += _dot(hid, w2_ref[...])

    @pl.when(k == pl.num_programs(2) - 1)
    def _():
        out = x1_scr[...] + g2_ref[...] * acc_scr[...].reshape(bb, tl, D_MODEL)
        if final_norm:
            out = out * lax.rsqrt(jnp.mean(out * out, axis=-1, keepdims=True) + NORM_EPS) * fg_ref[...]
        o_ref[...] = out


def _out_mlp(x, y_ssd, y_rw, y_s5, mod, norm2_g, final_g, w_out_b, w1_b, w2_b, l, *, bb, tl, time_major_s5,
             final_norm):
    nb, L, _ = x.shape
    grid = (nb // bb, L // tl, D_FF // FF_TILE)
    xmap = lambda i, t, k: (i, t, 0)
    cmap = lambda i, t, k: (0, 0)

    def modspec(j):
        return pl.BlockSpec((bb, 1, D_MODEL), lambda i, t, k: (i, 0, j))

    s5_spec = (pl.BlockSpec((tl, S5_WIDTH), lambda i, t, k: (t, i)) if time_major_s5
               else pl.BlockSpec((bb, tl, S5_WIDTH), xmap))
    return pl.pallas_call(
        functools.partial(_out_mlp_kernel, final_norm=final_norm),
        grid=grid,
        in_specs=[pl.BlockSpec((bb, tl, D_MODEL), xmap), pl.BlockSpec((bb, tl, SSD_WIDTH), xmap),
                  pl.BlockSpec((bb, tl, RWKV_WIDTH), xmap), s5_spec,
                  modspec(2), modspec(3), modspec(4), modspec(5),
                  pl.BlockSpec((1, D_MODEL), cmap), pl.BlockSpec((1, D_MODEL), cmap),
                  pl.BlockSpec((None, D_MODEL, D_MODEL), lambda i, t, k: (l, 0, 0)),
                  pl.BlockSpec((None, D_MODEL, FF_TILE), lambda i, t, k: (l, 0, k)),
                  pl.BlockSpec((None, FF_TILE, D_MODEL), lambda i, t, k: (l, k, 0))],
        out_specs=pl.BlockSpec((bb, tl, D_MODEL), xmap),
        out_shape=jax.ShapeDtypeStruct(x.shape, _F32),
        scratch_shapes=[pltpu.VMEM((bb, tl, D_MODEL), _F32), pltpu.VMEM((bb * tl, D_MODEL), _BF16),
                        pltpu.VMEM((bb * tl, D_MODEL), _F32)],
        compiler_params=_params("arbitrary", "arbitrary", "arbitrary"),
    )(x, y_ssd, y_rw, y_s5, mod, mod, mod, mod, norm2_g.reshape(1, D_MODEL), final_g.reshape(1, D_MODEL),
      w_out_b, w1_b, w2_b)


def _trunk(x, mod, states, W, final_g, *, prompt):
    st_ssd, st_conv, st_rwkv, st_shift, st_re, st_im = states
    nb, L, _ = x.shape
    if prompt:
        bb, tl, mlp_tl = 1, min(ROW_TILE, L), min(MLP_ROW_TILE, L)
        ssd_t, rwkv_t, s5_tl = min(SSD_CHUNK, L), min(RWKV_CHUNK, L), min(S5_TILE, L)
        ssd_bb, rwkv_bb = min(nb, SSD_PROMPT_SEQS), min(nb, RWKV_PROMPT_SEQS)
    else:
        bb, tl, mlp_tl = min(nb, ROW_TILE // L), L, L
        ssd_t = rwkv_t = s5_tl = L
        ssd_bb = rwkv_bb = min(nb, SAMPLE_SEQS)
    new = ([], [], [], [], [], [])
    for l in range(DEPTH):
        z, xbc, frw, us5, dt = _in_proj(x, mod[l], W['norm1_g'][l], W['w_in_p'], l, bb=bb, tl=tl,
                                        time_major_s5=prompt)
        v = W['valid']
        y_ssd, n_ssd, n_conv = _ssd(z, xbc, dt, st_ssd, st_conv, l, W['ssd_conv_w'][l], W['ssd_conv_b'][l],
                                    W['ssd_dt_bias'][l], W['ssd_a_log'][l], W['ssd_d'][l], W['ssd_norm_g'][l],
                                    bb=ssd_bb, T=ssd_t, valid=ssd_t if prompt else v)
        y_rw, n_rwkv, n_shift = _rwkv(frw, st_rwkv, st_shift, l, W['rwkv_mu'][l], W['rwkv_w0'][l],
                                      W['rwkv_w2'][l], W['rwkv_a0'][l], W['rwkv_a2'][l], W['rwkv_g2'][l],
                                      W['rwkv_k_k'][l], W['rwkv_k_a'][l], W['rwkv_r_k'][l], W['rwkv_ln_g'][l],
                                      W['rwkv_ln_b'][l], bb=rwkv_bb, T=rwkv_t, valid=rwkv_t if prompt else v)
        if prompt:
            u_tm = us5.reshape(L * nb, S5_WIDTH)
        else:
            u_tm = jnp.transpose(us5, (1, 0, 2)).reshape(L * nb, S5_WIDTH)
        y_s5, n_re, n_im = _s5(u_tm, st_re, st_im, l, W['s5_a_re'][l], W['s5_a_im'][l], W['s5_log_dt'][l],
                               W['s5_b_re'][l], W['s5_b_im'][l], W['s5_c_re'][l], W['s5_c_im'][l], W['s5_d'][l],
                               W['s5_glu_w'][l], W['s5_glu_b'][l], nb=nb, tl=s5_tl,
                               final_step=(s5_tl - 1) if prompt else (v - 1))
        if prompt:
            y_s5 = y_s5.reshape(L, nb * S5_WIDTH)
        else:
            y_s5 = jnp.transpose(y_s5.reshape(L, nb, S5_WIDTH), (1, 0, 2))
        x = _out_mlp(x, y_ssd, y_rw, y_s5, mod[l], W['norm2_g'][l], final_g, W['w_out_b'], W['w1_b'],
                     W['w2_b'], l, bb=bb, tl=mlp_tl, time_major_s5=prompt, final_norm=(l == DEPTH - 1))
        for lst, s in zip(new, (n_ssd, n_conv, n_rwkv, n_shift, n_re, n_im)):
            lst.append(s)
    return x, [jnp.stack(lst) for lst in new]


def _pack_states(st_ssd, st_conv, st_rwkv, st_shift, st_re, st_im):
    d, nb = st_ssd.shape[:2]
    conv8 = jnp.pad(st_conv, ((0, 0), (0, 0), (SUBLANES - (SSD_CONV - 1), 0), (0, 0)))
    shift8 = jnp.pad(st_shift[:, :, None, :], ((0, 0), (0, 0), (SUBLANES - 1, 0), (0, 0)))
    return (st_ssd, conv8, st_rwkv, shift8, st_re.reshape(d, nb, S5_FLAT), st_im.reshape(d, nb, S5_FLAT))


def _unpack_states(n_ssd, n_conv8, n_rwkv, n_shift8, n_re, n_im):
    d, nb = n_ssd.shape[:2]
    return (n_ssd, n_conv8[:, :, SUBLANES - (SSD_CONV - 1):, :], n_rwkv, n_shift8[:, :, SUBLANES - 1, :],
            n_re.reshape(d, nb, S5_GROUPS, S5_STATE), n_im.reshape(d, nb, S5_GROUPS, S5_STATE))


def kernel(x_prompt, x_sample, c_prompt, c_sample, state_ssd, state_ssd_conv, state_rwkv, state_rwkv_shift,
           state_s5_re, state_s5_im, ada_w, ada_b, norm1_g, norm2_g, w_in, ssd_conv_w, ssd_conv_b, ssd_dt_bias,
           ssd_a_log, ssd_d, ssd_norm_g, rwkv_mu, rwkv_w0, rwkv_w2, rwkv_a0, rwkv_a2, rwkv_g2, rwkv_k_k, rwkv_k_a,
           rwkv_r_k, rwkv_ln_g, rwkv_ln_b, s5_a_re, s5_a_im, s5_log_dt, s5_b_re, s5_b_im, s5_c_re, s5_c_im, s5_d,
           s5_glu_w, s5_glu_b, w_out, mlp_w1, mlp_w2, final_g):
    bp, lp, _ = x_prompt.shape
    bs, ls, _ = x_sample.shape
    ls_pad = -(-ls // SUBLANES) * SUBLANES
    w_in_p = jnp.concatenate(
        [w_in[:, :, _Z0:_DT0], w_in[:, :, _FRW0:_IN_COLS], w_in[:, :, _DT0:_FRW0],
         jnp.zeros((DEPTH, D_MODEL, DT_PAD - SSD_HEADS), w_in.dtype)], axis=-1).astype(_BF16)
    W = dict(norm1_g=norm1_g, norm2_g=norm2_g, w_in_p=w_in_p, ssd_conv_w=ssd_conv_w, ssd_conv_b=ssd_conv_b,
             ssd_dt_bias=ssd_dt_bias, ssd_a_log=ssd_a_log, ssd_d=ssd_d, ssd_norm_g=ssd_norm_g, rwkv_mu=rwkv_mu,
             rwkv_w0=rwkv_w0, rwkv_w2=rwkv_w2, rwkv_a0=rwkv_a0, rwkv_a2=rwkv_a2, rwkv_g2=rwkv_g2,
             rwkv_k_k=rwkv_k_k, rwkv_k_a=rwkv_k_a, rwkv_r_k=rwkv_r_k.reshape(DEPTH, RWKV_WIDTH),
             rwkv_ln_g=rwkv_ln_g, rwkv_ln_b=rwkv_ln_b, s5_a_re=s5_a_re, s5_a_im=s5_a_im, s5_log_dt=s5_log_dt,
             s5_b_re=s5_b_re, s5_b_im=s5_b_im, s5_c_re=s5_c_re, s5_c_im=s5_c_im, s5_d=s5_d, s5_glu_w=s5_glu_w,
             s5_glu_b=s5_glu_b, w_out_b=w_out.astype(_BF16), w1_b=mlp_w1.astype(_BF16), w2_b=mlp_w2.astype(_BF16),
             valid=ls)
    mod = _ada(jnp.concatenate([c_prompt, c_sample], axis=0), ada_w, ada_b)
    mod_p = mod[:, :bp].reshape(DEPTH, bp, 1, 6 * D_MODEL)
    mod_s = mod[:, bp:].reshape(DEPTH, bs, 1, 6 * D_MODEL)

    zeros = lambda shape: jnp.zeros((DEPTH, bp) + shape, _F32)
    p_states = _pack_states(zeros((SSD_HEADS, SSD_HEAD_DIM, SSD_STATE)), zeros((SSD_CONV - 1, SSD_CONV_DIM)),
                            zeros((RWKV_HEADS, RWKV_HEAD_DIM, RWKV_HEAD_DIM)), zeros((RWKV_PROJ,)),
                            zeros((S5_GROUPS, S5_STATE)), zeros((S5_GROUPS, S5_STATE)))
    y_prompt, sp = _trunk(x_prompt, mod_p, p_states, W, final_g, prompt=True)

    s_states = _pack_states(state_ssd, state_ssd_conv, state_rwkv, state_rwkv_shift, state_s5_re, state_s5_im)
    x_s = jnp.pad(x_sample, ((0, 0), (0, ls_pad - ls), (0, 0)))
    y_sample, ss = _trunk(x_s, mod_s, s_states, W, final_g, prompt=False)
    return (y_prompt, y_sample[:, :ls]) + _unpack_states(*sp) + _unpack_states(*ss)
```

```python
import functools

import jax
import jax.numpy as jnp
from jax import lax
from jax.experimental import pallas as pl
from jax.experimental.pallas import tpu as pltpu

D_MODEL = 1024
DEPTH = 2
SSD_WIDTH = 512
SSD_HEAD_DIM = 64
SSD_HEADS = 8
SSD_GROUPS = 2
SSD_HEADS_PER_GROUP = SSD_HEADS // SSD_GROUPS
SSD_STATE = 128
SSD_CONV = 4
SSD_CONV_DIM = SSD_WIDTH + 2 * SSD_GROUPS * SSD_STATE
RWKV_WIDTH = 256
RWKV_HEAD_DIM = 64
RWKV_HEADS = 4
RWKV_DECAY_LORA = 64
RWKV_A_LORA = 64
RWKV_GATE_LORA = 128
RWKV_PROJ = 3 * RWKV_WIDTH + RWKV_DECAY_LORA + RWKV_A_LORA + RWKV_GATE_LORA
RWKV_LN_EPS = 64e-5
S5_WIDTH = 256
S5_GROUP_CH = 16
S5_GROUPS = 16
S5_STATE = 64
S5_FLAT = S5_GROUPS * S5_STATE
D_FF = 4 * D_MODEL
NORM_EPS = 1e-6
DT_PAD = 128
SUBLANES = 8

_Z0, _XBC0, _DT0, _FRW0, _US50 = 0, SSD_WIDTH, SSD_WIDTH + SSD_CONV_DIM, SSD_WIDTH + SSD_CONV_DIM + SSD_HEADS, \
    SSD_WIDTH + SSD_CONV_DIM + SSD_HEADS + RWKV_PROJ
_IN_COLS = _US50 + S5_WIDTH
_PZ, _PXBC, _PFRW, _PUS5, _PDT = 0, 512, 1536, 2560, 2816
_P_COLS = _PDT + DT_PAD

SSD_CHUNK = 128
RWKV_CHUNK = 64
SSD_PROMPT_SEQS = 2
RWKV_PROMPT_SEQS = 4
SAMPLE_SEQS = 8
S5_TILE = 256
ROW_TILE = 512
MLP_ROW_TILE = 512
FF_TILE = 1024
VMEM_LIMIT = 56 * 1024 * 1024

_HI = lax.Precision.HIGHEST
P_AMAT = 1
P_INV = 1
P_S0 = 1
P_INTRA = 1
P_SOLVE = 1
P_STATE = 1
_F32 = jnp.float32
_BF16 = jnp.bfloat16


_NN = (((1,), (0,)), ((), ()))
_NT = (((1,), (1,)), ((), ()))
_TN = (((0,), (0,)), ((), ()))


def _split(x):
    hi = x.astype(_BF16)
    return hi, (x - hi.astype(_F32)).astype(_BF16)


def _mm(a, b, dims, passes):
    dg = lambda x, y, prec=None: lax.dot_general(x, y, dims, preferred_element_type=_F32, precision=prec)
    if passes == 6:
        return dg(a.astype(_F32), b.astype(_F32), _HI)
    if passes == 3:
        ah, al = _split(a)
        bh, bl = _split(b)
        return dg(ah, bh) + (dg(ah, bl) + dg(al, bh))
    return dg(a.astype(_BF16), b.astype(_BF16))


def _dot(a, b, passes=1):
    return _mm(a, b, _NN, passes)


def _dot_nt(a, b, passes=1):
    return _mm(a, b, _NT, passes)


def _dot_tn(a, b, passes=1):
    return _mm(a, b, _TN, passes)


def _mm_sel(sel, x, dims, x_is_rhs=True):
    hi = x.astype(_BF16)
    rest = x - hi.astype(_F32)
    mid = rest.astype(_BF16)
    lo = (rest - mid.astype(_F32)).astype(_BF16)
    sel = sel.astype(_BF16)
    if x_is_rhs:
        dg = lambda y: lax.dot_general(sel, y, dims, preferred_element_type=_F32)
    else:
        dg = lambda y: lax.dot_general(y, sel, dims, preferred_element_type=_F32)
    return dg(hi) + (dg(mid) + dg(lo))


def _iota(shape, dim):
    return lax.broadcasted_iota(jnp.int32, shape, dim)


def _sigmoid(x):
    return 1.0 / (1.0 + jnp.exp(-x))


def _softplus(x):
    return jnp.maximum(x, 0.0) + jnp.log1p(jnp.exp(-jnp.abs(x)))


def _silu(x):
    return x * _sigmoid(x)


def _params(*sems):
    return pltpu.CompilerParams(dimension_semantics=sems, vmem_limit_bytes=VMEM_LIMIT)


def _ada_kernel(c_ref, w_ref, b_ref, o_ref):
    c = c_ref[...]
    s = _silu(c).astype(_BF16)
    o_ref[0] = _dot(s, w_ref[0].astype(_BF16)) + b_ref[0]


def _ada(c_all, ada_w, ada_b):
    rows = c_all.shape[0]
    tn = 1536
    return pl.pallas_call(
        _ada_kernel,
        grid=(DEPTH, 6 * D_MODEL // tn),
        in_specs=[pl.BlockSpec((rows, D_MODEL), lambda l, j: (0, 0)),
                  pl.BlockSpec((1, D_MODEL, tn), lambda l, j: (l, 0, j)),
                  pl.BlockSpec((1, 1, tn), lambda l, j: (l, 0, j))],
        out_specs=pl.BlockSpec((1, rows, tn), lambda l, j: (l, 0, j)),
        out_shape=jax.ShapeDtypeStruct((DEPTH, rows, 6 * D_MODEL), _F32),
        compiler_params=_params("arbitrary", "arbitrary"),
    )(c_all, ada_w, ada_b.reshape(DEPTH, 1, 6 * D_MODEL))


def _in_proj_kernel(x_ref, sh_ref, sc_ref, g_ref, w_ref, z_ref, xbc_ref, frw_ref, us5_ref, dt_ref):
    x = x_ref[...]
    bb, tl, _ = x.shape
    xn = x * lax.rsqrt(jnp.mean(x * x, axis=-1, keepdims=True) + NORM_EPS)
    h = xn * g_ref[...] * (1.0 + sc_ref[...]) + sh_ref[...]
    h = h.reshape(bb * tl, D_MODEL).astype(_BF16)
    for o_ref, lo, hi in ((z_ref, _PZ, _PXBC), (xbc_ref, _PXBC, _PFRW), (frw_ref, _PFRW, _PUS5),
                          (us5_ref, _PUS5, _PDT), (dt_ref, _PDT, _P_COLS)):
        o_ref[...] = _dot(h, w_ref[:, lo:hi]).reshape(o_ref.shape)


def _in_proj(x, mod, norm_g, w_in_p, l, *, bb, tl, time_major_s5):
    nb, L, _ = x.shape
    grid = (nb // bb, L // tl)
    xmap = lambda i, t: (i, t, 0)

    def modspec(j):
        return pl.BlockSpec((bb, 1, D_MODEL), lambda i, t: (i, 0, j))

    def out(n):
        return jax.ShapeDtypeStruct((nb, L, n), _F32), pl.BlockSpec((bb, tl, n), xmap)

    outs = [out(SSD_WIDTH), out(SSD_CONV_DIM), out(RWKV_PROJ), out(S5_WIDTH), out(DT_PAD)]
    if time_major_s5:
        assert bb == 1
        outs[3] = (jax.ShapeDtypeStruct((L, nb * S5_WIDTH), _F32), pl.BlockSpec((tl, S5_WIDTH), lambda i, t: (t, i)))
    return pl.pallas_call(
        _in_proj_kernel,
        grid=grid,
        in_specs=[pl.BlockSpec((bb, tl, D_MODEL), xmap), modspec(0), modspec(1),
                  pl.BlockSpec((1, D_MODEL), lambda i, t: (0, 0)),
                  pl.BlockSpec((None, D_MODEL, _P_COLS), lambda i, t: (l, 0, 0))],
        out_specs=[o[1] for o in outs],
        out_shape=[o[0] for o in outs],
        compiler_params=_params("arbitrary", "arbitrary"),
    )(x, mod, mod, norm_g.reshape(1, D_MODEL), w_in_p)


def _ssd_seq(j, z_ref, xbc_ref, dt_ref, cw_ref, cb_ref, dtb_ref, alog_ref, dexp_ref, ng_ref, eexp_ref,
             y_ref, sto_ref, cvo_ref, ext_scr, ybuf_scr, T, valid):
    u = xbc_ref[j]
    ext_scr[j, 0:SUBLANES, :] = cvo_ref[j]
    ext_scr[j, SUBLANES:SUBLANES + T, :] = u
    conv = cb_ref[...] + cw_ref[3:4, :] * u
    for back in (1, 2, 3):
        conv = conv + cw_ref[3 - back:4 - back, :] * ext_scr[j, SUBLANES - back:SUBLANES - back + T, :]
    cvo_ref[j] = ext_scr[j, valid:valid + SUBLANES, :]
    xbc = _silu(conv)
    xs = xbc[:, :SSD_WIDTH]
    row = _iota((T, 1), 0)
    dtv = _softplus(dt_ref[j] + dtb_ref[...])
    if valid < T:
        dtv = jnp.where(row < valid, dtv, 0.0)
    a_row = jnp.where(_iota((1, DT_PAD), 1) < SSD_HEADS, -jnp.exp(alog_ref[...]), 0.0)
    a = dtv * a_row
    yield
    tri_b = _iota((T, T), 0) >= _iota((T, T), 1)
    cs = _mm_sel(tri_b, a, _NN)
    eye = _iota((DT_PAD, DT_PAD), 0) == _iota((DT_PAD, DT_PAD), 1)
    xdt = xs * _mm_sel(eexp_ref[...], dtv, _NN, x_is_rhs=False)
    yield
    cs_t = _mm_sel(eye, cs, _NT)
    ecs = jnp.exp(cs)
    cs_last = cs[T - 1:T, :]
    dte = jnp.exp(cs_last - cs)
    elast = jnp.exp(cs_last)
    for g in range(SSD_GROUPS):
        bm = xbc[:, SSD_WIDTH + g * SSD_STATE:SSD_WIDTH + (g + 1) * SSD_STATE]
        cm = xbc[:, SSD_WIDTH + (SSD_GROUPS + g) * SSD_STATE:SSD_WIDTH + (SSD_GROUPS + g + 1) * SSD_STATE]
        cb = _dot_nt(cm, bm)
        for r in range(SSD_HEADS_PER_GROUP):
            yield
            h = g * SSD_HEADS_PER_GROUP + r
            lo, hi = h * SSD_HEAD_DIM, (h + 1) * SSD_HEAD_DIM
            decay = jnp.exp(jnp.where(tri_b, cs[:, h:h + 1] - cs_t[h:h + 1, :], -jnp.inf))
            xdt_h = xdt[:, lo:hi]
            st = sto_ref[j, h]
            y_h = _dot(cb * decay, xdt_h) + ecs[:, h:h + 1] * _dot_nt(cm, st)
            ybuf_scr[j, :, lo:hi] = y_h
            sto_ref[j, h] = elast[:, h:h + 1] * st + _dot_tn(xdt_h * dte[:, h:h + 1], bm)
    yield
    y = ybuf_scr[j] + xs * dexp_ref[...]
    y = y * _silu(z_ref[j])
    y = y * lax.rsqrt(jnp.mean(y * y, axis=-1, keepdims=True) + NORM_EPS) * ng_ref[...]
    y_ref[j] = y


def _ssd_kernel(z_ref, xbc_ref, dt_ref, st_ref, cv_ref, cw_ref, cb_ref, dtb_ref, alog_ref, dexp_ref, ng_ref,
                eexp_ref, y_ref, sto_ref, cvo_ref, ext_scr, ybuf_scr, *, bb, T, valid):
    @pl.when(pl.program_id(1) == 0)
    def _():
        sto_ref[...] = st_ref[...]
        cvo_ref[...] = cv_ref[...]

    seq = functools.partial(_ssd_seq, z_ref=z_ref, xbc_ref=xbc_ref, dt_ref=dt_ref, cw_ref=cw_ref, cb_ref=cb_ref,
                            dtb_ref=dtb_ref, alog_ref=alog_ref, dexp_ref=dexp_ref, ng_ref=ng_ref, eexp_ref=eexp_ref,
                            y_ref=y_ref, sto_ref=sto_ref, cvo_ref=cvo_ref, ext_scr=ext_scr, ybuf_scr=ybuf_scr,
                            T=T, valid=valid)
    _interleave(seq(j) for j in range(bb))


def _ssd(z, xbc, dt, st_ssd, st_conv8, l, conv_w, conv_b, dt_bias, a_log, d, norm_g, *, bb, T, valid):
    nb, L, _ = z.shape
    grid = (nb // bb, L // T)
    tmap = lambda i, c: (i, c, 0)
    cmap2 = lambda i, c: (0, 0)
    smap = lambda i, c: (i, 0, 0, 0)
    cvmap = lambda i, c: (i, 0, 0)
    pad = DT_PAD - SSD_HEADS
    eexp = jnp.repeat(jnp.eye(DT_PAD, SSD_HEADS, dtype=_F32), SSD_HEAD_DIM, axis=1)
    return pl.pallas_call(
        functools.partial(_ssd_kernel, bb=bb, T=T, valid=valid),
        grid=grid,
        in_specs=[pl.BlockSpec((bb, T, SSD_WIDTH), tmap), pl.BlockSpec((bb, T, SSD_CONV_DIM), tmap),
                  pl.BlockSpec((bb, T, DT_PAD), tmap),
                  pl.BlockSpec((None, bb, SSD_HEADS, SSD_HEAD_DIM, SSD_STATE), lambda i, c: (l, i, 0, 0, 0)),
                  pl.BlockSpec((None, bb, SUBLANES, SSD_CONV_DIM), lambda i, c: (l, i, 0, 0)),
                  pl.BlockSpec((SSD_CONV, SSD_CONV_DIM), cmap2), pl.BlockSpec((1, SSD_CONV_DIM), cmap2),
                  pl.BlockSpec((1, DT_PAD), cmap2), pl.BlockSpec((1, DT_PAD), cmap2),
                  pl.BlockSpec((1, SSD_WIDTH), cmap2), pl.BlockSpec((1, SSD_WIDTH), cmap2),
                  pl.BlockSpec((DT_PAD, SSD_WIDTH), cmap2)],
        out_specs=[pl.BlockSpec((bb, T, SSD_WIDTH), tmap),
                   pl.BlockSpec((bb, SSD_HEADS, SSD_HEAD_DIM, SSD_STATE), smap),
                   pl.BlockSpec((bb, SUBLANES, SSD_CONV_DIM), cvmap)],
        out_shape=[jax.ShapeDtypeStruct((nb, L, SSD_WIDTH), _F32),
                   jax.ShapeDtypeStruct(st_ssd.shape[1:], _F32),
                   jax.ShapeDtypeStruct(st_conv8.shape[1:], _F32)],
        scratch_shapes=[pltpu.VMEM((bb, T + SUBLANES, SSD_CONV_DIM), _F32), pltpu.VMEM((bb, T, SSD_WIDTH), _F32)],
        compiler_params=_params("arbitrary", "arbitrary"),
    )(z, xbc, dt, st_ssd, st_conv8, conv_w, conv_b.reshape(1, -1),
      jnp.pad(dt_bias, (0, pad)).reshape(1, DT_PAD), jnp.pad(a_log, (0, pad)).reshape(1, DT_PAD),
      jnp.repeat(d, SSD_HEAD_DIM).reshape(1, SSD_WIDTH), norm_g.reshape(1, SSD_WIDTH), eexp)


def _stack_heads(x):
    return jnp.concatenate([x[:, h * RWKV_HEAD_DIM:(h + 1) * RWKV_HEAD_DIM] for h in range(RWKV_HEADS)], axis=0)


def _unstack_heads(x, T):
    return jnp.concatenate([x[h * T:(h + 1) * T, :] for h in range(RWKV_HEADS)], axis=1)


def _interleave(gens):
    gens = list(gens)
    while gens:
        alive = []
        for g in gens:
            try:
                next(g)
                alive.append(g)
            except StopIteration:
                pass
        gens = alive


def _rwkv_masks(T):
    HT = RWKV_HEADS * T
    rr = _iota((HT, HT), 0)
    cc = _iota((HT, HT), 1)
    same = (rr // T) == (cc // T)
    rt, ct = _iota((T, T), 0), _iota((T, T), 1)
    head_of_row = _iota((HT, RWKV_WIDTH), 0) // T
    head_of_col = _iota((HT, RWKV_WIDTH), 1) // RWKV_HEAD_DIM
    return dict(strict=jnp.logical_and(same, rr > cc), incl=jnp.logical_and(same, rr >= cc),
                eye=(rr == cc).astype(_F32), hmask=head_of_row == head_of_col,
                strict_t=rt > ct, incl_t=rt >= ct, tri=(rt >= ct).astype(_BF16))


def _rwkv_inverse(masks, n_ab, T):
    inv = masks['eye'] + n_ab
    p = n_ab
    for _ in range(max((T - 1).bit_length() - 1, 0)):
        yield
        p = _dot(p, p, P_INV)
        yield
        inv = inv + _dot(inv, p, P_INV)
    return inv


def _rwkv_core_per_head(j, masks, so_ref, T, at, rt, bt, kt, v4f, w_end):
    cast = lambda x: x.astype(_BF16)
    a4, r4, b4, k4 = (cast(_stack_heads(t)) for t in (at, rt, bt, kt))
    v4 = cast(v4f)
    bw4, kw4 = (cast(_stack_heads(t * w_end)) for t in (bt, kt))
    heads = range(RWKV_HEADS)
    blk = lambda x, h: x[h * T:(h + 1) * T]
    yield
    n_ab = jnp.where(masks['strict'], _dot_nt(a4, b4, P_AMAT), 0.0)
    n_ak = [cast(jnp.where(masks['strict_t'], _dot_nt(blk(a4, h), blk(k4, h), P_AMAT), 0.0)) for h in heads]
    m_rb = [cast(jnp.where(masks['incl_t'], _dot_nt(blk(r4, h), blk(b4, h), P_AMAT), 0.0)) for h in heads]
    m_rk = [cast(jnp.where(masks['incl_t'], _dot_nt(blk(r4, h), blk(k4, h), P_AMAT), 0.0)) for h in heads]
    inv = yield from _rwkv_inverse(masks, n_ab, T)
    yield
    s0 = [so_ref[j, h] for h in heads]
    rhs = jnp.concatenate([_dot_nt(blk(a4, h), s0[h], P_S0) + _dot(n_ak[h], blk(v4, h), P_INTRA)
                           for h in heads], axis=0)
    o_own = [_dot_nt(blk(r4, h), s0[h], P_S0) + _dot(m_rk[h], blk(v4, h), P_INTRA) for h in heads]
    yield
    u4 = cast(_dot(inv, rhs, P_SOLVE))
    yield
    o4 = jnp.concatenate([o_own[h] + _dot(m_rb[h], blk(u4, h), P_INTRA) for h in heads], axis=0)
    for h in heads:
        so_ref[j, h] = (s0[h] * w_end[:, h * RWKV_HEAD_DIM:(h + 1) * RWKV_HEAD_DIM]
                        + _dot_tn(blk(u4, h), blk(bw4, h), P_STATE) + _dot_tn(blk(v4, h), blk(kw4, h), P_STATE))
    yield
    return o4


def _rwkv_core_stacked(j, masks, so_ref, T, at, rt, bt, kt, v4f, w_end):
    strict, incl, hmask = masks['strict'], masks['incl'], masks['hmask']
    a4, r4, b4, k4 = (_stack_heads(t) for t in (at, rt, bt, kt))
    yield
    n_ab = jnp.where(strict, _dot_nt(a4, b4, P_AMAT), 0.0)
    n_ak = jnp.where(strict, _dot_nt(a4, k4, P_AMAT), 0.0)
    m_rb = jnp.where(incl, _dot_nt(r4, b4, P_AMAT), 0.0)
    m_rk = jnp.where(incl, _dot_nt(r4, k4, P_AMAT), 0.0)
    inv = yield from _rwkv_inverse(masks, n_ab, T)

    def wide(x):
        return jnp.where(hmask, jnp.concatenate([x] * RWKV_HEADS, axis=0), 0.0)

    yield
    s0 = jnp.concatenate([so_ref[j, h] for h in range(RWKV_HEADS)], axis=1)
    rhs = _dot_nt(wide(at), s0, P_S0) + _dot(n_ak, v4f, P_INTRA)
    o4 = _dot_nt(wide(rt), s0, P_S0) + _dot(m_rk, v4f, P_INTRA)
    yield
    u4 = _dot(inv, rhs, P_SOLVE)
    yield
    o4 = o4 + _dot(m_rb, u4, P_INTRA)
    s1 = (s0 * w_end + _dot_tn(u4, wide(bt * w_end), P_STATE) + _dot_tn(v4f, wide(kt * w_end), P_STATE))
    for h in range(RWKV_HEADS):
        so_ref[j, h] = s1[:, h * RWKV_HEAD_DIM:(h + 1) * RWKV_HEAD_DIM]
    yield
    return o4


def _rwkv_seq(j, masks, f_ref, mu_ref, w0_ref, w2_ref, a0_ref, a2_ref, g2_ref, kk_ref, ka_ref, rk_ref, lng_ref,
              lnb_ref, y_ref, so_ref, sho_ref, ext_scr, T, valid):
    f = f_ref[j]
    ext_scr[j, 0:SUBLANES, :] = sho_ref[j]
    ext_scr[j, SUBLANES:SUBLANES + T, :] = f
    prev = ext_scr[j, SUBLANES - 1:SUBLANES - 1 + T, :]
    sho_ref[j] = ext_scr[j, valid:valid + SUBLANES, :]
    fm = f + (prev - f) * mu_ref[...]
    r = fm[:, 0:256]
    k = fm[:, 256:512]
    v = fm[:, 512:768]
    lora = fm[:, 768:896]
    gl = fm[:, 896:1024]
    w_log = -_softplus(-(w0_ref[...] + _dot(jnp.tanh(lora), w2_ref[...]))) - 0.5
    logw = -jnp.exp(w_log)
    a = _sigmoid(a0_ref[...] + _dot(lora, a2_ref[...]))
    g = _dot(_sigmoid(gl), g2_ref[...])
    k2 = k * (1.0 + (a - 1.0) * ka_ref[...])
    kk4 = _stack_heads(k * kk_ref[...])
    kk4 = kk4 * lax.rsqrt(jnp.maximum(jnp.sum(kk4 * kk4, axis=-1, keepdims=True), 1e-24))
    kkn = _unstack_heads(kk4, T)
    if valid < T:
        live = _iota((T, 1), 0) < valid
        logw = jnp.where(live, logw, 0.0)
        kkn = jnp.where(live, kkn, 0.0)
        k2 = jnp.where(live, k2, 0.0)
    yield
    cw = _mm_sel(masks['tri'], logw, _NN)
    w_inc = jnp.exp(cw)
    w_exc = jnp.exp(cw - logw)
    w_inv = jnp.exp(-cw)
    w_end = w_inc[T - 1:T, :]
    at = -kkn * w_exc
    rt = r * w_inc
    bt = kkn * a * w_inv
    kt = k2 * w_inv
    v4f = _stack_heads(v)
    core = _rwkv_core_per_head if T % 16 == 0 else _rwkv_core_stacked
    o4 = yield from core(j, masks, so_ref, T, at, rt, bt, kt, v4f, w_end)

    mean = jnp.mean(o4, axis=-1, keepdims=True)
    var = jnp.mean(jnp.square(o4 - mean), axis=-1, keepdims=True)
    o4 = (o4 - mean) * lax.rsqrt(var + RWKV_LN_EPS)
    bonus4 = jnp.sum(_stack_heads(r * k2 * rk_ref[...]), axis=-1, keepdims=True) * v4f
    o = _unstack_heads(o4, T) * lng_ref[...] + lnb_ref[...] + _unstack_heads(bonus4, T)
    y_ref[j] = o * g


def _rwkv_kernel(f_ref, s_ref, sh_ref, mu_ref, w0_ref, w2_ref, a0_ref, a2_ref, g2_ref, kk_ref, ka_ref, rk_ref,
                 lng_ref, lnb_ref, y_ref, so_ref, sho_ref, ext_scr, *, bb, T, valid):
    @pl.when(pl.program_id(1) == 0)
    def _():
        so_ref[...] = s_ref[...]
        sho_ref[...] = sh_ref[...]

    seq = functools.partial(_rwkv_seq, f_ref=f_ref, mu_ref=mu_ref, w0_ref=w0_ref, w2_ref=w2_ref, a0_ref=a0_ref,
                            a2_ref=a2_ref, g2_ref=g2_ref, kk_ref=kk_ref, ka_ref=ka_ref, rk_ref=rk_ref,
                            lng_ref=lng_ref, lnb_ref=lnb_ref, y_ref=y_ref, so_ref=so_ref, sho_ref=sho_ref,
                            ext_scr=ext_scr, T=T, valid=valid)
    masks = _rwkv_masks(T)
    _interleave(seq(j, masks) for j in range(bb))


def _rwkv(f, st_rwkv, shift8, l, mu, w0, w2, a0, a2, g2, k_k, k_a, r_k, ln_g, ln_b, *, bb, T, valid):
    nb, L, _ = f.shape
    hd = RWKV_HEAD_DIM
    grid = (nb // bb, L // T)
    tmap = lambda i, c: (i, c, 0)
    bmap = lambda i, c: (i, 0, 0)
    cmap2 = lambda i, c: (0, 0)
    row = lambda n: pl.BlockSpec((1, n), cmap2)
    lora_w = RWKV_DECAY_LORA + RWKV_A_LORA
    w2_pad = jnp.concatenate([w2, jnp.zeros((RWKV_A_LORA, RWKV_WIDTH), _F32)], axis=0)
    a2_pad = jnp.concatenate([jnp.zeros((RWKV_DECAY_LORA, RWKV_WIDTH), _F32), a2], axis=0)
    return pl.pallas_call(
        functools.partial(_rwkv_kernel, bb=bb, T=T, valid=valid),
        grid=grid,
        in_specs=[pl.BlockSpec((bb, T, RWKV_PROJ), tmap),
                  pl.BlockSpec((None, bb, RWKV_HEADS, hd, hd), lambda i, c: (l, i, 0, 0, 0)),
                  pl.BlockSpec((None, bb, SUBLANES, RWKV_PROJ), lambda i, c: (l, i, 0, 0)),
                  row(RWKV_PROJ), row(RWKV_WIDTH), pl.BlockSpec((lora_w, RWKV_WIDTH), cmap2),
                  row(RWKV_WIDTH), pl.BlockSpec((lora_w, RWKV_WIDTH), cmap2),
                  pl.BlockSpec((RWKV_GATE_LORA, RWKV_WIDTH), cmap2),
                  row(RWKV_WIDTH), row(RWKV_WIDTH), row(RWKV_WIDTH), row(RWKV_WIDTH), row(RWKV_WIDTH)],
        out_specs=[pl.BlockSpec((bb, T, RWKV_WIDTH), tmap),
                   pl.BlockSpec((bb, RWKV_HEADS, hd, hd), lambda i, c: (i, 0, 0, 0)),
                   pl.BlockSpec((bb, SUBLANES, RWKV_PROJ), bmap)],
        out_shape=[jax.ShapeDtypeStruct((nb, L, RWKV_WIDTH), _F32),
                   jax.ShapeDtypeStruct(st_rwkv.shape[1:], _F32),
                   jax.ShapeDtypeStruct(shift8.shape[1:], _F32)],
        scratch_shapes=[pltpu.VMEM((bb, T + SUBLANES, RWKV_PROJ), _F32)],
        compiler_params=_params("arbitrary", "arbitrary"),
    )(f, st_rwkv, shift8, mu.reshape(1, -1), w0.reshape(1, -1), w2_pad, a0.reshape(1, -1), a2_pad, g2,
      k_k.reshape(1, -1), k_a.reshape(1, -1), r_k.reshape(1, -1), ln_g.reshape(1, -1), ln_b.reshape(1, -1))


def _s5_kernel(u_ref, sre_ref, sim_ref, lre_ref, lim_ref, dt_ref, bre_ref, bim_ref, cre_ref, cim_ref, d_ref,
               gw_ref, gb_ref, y_ref, hre_ref, him_ref, hre_scr, him_scr, *, nb, tl, final_step):
    @pl.when(pl.program_id(0) == 0)
    def _():
        hre_ref[...] = sre_ref[...]
        him_ref[...] = sim_ref[...]

    lre, lim, dt = lre_ref[...], lim_ref[...], jnp.exp(dt_ref[...])
    mag = jnp.exp(lre * dt)
    ang = lim * dt
    ab_re, ab_im = mag * jnp.cos(ang), mag * jnp.sin(ang)
    den = jnp.square(lre) + jnp.square(lim)
    q_re = ((ab_re - 1.0) * lre + ab_im * lim) / den
    q_im = (ab_im * lre - (ab_re - 1.0) * lim) / den
    b_re, b_im = bre_ref[...], bim_ref[...]
    bb_re = q_re * b_re - q_im * b_im
    bb_im = q_re * b_im + q_im * b_re
    u = u_ref[...]
    hre_scr[0:nb, :] = hre_ref[...]
    him_scr[0:nb, :] = him_ref[...]
    hre_scr[nb:, :] = _dot(u, bb_re)
    him_scr[nb:, :] = _dot(u, bb_im)
    ar = jnp.broadcast_to(ab_re, (nb, S5_FLAT))
    ai = jnp.broadcast_to(ab_im, (nb, S5_FLAT))

    def step(s, carry):
        prev = pl.ds(pl.multiple_of(s * nb, nb), nb)
        cur = pl.ds(pl.multiple_of((s + 1) * nb, nb), nb)
        pr, pi = hre_scr[prev, :], him_scr[prev, :]
        hre_scr[cur, :] = ar * pr - ai * pi + hre_scr[cur, :]
        him_scr[cur, :] = ar * pi + ai * pr + him_scr[cur, :]
        return carry

    lax.fori_loop(0, tl, step, 0)
    hre_ref[...] = hre_scr[(final_step + 1) * nb:(final_step + 2) * nb, :]
    him_ref[...] = him_scr[(final_step + 1) * nb:(final_step + 2) * nb, :]
    y = _dot(hre_scr[nb:, :], cre_ref[...]) - _dot(him_scr[nb:, :], cim_ref[...]) + d_ref[...] * u
    y = 0.5 * y * (1.0 + jnp.tanh(0.7978845608028654 * (y + 0.044715 * (y * y * y))))
    y_ref[...] = y * _sigmoid(_dot(y, gw_ref[...]) + gb_ref[...])


def _s5(u_tm, st_re, st_im, l, a_re, a_im, log_dt, b_re, b_im, c_re, c_im, d, glu_w, glu_b, *, nb, tl,
        final_step):
    rows = u_tm.shape[0]
    L = rows // nb
    eye = jnp.eye(S5_GROUPS, dtype=_F32)
    bd_in = lambda b: jnp.einsum('gpc,gh->gchp', b, eye).reshape(S5_WIDTH, S5_FLAT)
    bd_out = lambda c: jnp.einsum('gcp,gh->gphc', c, eye).reshape(S5_FLAT, S5_WIDTH)
    cmap = lambda t: (0, 0)
    row = lambda n: pl.BlockSpec((1, n), cmap)
    full = lambda a, b: pl.BlockSpec((a, b), cmap)
    state = pl.BlockSpec((None, nb, S5_FLAT), lambda t: (l, 0, 0))
    return pl.pallas_call(
        functools.partial(_s5_kernel, nb=nb, tl=tl, final_step=final_step),
        grid=(L // tl,),
        in_specs=[pl.BlockSpec((tl * nb, S5_WIDTH), lambda t: (t, 0)), state, state,
                  row(S5_FLAT), row(S5_FLAT), row(S5_FLAT),
                  full(S5_WIDTH, S5_FLAT), full(S5_WIDTH, S5_FLAT), full(S5_FLAT, S5_WIDTH), full(S5_FLAT, S5_WIDTH),
                  row(S5_WIDTH), full(S5_WIDTH, S5_WIDTH), row(S5_WIDTH)],
        out_specs=[pl.BlockSpec((tl * nb, S5_WIDTH), lambda t: (t, 0)), full(nb, S5_FLAT), full(nb, S5_FLAT)],
        out_shape=[jax.ShapeDtypeStruct((rows, S5_WIDTH), _F32),
                   jax.ShapeDtypeStruct((nb, S5_FLAT), _F32), jax.ShapeDtypeStruct((nb, S5_FLAT), _F32)],
        scratch_shapes=[pltpu.VMEM(((tl + 1) * nb, S5_FLAT), _F32), pltpu.VMEM(((tl + 1) * nb, S5_FLAT), _F32)],
        compiler_params=_params("arbitrary"),
    )(u_tm, st_re, st_im, a_re.reshape(1, S5_FLAT), a_im.reshape(1, S5_FLAT),
      jnp.repeat(log_dt, S5_STATE).reshape(1, S5_FLAT), bd_in(b_re), bd_in(b_im), bd_out(c_re), bd_out(c_im),
      d.reshape(1, S5_WIDTH), glu_w, glu_b.reshape(1, S5_WIDTH))


def _out_mlp_kernel(x_ref, yssd_ref, yrw_ref, ys5_ref, g1_ref, sh2_ref, sc2_ref, g2_ref, n2g_ref, fg_ref,
                    wo_ref, w1_ref, w2_ref, o_ref, hid_scr, *, final_norm):
    bb, tl, _ = x_ref.shape
    rows = bb * tl
    flat = lambda ref: ref[...].reshape(rows, ref.shape[-1]).astype(_BF16)
    mix = (_dot(flat(yssd_ref), wo_ref[0:512, :]) + _dot(flat(yrw_ref), wo_ref[512:768, :])
           + _dot(flat(ys5_ref), wo_ref[768:1024, :]))
    x1 = x_ref[...] + g1_ref[...] * mix.reshape(bb, tl, D_MODEL)
    xn = x1 * lax.rsqrt(jnp.mean(x1 * x1, axis=-1, keepdims=True) + NORM_EPS)
    h2 = (xn * n2g_ref[...] * (1.0 + sc2_ref[...]) + sh2_ref[...]).reshape(rows, D_MODEL).astype(_BF16)
    for c in range(D_FF // FF_TILE):
        cols = slice(c * FF_TILE, (c + 1) * FF_TILE)
        hid_scr[:, cols] = jnp.square(jnp.maximum(_dot(h2, w1_ref[:, cols]), 0.0)).astype(_BF16)
    ff = _dot(hid_scr[...], w2_ref[...])
    out = x1 + g2_ref[...] * ff.reshape(bb, tl, D_MODEL)
    if final_norm:
        out = out * lax.rsqrt(jnp.mean(out * out, axis=-1, keepdims=True) + NORM_EPS) * fg_ref[...]
    o_ref[...] = out


def _out_mlp(x, y_ssd, y_rw, y_s5, mod, norm2_g, final_g, w_out_b, w1_b, w2_b, l, *, bb, tl, time_major_s5,
             final_norm):
    nb, L, _ = x.shape
    grid = (nb // bb, L // tl)
    xmap = lambda i, t: (i, t, 0)
    cmap = lambda i, t: (0, 0)
    wmap = lambda i, t: (l, 0, 0)
    once = pl.Buffered(1)

    def modspec(j):
        return pl.BlockSpec((bb, 1, D_MODEL), lambda i, t: (i, 0, j))

    s5_spec = (pl.BlockSpec((tl, S5_WIDTH), lambda i, t: (t, i)) if time_major_s5
               else pl.BlockSpec((bb, tl, S5_WIDTH), xmap))
    return pl.pallas_call(
        functools.partial(_out_mlp_kernel, final_norm=final_norm),
        grid=grid,
        in_specs=[pl.BlockSpec((bb, tl, D_MODEL), xmap), pl.BlockSpec((bb, tl, SSD_WIDTH), xmap),
                  pl.BlockSpec((bb, tl, RWKV_WIDTH), xmap), s5_spec,
                  modspec(2), modspec(3), modspec(4), modspec(5),
                  pl.BlockSpec((1, D_MODEL), cmap), pl.BlockSpec((1, D_MODEL), cmap),
                  pl.BlockSpec((None, D_MODEL, D_MODEL), wmap, pipeline_mode=once),
                  pl.BlockSpec((None, D_MODEL, D_FF), wmap, pipeline_mode=once),
                  pl.BlockSpec((None, D_FF, D_MODEL), wmap, pipeline_mode=once)],
        out_specs=pl.BlockSpec((bb, tl, D_MODEL), xmap),
        out_shape=jax.ShapeDtypeStruct(x.shape, _F32),
        scratch_shapes=[pltpu.VMEM((bb * tl, D_FF), _BF16)],
        compiler_params=_params("arbitrary", "arbitrary"),
    )(x, y_ssd, y_rw, y_s5, mod, mod, mod, mod, norm2_g.reshape(1, D_MODEL), final_g.reshape(1, D_MODEL),
      w_out_b, w1_b, w2_b)


def _trunk(x, mod, states, W, final_g, *, prompt):
    st_ssd, st_conv, st_rwkv, st_shift, st_re, st_im = states
    nb, L, _ = x.shape
    if prompt:
        bb, tl, mlp_tl = 1, min(ROW_TILE, L), min(MLP_ROW_TILE, L)
        ssd_t, rwkv_t, s5_tl = min(SSD_CHUNK, L), min(RWKV_CHUNK, L), min(S5_TILE, L)
        ssd_bb, rwkv_bb = min(nb, SSD_PROMPT_SEQS), min(nb, RWKV_PROMPT_SEQS)
    else:
        bb, tl, mlp_tl = min(nb, ROW_TILE // L), L, L
        ssd_t = rwkv_t = s5_tl = L
        ssd_bb = rwkv_bb = min(nb, SAMPLE_SEQS)
    new = ([], [], [], [], [], [])
    for l in range(DEPTH):
        z, xbc, frw, us5, dt = _in_proj(x, mod[l], W['norm1_g'][l], W['w_in_p'], l, bb=bb, tl=tl,
                                        time_major_s5=prompt)
        v = W['valid']
        y_ssd, n_ssd, n_conv = _ssd(z, xbc, dt, st_ssd, st_conv, l, W['ssd_conv_w'][l], W['ssd_conv_b'][l],
                                    W['ssd_dt_bias'][l], W['ssd_a_log'][l], W['ssd_d'][l], W['ssd_norm_g'][l],
                                    bb=ssd_bb, T=ssd_t, valid=ssd_t if prompt else v)
        y_rw, n_rwkv, n_shift = _rwkv(frw, st_rwkv, st_shift, l, W['rwkv_mu'][l], W['rwkv_w0'][l],
                                      W['rwkv_w2'][l], W['rwkv_a0'][l], W['rwkv_a2'][l], W['rwkv_g2'][l],
                                      W['rwkv_k_k'][l], W['rwkv_k_a'][l], W['rwkv_r_k'][l], W['rwkv_ln_g'][l],
                                      W['rwkv_ln_b'][l], bb=rwkv_bb, T=rwkv_t, valid=rwkv_t if prompt else v)
        if prompt:
            u_tm = us5.reshape(L * nb, S5_WIDTH)
        else:
            u_tm = jnp.transpose(us5, (1, 0, 2)).reshape(L * nb, S5_WIDTH)
        y_s5, n_re, n_im = _s5(u_tm, st_re, st_im, l, W['s5_a_re'][l], W['s5_a_im'][l], W['s5_log_dt'][l],
                               W['s5_b_re'][l], W['s5_b_im'][l], W['s5_c_re'][l], W['s5_c_im'][l], W['s5_d'][l],
                               W['s5_glu_w'][l], W['s5_glu_b'][l], nb=nb, tl=s5_tl,
                               final_step=(s5_tl - 1) if prompt else (v - 1))
        if prompt:
            y_s5 = y_s5.reshape(L, nb * S5_WIDTH)
        else:
            y_s5 = jnp.transpose(y_s5.reshape(L, nb, S5_WIDTH), (1, 0, 2))
        x = _out_mlp(x, y_ssd, y_rw, y_s5, mod[l], W['norm2_g'][l], final_g, W['w_out_b'], W['w1_b'],
                     W['w2_b'], l, bb=bb, tl=mlp_tl, time_major_s5=prompt, final_norm=(l == DEPTH - 1))
        for lst, s in zip(new, (n_ssd, n_conv, n_rwkv, n_shift, n_re, n_im)):
            lst.append(s)
    return x, [jnp.stack(lst) for lst in new]


def _pack_states(st_ssd, st_conv, st_rwkv, st_shift, st_re, st_im):
    d, nb = st_ssd.shape[:2]
    conv8 = jnp.pad(st_conv, ((0, 0), (0, 0), (SUBLANES - (SSD_CONV - 1), 0), (0, 0)))
    shift8 = jnp.pad(st_shift[:, :, None, :], ((0, 0), (0, 0), (SUBLANES - 1, 0), (0, 0)))
    return (st_ssd, conv8, st_rwkv, shift8, st_re.reshape(d, nb, S5_FLAT), st_im.reshape(d, nb, S5_FLAT))


def _unpack_states(n_ssd, n_conv8, n_rwkv, n_shift8, n_re, n_im):
    d, nb = n_ssd.shape[:2]
    return (n_ssd, n_conv8[:, :, SUBLANES - (SSD_CONV - 1):, :], n_rwkv, n_shift8[:, :, SUBLANES - 1, :],
            n_re.reshape(d, nb, S5_GROUPS, S5_STATE), n_im.reshape(d, nb, S5_GROUPS, S5_STATE))


def kernel(x_prompt, x_sample, c_prompt, c_sample, state_ssd, state_ssd_conv, state_rwkv, state_rwkv_shift,
           state_s5_re, state_s5_im, ada_w, ada_b, norm1_g, norm2_g, w_in, ssd_conv_w, ssd_conv_b, ssd_dt_bias,
           ssd_a_log, ssd_d, ssd_norm_g, rwkv_mu, rwkv_w0, rwkv_w2, rwkv_a0, rwkv_a2, rwkv_g2, rwkv_k_k, rwkv_k_a,
           rwkv_r_k, rwkv_ln_g, rwkv_ln_b, s5_a_re, s5_a_im, s5_log_dt, s5_b_re, s5_b_im, s5_c_re, s5_c_im, s5_d,
           s5_glu_w, s5_glu_b, w_out, mlp_w1, mlp_w2, final_g):
    bp, lp, _ = x_prompt.shape
    bs, ls, _ = x_sample.shape
    ls_pad = -(-ls // SUBLANES) * SUBLANES
    w_in_p = jnp.concatenate(
        [w_in[:, :, _Z0:_DT0], w_in[:, :, _FRW0:_IN_COLS], w_in[:, :, _DT0:_FRW0],
         jnp.zeros((DEPTH, D_MODEL, DT_PAD - SSD_HEADS), w_in.dtype)], axis=-1).astype(_BF16)
    W = dict(norm1_g=norm1_g, norm2_g=norm2_g, w_in_p=w_in_p, ssd_conv_w=ssd_conv_w, ssd_conv_b=ssd_conv_b,
             ssd_dt_bias=ssd_dt_bias, ssd_a_log=ssd_a_log, ssd_d=ssd_d, ssd_norm_g=ssd_norm_g, rwkv_mu=rwkv_mu,
             rwkv_w0=rwkv_w0, rwkv_w2=rwkv_w2, rwkv_a0=rwkv_a0, rwkv_a2=rwkv_a2, rwkv_g2=rwkv_g2,
             rwkv_k_k=rwkv_k_k, rwkv_k_a=rwkv_k_a, rwkv_r_k=rwkv_r_k.reshape(DEPTH, RWKV_WIDTH),
             rwkv_ln_g=rwkv_ln_g, rwkv_ln_b=rwkv_ln_b, s5_a_re=s5_a_re, s5_a_im=s5_a_im, s5_log_dt=s5_log_dt,
             s5_b_re=s5_b_re, s5_b_im=s5_b_im, s5_c_re=s5_c_re, s5_c_im=s5_c_im, s5_d=s5_d, s5_glu_w=s5_glu_w,
             s5_glu_b=s5_glu_b, w_out_b=w_out.astype(_BF16), w1_b=mlp_w1.astype(_BF16), w2_b=mlp_w2.astype(_BF16),
             valid=ls)
    mod = _ada(jnp.concatenate([c_prompt, c_sample], axis=0), ada_w, ada_b)
    mod_p = mod[:, :bp].reshape(DEPTH, bp, 1, 6 * D_MODEL)
    mod_s = mod[:, bp:].reshape(DEPTH, bs, 1, 6 * D_MODEL)

    zeros = lambda shape: jnp.zeros((DEPTH, bp) + shape, _F32)
    p_states = _pack_states(zeros((SSD_HEADS, SSD_HEAD_DIM, SSD_STATE)), zeros((SSD_CONV - 1, SSD_CONV_DIM)),
                            zeros((RWKV_HEADS, RWKV_HEAD_DIM, RWKV_HEAD_DIM)), zeros((RWKV_PROJ,)),
                            zeros((S5_GROUPS, S5_STATE)), zeros((S5_GROUPS, S5_STATE)))
    y_prompt, sp = _trunk(x_prompt, mod_p, p_states, W, final_g, prompt=True)

    s_states = _pack_states(state_ssd, state_ssd_conv, state_rwkv, state_rwkv_shift, state_s5_re, state_s5_im)
    x_s = jnp.pad(x_sample, ((0, 0), (0, ls_pad - ls), (0, 0)))
    y_sample, ss = _trunk(x_s, mod_s, s_states, W, final_g, prompt=False)
    return (y_prompt, y_sample[:, :ls]) + _unpack_states(*sp) + _unpack_states(*ss)
```

```python
import functools

import jax
import jax.numpy as jnp
from jax import lax
from jax.experimental import pallas as pl
from jax.experimental.pallas import tpu as pltpu

D_MODEL = 1024
DEPTH = 2
SSD_WIDTH = 512
SSD_HEAD_DIM = 64
SSD_HEADS = 8
SSD_GROUPS = 2
SSD_HEADS_PER_GROUP = SSD_HEADS // SSD_GROUPS
SSD_STATE = 128
SSD_CONV = 4
SSD_CONV_DIM = SSD_WIDTH + 2 * SSD_GROUPS * SSD_STATE
RWKV_WIDTH = 256
RWKV_HEAD_DIM = 64
RWKV_HEADS = 4
RWKV_DECAY_LORA = 64
RWKV_A_LORA = 64
RWKV_GATE_LORA = 128
RWKV_PROJ = 3 * RWKV_WIDTH + RWKV_DECAY_LORA + RWKV_A_LORA + RWKV_GATE_LORA
RWKV_LN_EPS = 64e-5
S5_WIDTH = 256
S5_GROUP_CH = 16
S5_GROUPS = 16
S5_STATE = 64
S5_FLAT = S5_GROUPS * S5_STATE
D_FF = 4 * D_MODEL
NORM_EPS = 1e-6
DT_PAD = 128
SUBLANES = 8

_Z0, _XBC0, _DT0, _FRW0, _US50 = 0, SSD_WIDTH, SSD_WIDTH + SSD_CONV_DIM, SSD_WIDTH + SSD_CONV_DIM + SSD_HEADS, \
    SSD_WIDTH + SSD_CONV_DIM + SSD_HEADS + RWKV_PROJ
_IN_COLS = _US50 + S5_WIDTH
_PZ, _PXBC, _PFRW, _PUS5, _PDT = 0, 512, 1536, 2560, 2816
_P_COLS = _PDT + DT_PAD

SSD_CHUNK = 128
RWKV_CHUNK = 64
SSD_PROMPT_SEQS = 2
RWKV_PROMPT_SEQS = 4
SAMPLE_SEQS = 8
S5_TILE = 256
ROW_TILE = 512
MLP_ROW_TILE = 512
FF_TILE = 1024
VMEM_LIMIT = 56 * 1024 * 1024

_HI = lax.Precision.HIGHEST
P_AMAT = 1
P_INV = 1
P_S0 = 1
P_INTRA = 1
P_SOLVE = 1
P_STATE = 1
_F32 = jnp.float32
_BF16 = jnp.bfloat16


_NN = (((1,), (0,)), ((), ()))
_NT = (((1,), (1,)), ((), ()))
_TN = (((0,), (0,)), ((), ()))


def _split(x):
    hi = x.astype(_BF16)
    return hi, (x - hi.astype(_F32)).astype(_BF16)


def _mm(a, b, dims, passes):
    dg = lambda x, y, prec=None: lax.dot_general(x, y, dims, preferred_element_type=_F32, precision=prec)
    if passes == 6:
        return dg(a.astype(_F32), b.astype(_F32), _HI)
    if passes == 3:
        ah, al = _split(a)
        bh, bl = _split(b)
        return dg(ah, bh) + (dg(ah, bl) + dg(al, bh))
    return dg(a.astype(_BF16), b.astype(_BF16))


def _dot(a, b, passes=1):
    return _mm(a, b, _NN, passes)


def _dot_nt(a, b, passes=1):
    return _mm(a, b, _NT, passes)


def _dot_tn(a, b, passes=1):
    return _mm(a, b, _TN, passes)


def _mm_sel(sel, x, dims, x_is_rhs=True):
    hi = x.astype(_BF16)
    rest = x - hi.astype(_F32)
    mid = rest.astype(_BF16)
    lo = (rest - mid.astype(_F32)).astype(_BF16)
    sel = sel.astype(_BF16)
    if x_is_rhs:
        dg = lambda y: lax.dot_general(sel, y, dims, preferred_element_type=_F32)
    else:
        dg = lambda y: lax.dot_general(y, sel, dims, preferred_element_type=_F32)
    return dg(hi) + (dg(mid) + dg(lo))


def _iota(shape, dim):
    return lax.broadcasted_iota(jnp.int32, shape, dim)


def _sigmoid(x):
    return 1.0 / (1.0 + jnp.exp(-x))


def _softplus(x):
    return jnp.maximum(x, 0.0) + jnp.log1p(jnp.exp(-jnp.abs(x)))


def _silu(x):
    return x * _sigmoid(x)


def _params(*sems):
    return pltpu.CompilerParams(dimension_semantics=sems, vmem_limit_bytes=VMEM_LIMIT)


def _ada_kernel(c_ref, w_ref, b_ref, o_ref):
    c = c_ref[...]
    s = _silu(c).astype(_BF16)
    o_ref[0] = _dot(s, w_ref[0].astype(_BF16)) + b_ref[0]


def _ada(c_all, ada_w, ada_b):
    rows = c_all.shape[0]
    tn = 1536
    return pl.pallas_call(
        _ada_kernel,
        grid=(DEPTH, 6 * D_MODEL // tn),
        in_specs=[pl.BlockSpec((rows, D_MODEL), lambda l, j: (0, 0)),
                  pl.BlockSpec((1, D_MODEL, tn), lambda l, j: (l, 0, j)),
                  pl.BlockSpec((1, 1, tn), lambda l, j: (l, 0, j))],
        out_specs=pl.BlockSpec((1, rows, tn), lambda l, j: (l, 0, j)),
        out_shape=jax.ShapeDtypeStruct((DEPTH, rows, 6 * D_MODEL), _F32),
        compiler_params=_params("arbitrary", "arbitrary"),
    )(c_all, ada_w, ada_b.reshape(DEPTH, 1, 6 * D_MODEL))


def _in_proj_kernel(x_ref, sh_ref, sc_ref, g_ref, w_ref, z_ref, xbc_ref, frw_ref, us5_ref, dt_ref):
    x = x_ref[...]
    bb, tl, _ = x.shape
    xn = x * lax.rsqrt(jnp.mean(x * x, axis=-1, keepdims=True) + NORM_EPS)
    h = xn * g_ref[...] * (1.0 + sc_ref[...]) + sh_ref[...]
    h = h.reshape(bb * tl, D_MODEL).astype(_BF16)
    for o_ref, lo, hi in ((z_ref, _PZ, _PXBC), (xbc_ref, _PXBC, _PFRW), (frw_ref, _PFRW, _PUS5),
                          (us5_ref, _PUS5, _PDT), (dt_ref, _PDT, _P_COLS)):
        o_ref[...] = _dot(h, w_ref[:, lo:hi]).reshape(o_ref.shape)


def _in_proj(x, mod, norm_g, w_in_p, l, *, bb, tl, time_major_s5):
    nb, L, _ = x.shape
    grid = (nb // bb, L // tl)
    xmap = lambda i, t: (i, t, 0)

    def modspec(j):
        return pl.BlockSpec((bb, 1, D_MODEL), lambda i, t: (i, 0, j))

    def out(n):
        return jax.ShapeDtypeStruct((nb, L, n), _F32), pl.BlockSpec((bb, tl, n), xmap)

    outs = [out(SSD_WIDTH), out(SSD_CONV_DIM), out(RWKV_PROJ), out(S5_WIDTH), out(DT_PAD)]
    if time_major_s5:
        assert bb == 1
        outs[3] = (jax.ShapeDtypeStruct((L, nb * S5_WIDTH), _F32), pl.BlockSpec((tl, S5_WIDTH), lambda i, t: (t, i)))
    return pl.pallas_call(
        _in_proj_kernel,
        grid=grid,
        in_specs=[pl.BlockSpec((bb, tl, D_MODEL), xmap), modspec(0), modspec(1),
                  pl.BlockSpec((1, D_MODEL), lambda i, t: (0, 0)),
                  pl.BlockSpec((None, D_MODEL, _P_COLS), lambda i, t: (l, 0, 0))],
        out_specs=[o[1] for o in outs],
        out_shape=[o[0] for o in outs],
        compiler_params=_params("arbitrary", "arbitrary"),
    )(x, mod, mod, norm_g.reshape(1, D_MODEL), w_in_p)


def _ssd_seq(j, z_ref, xbc_ref, dt_ref, cw_ref, cb_ref, dtb_ref, alog_ref, dexp_ref, ng_ref, eexp_ref,
             y_ref, sto_ref, cvo_ref, ext_scr, ybuf_scr, T, valid):
    u = xbc_ref[j]
    ext_scr[j, 0:SUBLANES, :] = cvo_ref[j]
    ext_scr[j, SUBLANES:SUBLANES + T, :] = u
    conv = cb_ref[...] + cw_ref[3:4, :] * u
    for back in (1, 2, 3):
        conv = conv + cw_ref[3 - back:4 - back, :] * ext_scr[j, SUBLANES - back:SUBLANES - back + T, :]
    cvo_ref[j] = ext_scr[j, valid:valid + SUBLANES, :]
    xbc = _silu(conv)
    xs = xbc[:, :SSD_WIDTH]
    row = _iota((T, 1), 0)
    dtv = _softplus(dt_ref[j] + dtb_ref[...])
    if valid < T:
        dtv = jnp.where(row < valid, dtv, 0.0)
    a_row = jnp.where(_iota((1, DT_PAD), 1) < SSD_HEADS, -jnp.exp(alog_ref[...]), 0.0)
    a = dtv * a_row
    yield
    tri_b = _iota((T, T), 0) >= _iota((T, T), 1)
    cs = _mm_sel(tri_b, a, _NN)
    eye = _iota((DT_PAD, DT_PAD), 0) == _iota((DT_PAD, DT_PAD), 1)
    xdt = xs * _mm_sel(eexp_ref[...], dtv, _NN, x_is_rhs=False)
    yield
    cs_t = _mm_sel(eye, cs, _NT)
    ecs = jnp.exp(cs)
    cs_last = cs[T - 1:T, :]
    dte = jnp.exp(cs_last - cs)
    elast = jnp.exp(cs_last)
    for g in range(SSD_GROUPS):
        bm = xbc[:, SSD_WIDTH + g * SSD_STATE:SSD_WIDTH + (g + 1) * SSD_STATE]
        cm = xbc[:, SSD_WIDTH + (SSD_GROUPS + g) * SSD_STATE:SSD_WIDTH + (SSD_GROUPS + g + 1) * SSD_STATE]
        cb = _dot_nt(cm, bm)
        for r in range(SSD_HEADS_PER_GROUP):
            yield
            h = g * SSD_HEADS_PER_GROUP + r
            lo, hi = h * SSD_HEAD_DIM, (h + 1) * SSD_HEAD_DIM
            decay = jnp.exp(jnp.where(tri_b, cs[:, h:h + 1] - cs_t[h:h + 1, :], -jnp.inf))
            xdt_h = xdt[:, lo:hi]
            st = sto_ref[j, h]
            y_h = _dot(cb * decay, xdt_h) + ecs[:, h:h + 1] * _dot_nt(cm, st)
            ybuf_scr[j, :, lo:hi] = y_h
            sto_ref[j, h] = elast[:, h:h + 1] * st + _dot_tn(xdt_h * dte[:, h:h + 1], bm)
    yield
    y = ybuf_scr[j] + xs * dexp_ref[...]
    y = y * _silu(z_ref[j])
    y = y * lax.rsqrt(jnp.mean(y * y, axis=-1, keepdims=True) + NORM_EPS) * ng_ref[...]
    y_ref[j] = y


def _ssd_kernel(z_ref, xbc_ref, dt_ref, st_ref, cv_ref, cw_ref, cb_ref, dtb_ref, alog_ref, dexp_ref, ng_ref,
                eexp_ref, *rest, bb, T, valid, n_earlier):
    earlier = rest[:n_earlier]
    y_ref, sto_all, cvo_ref, ext_scr, ybuf_scr = rest[n_earlier:]
    sto_ref = sto_all.at[n_earlier] if n_earlier else sto_all

    @pl.when(pl.program_id(1) == 0)
    def _():
        sto_ref[...] = st_ref[...]
        cvo_ref[...] = cv_ref[...]
        for k, e_ref in enumerate(earlier):
            sto_all[k] = e_ref[...]

    seq = functools.partial(_ssd_seq, z_ref=z_ref, xbc_ref=xbc_ref, dt_ref=dt_ref, cw_ref=cw_ref, cb_ref=cb_ref,
                            dtb_ref=dtb_ref, alog_ref=alog_ref, dexp_ref=dexp_ref, ng_ref=ng_ref, eexp_ref=eexp_ref,
                            y_ref=y_ref, sto_ref=sto_ref, cvo_ref=cvo_ref, ext_scr=ext_scr, ybuf_scr=ybuf_scr,
                            T=T, valid=valid)
    _interleave(seq(j) for j in range(bb))


def _ssd(z, xbc, dt, st_ssd, st_conv8, l, conv_w, conv_b, dt_bias, a_log, d, norm_g, *, bb, T, valid, earlier=()):
    nb, L, _ = z.shape
    grid = (nb // bb, L // T)
    tmap = lambda i, c: (i, c, 0)
    cmap2 = lambda i, c: (0, 0)
    smap = lambda i, c: (i, 0, 0, 0)
    cvmap = lambda i, c: (i, 0, 0)
    pad = DT_PAD - SSD_HEADS
    eexp = jnp.repeat(jnp.eye(DT_PAD, SSD_HEADS, dtype=_F32), SSD_HEAD_DIM, axis=1)
    state_blk = (bb, SSD_HEADS, SSD_HEAD_DIM, SSD_STATE)
    n_e = len(earlier)
    if n_e:
        state_out = (pl.BlockSpec((n_e + 1,) + state_blk, lambda i, c: (0, i, 0, 0, 0)),
                     jax.ShapeDtypeStruct((n_e + 1,) + st_ssd.shape[1:], _F32))
    else:
        state_out = (pl.BlockSpec(state_blk, smap), jax.ShapeDtypeStruct(st_ssd.shape[1:], _F32))
    return pl.pallas_call(
        functools.partial(_ssd_kernel, bb=bb, T=T, valid=valid, n_earlier=n_e),
        grid=grid,
        in_specs=[pl.BlockSpec((bb, T, SSD_WIDTH), tmap), pl.BlockSpec((bb, T, SSD_CONV_DIM), tmap),
                  pl.BlockSpec((bb, T, DT_PAD), tmap),
                  pl.BlockSpec((None, bb, SSD_HEADS, SSD_HEAD_DIM, SSD_STATE), lambda i, c: (l, i, 0, 0, 0)),
                  pl.BlockSpec((None, bb, SUBLANES, SSD_CONV_DIM), lambda i, c: (l, i, 0, 0)),
                  pl.BlockSpec((SSD_CONV, SSD_CONV_DIM), cmap2), pl.BlockSpec((1, SSD_CONV_DIM), cmap2),
                  pl.BlockSpec((1, DT_PAD), cmap2), pl.BlockSpec((1, DT_PAD), cmap2),
                  pl.BlockSpec((1, SSD_WIDTH), cmap2), pl.BlockSpec((1, SSD_WIDTH), cmap2),
                  pl.BlockSpec((DT_PAD, SSD_WIDTH), cmap2)] + [pl.BlockSpec(state_blk, smap)] * n_e,
        out_specs=[pl.BlockSpec((bb, T, SSD_WIDTH), tmap), state_out[0],
                   pl.BlockSpec((bb, SUBLANES, SSD_CONV_DIM), cvmap)],
        out_shape=[jax.ShapeDtypeStruct((nb, L, SSD_WIDTH), _F32), state_out[1],
                   jax.ShapeDtypeStruct(st_conv8.shape[1:], _F32)],
        scratch_shapes=[pltpu.VMEM((bb, T + SUBLANES, SSD_CONV_DIM), _F32), pltpu.VMEM((bb, T, SSD_WIDTH), _F32)],
        compiler_params=_params("arbitrary", "arbitrary"),
    )(z, xbc, dt, st_ssd, st_conv8, conv_w, conv_b.reshape(1, -1),
      jnp.pad(dt_bias, (0, pad)).reshape(1, DT_PAD), jnp.pad(a_log, (0, pad)).reshape(1, DT_PAD),
      jnp.repeat(d, SSD_HEAD_DIM).reshape(1, SSD_WIDTH), norm_g.reshape(1, SSD_WIDTH), eexp, *earlier)


def _stack_heads(x):
    return jnp.concatenate([x[:, h * RWKV_HEAD_DIM:(h + 1) * RWKV_HEAD_DIM] for h in range(RWKV_HEADS)], axis=0)


def _unstack_heads(x, T):
    return jnp.concatenate([x[h * T:(h + 1) * T, :] for h in range(RWKV_HEADS)], axis=1)


def _interleave(gens):
    gens = list(gens)
    while gens:
        alive = []
        for g in gens:
            try:
                next(g)
                alive.append(g)
            except StopIteration:
                pass
        gens = alive


def _rwkv_masks(T):
    HT = RWKV_HEADS * T
    rr = _iota((HT, HT), 0)
    cc = _iota((HT, HT), 1)
    same = (rr // T) == (cc // T)
    rt, ct = _iota((T, T), 0), _iota((T, T), 1)
    head_of_row = _iota((HT, RWKV_WIDTH), 0) // T
    head_of_col = _iota((HT, RWKV_WIDTH), 1) // RWKV_HEAD_DIM
    return dict(strict=jnp.logical_and(same, rr > cc), incl=jnp.logical_and(same, rr >= cc),
                eye=(rr == cc).astype(_F32), hmask=head_of_row == head_of_col,
                strict_t=rt > ct, incl_t=rt >= ct, eye_t=(rt == ct).astype(_F32), tri=(rt >= ct).astype(_BF16))


def _rwkv_inverse(eye, mats, T):
    inv = [eye + n for n in mats]
    p = list(mats)
    for _ in range(max((T - 1).bit_length() - 1, 0)):
        yield
        p = [_dot(x, x, P_INV) for x in p]
        yield
        inv = [i + _dot(i, x, P_INV) for i, x in zip(inv, p)]
    return inv


def _rwkv_core_per_head(j, masks, so_ref, T, at, rt, bt, kt, v4f, w_end):
    cast = lambda x: x.astype(_BF16)
    a4, r4, b4, k4 = (cast(_stack_heads(t)) for t in (at, rt, bt, kt))
    v4 = cast(v4f)
    bw4, kw4 = (cast(_stack_heads(t * w_end)) for t in (bt, kt))
    heads = range(RWKV_HEADS)
    blk = lambda x, h: x[h * T:(h + 1) * T]
    yield
    n_ab = [jnp.where(masks['strict_t'], _dot_nt(blk(a4, h), blk(b4, h), P_AMAT), 0.0) for h in heads]
    n_ak = [cast(jnp.where(masks['strict_t'], _dot_nt(blk(a4, h), blk(k4, h), P_AMAT), 0.0)) for h in heads]
    m_rb = [cast(jnp.where(masks['incl_t'], _dot_nt(blk(r4, h), blk(b4, h), P_AMAT), 0.0)) for h in heads]
    m_rk = [cast(jnp.where(masks['incl_t'], _dot_nt(blk(r4, h), blk(k4, h), P_AMAT), 0.0)) for h in heads]
    inv = yield from _rwkv_inverse(masks['eye_t'], n_ab, T)
    yield
    s0 = [so_ref[j, h] for h in heads]
    rhs = [_dot_nt(blk(a4, h), s0[h], P_S0) + _dot(n_ak[h], blk(v4, h), P_INTRA) for h in heads]
    o_own = [_dot_nt(blk(r4, h), s0[h], P_S0) + _dot(m_rk[h], blk(v4, h), P_INTRA) for h in heads]
    yield
    u = [cast(_dot(inv[h], rhs[h], P_SOLVE)) for h in heads]
    yield
    o4 = jnp.concatenate([o_own[h] + _dot(m_rb[h], u[h], P_INTRA) for h in heads], axis=0)
    for h in heads:
        so_ref[j, h] = (s0[h] * w_end[:, h * RWKV_HEAD_DIM:(h + 1) * RWKV_HEAD_DIM]
                        + _dot_tn(u[h], blk(bw4, h), P_STATE) + _dot_tn(blk(v4, h), blk(kw4, h), P_STATE))
    yield
    return o4


def _rwkv_core_stacked(j, masks, so_ref, T, at, rt, bt, kt, v4f, w_end):
    strict, incl, hmask = masks['strict'], masks['incl'], masks['hmask']
    a4, r4, b4, k4 = (_stack_heads(t) for t in (at, rt, bt, kt))
    yield
    n_ab = jnp.where(strict, _dot_nt(a4, b4, P_AMAT), 0.0)
    n_ak = jnp.where(strict, _dot_nt(a4, k4, P_AMAT), 0.0)
    m_rb = jnp.where(incl, _dot_nt(r4, b4, P_AMAT), 0.0)
    m_rk = jnp.where(incl, _dot_nt(r4, k4, P_AMAT), 0.0)
    (inv,) = yield from _rwkv_inverse(masks['eye'], [n_ab], T)

    def wide(x):
        return jnp.where(hmask, jnp.concatenate([x] * RWKV_HEADS, axis=0), 0.0)

    yield
    s0 = jnp.concatenate([so_ref[j, h] for h in range(RWKV_HEADS)], axis=1)
    rhs = _dot_nt(wide(at), s0, P_S0) + _dot(n_ak, v4f, P_INTRA)
    o4 = _dot_nt(wide(rt), s0, P_S0) + _dot(m_rk, v4f, P_INTRA)
    yield
    u4 = _dot(inv, rhs, P_SOLVE)
    yield
    o4 = o4 + _dot(m_rb, u4, P_INTRA)
    s1 = (s0 * w_end + _dot_tn(u4, wide(bt * w_end), P_STATE) + _dot_tn(v4f, wide(kt * w_end), P_STATE))
    for h in range(RWKV_HEADS):
        so_ref[j, h] = s1[:, h * RWKV_HEAD_DIM:(h + 1) * RWKV_HEAD_DIM]
    yield
    return o4


def _rwkv_seq(j, masks, f_ref, mu_ref, w0_ref, w2_ref, a0_ref, a2_ref, g2_ref, kk_ref, ka_ref, rk_ref, lng_ref,
              lnb_ref, y_ref, so_ref, sho_ref, ext_scr, T, valid):
    f = f_ref[j]
    ext_scr[j, 0:SUBLANES, :] = sho_ref[j]
    ext_scr[j, SUBLANES:SUBLANES + T, :] = f
    prev = ext_scr[j, SUBLANES - 1:SUBLANES - 1 + T, :]
    sho_ref[j] = ext_scr[j, valid:valid + SUBLANES, :]
    fm = f + (prev - f) * mu_ref[...]
    r = fm[:, 0:256]
    k = fm[:, 256:512]
    v = fm[:, 512:768]
    lora = fm[:, 768:896]
    gl = fm[:, 896:1024]
    w_log = -_softplus(-(w0_ref[...] + _dot(jnp.tanh(lora), w2_ref[...]))) - 0.5
    logw = -jnp.exp(w_log)
    a = _sigmoid(a0_ref[...] + _dot(lora, a2_ref[...]))
    g = _dot(_sigmoid(gl), g2_ref[...])
    k2 = k * (1.0 + (a - 1.0) * ka_ref[...])
    kk4 = _stack_heads(k * kk_ref[...])
    kk4 = kk4 * lax.rsqrt(jnp.maximum(jnp.sum(kk4 * kk4, axis=-1, keepdims=True), 1e-24))
    kkn = _unstack_heads(kk4, T)
    if valid < T:
        live = _iota((T, 1), 0) < valid
        logw = jnp.where(live, logw, 0.0)
        kkn = jnp.where(live, kkn, 0.0)
        k2 = jnp.where(live, k2, 0.0)
    yield
    cw = _mm_sel(masks['tri'], logw, _NN)
    w_inc = jnp.exp(cw)
    w_exc = jnp.exp(cw - logw)
    w_inv = jnp.exp(-cw)
    w_end = w_inc[T - 1:T, :]
    at = -kkn * w_exc
    rt = r * w_inc
    bt = kkn * a * w_inv
    kt = k2 * w_inv
    v4f = _stack_heads(v)
    core = _rwkv_core_per_head if T % 16 == 0 else _rwkv_core_stacked
    o4 = yield from core(j, masks, so_ref, T, at, rt, bt, kt, v4f, w_end)

    mean = jnp.mean(o4, axis=-1, keepdims=True)
    var = jnp.mean(jnp.square(o4 - mean), axis=-1, keepdims=True)
    o4 = (o4 - mean) * lax.rsqrt(var + RWKV_LN_EPS)
    bonus4 = jnp.sum(_stack_heads(r * k2 * rk_ref[...]), axis=-1, keepdims=True) * v4f
    o = _unstack_heads(o4, T) * lng_ref[...] + lnb_ref[...] + _unstack_heads(bonus4, T)
    y_ref[j] = o * g


def _rwkv_kernel(f_ref, s_ref, sh_ref, mu_ref, w0_ref, w2_ref, a0_ref, a2_ref, g2_ref, kk_ref, ka_ref, rk_ref,
                 lng_ref, lnb_ref, *rest, bb, T, valid, n_earlier):
    earlier = rest[:n_earlier]
    y_ref, so_all, sho_ref, ext_scr = rest[n_earlier:]
    so_ref = so_all.at[n_earlier] if n_earlier else so_all

    @pl.when(pl.program_id(1) == 0)
    def _():
        so_ref[...] = s_ref[...]
        sho_ref[...] = sh_ref[...]
        for k, e_ref in enumerate(earlier):
            so_all[k] = e_ref[...]

    seq = functools.partial(_rwkv_seq, f_ref=f_ref, mu_ref=mu_ref, w0_ref=w0_ref, w2_ref=w2_ref, a0_ref=a0_ref,
                            a2_ref=a2_ref, g2_ref=g2_ref, kk_ref=kk_ref, ka_ref=ka_ref, rk_ref=rk_ref,
                            lng_ref=lng_ref, lnb_ref=lnb_ref, y_ref=y_ref, so_ref=so_ref, sho_ref=sho_ref,
                            ext_scr=ext_scr, T=T, valid=valid)
    masks = _rwkv_masks(T)
    _interleave(seq(j, masks) for j in range(bb))


def _rwkv(f, st_rwkv, shift8, l, mu, w0, w2, a0, a2, g2, k_k, k_a, r_k, ln_g, ln_b, *, bb, T, valid, earlier=()):
    nb, L, _ = f.shape
    hd = RWKV_HEAD_DIM
    state_blk = (bb, RWKV_HEADS, hd, hd)
    smap = lambda i, c: (i, 0, 0, 0)
    n_e = len(earlier)
    if n_e:
        state_out = (pl.BlockSpec((n_e + 1,) + state_blk, lambda i, c: (0, i, 0, 0, 0)),
                     jax.ShapeDtypeStruct((n_e + 1,) + st_rwkv.shape[1:], _F32))
    else:
        state_out = (pl.BlockSpec(state_blk, smap), jax.ShapeDtypeStruct(st_rwkv.shape[1:], _F32))
    grid = (nb // bb, L // T)
    tmap = lambda i, c: (i, c, 0)
    bmap = lambda i, c: (i, 0, 0)
    cmap2 = lambda i, c: (0, 0)
    row = lambda n: pl.BlockSpec((1, n), cmap2)
    lora_w = RWKV_DECAY_LORA + RWKV_A_LORA
    w2_pad = jnp.concatenate([w2, jnp.zeros((RWKV_A_LORA, RWKV_WIDTH), _F32)], axis=0)
    a2_pad = jnp.concatenate([jnp.zeros((RWKV_DECAY_LORA, RWKV_WIDTH), _F32), a2], axis=0)
    return pl.pallas_call(
        functools.partial(_rwkv_kernel, bb=bb, T=T, valid=valid, n_earlier=n_e),
        grid=grid,
        in_specs=[pl.BlockSpec((bb, T, RWKV_PROJ), tmap),
                  pl.BlockSpec((None, bb, RWKV_HEADS, hd, hd), lambda i, c: (l, i, 0, 0, 0)),
                  pl.BlockSpec((None, bb, SUBLANES, RWKV_PROJ), lambda i, c: (l, i, 0, 0)),
                  row(RWKV_PROJ), row(RWKV_WIDTH), pl.BlockSpec((lora_w, RWKV_WIDTH), cmap2),
                  row(RWKV_WIDTH), pl.BlockSpec((lora_w, RWKV_WIDTH), cmap2),
                  pl.BlockSpec((RWKV_GATE_LORA, RWKV_WIDTH), cmap2),
                  row(RWKV_WIDTH), row(RWKV_WIDTH), row(RWKV_WIDTH), row(RWKV_WIDTH), row(RWKV_WIDTH)]
        + [pl.BlockSpec(state_blk, smap)] * n_e,
        out_specs=[pl.BlockSpec((bb, T, RWKV_WIDTH), tmap), state_out[0],
                   pl.BlockSpec((bb, SUBLANES, RWKV_PROJ), bmap)],
        out_shape=[jax.ShapeDtypeStruct((nb, L, RWKV_WIDTH), _F32), state_out[1],
                   jax.ShapeDtypeStruct(shift8.shape[1:], _F32)],
        scratch_shapes=[pltpu.VMEM((bb, T + SUBLANES, RWKV_PROJ), _F32)],
        compiler_params=_params("arbitrary", "arbitrary"),
    )(f, st_rwkv, shift8, mu.reshape(1, -1), w0.reshape(1, -1), w2_pad, a0.reshape(1, -1), a2_pad, g2,
      k_k.reshape(1, -1), k_a.reshape(1, -1), r_k.reshape(1, -1), ln_g.reshape(1, -1), ln_b.reshape(1, -1),
      *earlier)


def _s5_kernel(u_ref, sre_ref, sim_ref, lre_ref, lim_ref, dt_ref, bre_ref, bim_ref, cre_ref, cim_ref, d_ref,
               gw_ref, gb_ref, y_ref, hre_ref, him_ref, hre_scr, him_scr, *, nb, tl, final_step):
    @pl.when(pl.program_id(0) == 0)
    def _():
        hre_ref[...] = sre_ref[...]
        him_ref[...] = sim_ref[...]

    lre, lim, dt = lre_ref[...], lim_ref[...], jnp.exp(dt_ref[...])
    mag = jnp.exp(lre * dt)
    ang = lim * dt
    ab_re, ab_im = mag * jnp.cos(ang), mag * jnp.sin(ang)
    den = jnp.square(lre) + jnp.square(lim)
    q_re = ((ab_re - 1.0) * lre + ab_im * lim) / den
    q_im = (ab_im * lre - (ab_re - 1.0) * lim) / den
    b_re, b_im = bre_ref[...], bim_ref[...]
    bb_re = q_re * b_re - q_im * b_im
    bb_im = q_re * b_im + q_im * b_re
    u = u_ref[...]
    hre_scr[0:nb, :] = hre_ref[...]
    him_scr[0:nb, :] = him_ref[...]
    hre_scr[nb:, :] = _dot(u, bb_re)
    him_scr[nb:, :] = _dot(u, bb_im)
    ar = jnp.broadcast_to(ab_re, (nb, S5_FLAT))
    ai = jnp.broadcast_to(ab_im, (nb, S5_FLAT))

    if nb <= SUBLANES:
        def step(s, carry):
            pr, pi = carry
            cur = pl.ds(pl.multiple_of((s + 1) * nb, nb), nb)
            nr = ar * pr - ai * pi + hre_scr[cur, :]
            ni = ar * pi + ai * pr + him_scr[cur, :]
            hre_scr[cur, :] = nr
            him_scr[cur, :] = ni
            return nr, ni

        lax.fori_loop(0, tl, step, (hre_ref[...], him_ref[...]), unroll=2)
    else:
        def step(s, carry):
            prev = pl.ds(pl.multiple_of(s * nb, nb), nb)
            cur = pl.ds(pl.multiple_of((s + 1) * nb, nb), nb)
            pr, pi = hre_scr[prev, :], him_scr[prev, :]
            hre_scr[cur, :] = ar * pr - ai * pi + hre_scr[cur, :]
            him_scr[cur, :] = ar * pi + ai * pr + him_scr[cur, :]
            return carry

        lax.fori_loop(0, tl, step, 0)
    hre_ref[...] = hre_scr[(final_step + 1) * nb:(final_step + 2) * nb, :]
    him_ref[...] = him_scr[(final_step + 1) * nb:(final_step + 2) * nb, :]
    y = _dot(hre_scr[nb:, :], cre_ref[...]) - _dot(him_scr[nb:, :], cim_ref[...]) + d_ref[...] * u
    y = 0.5 * y * (1.0 + jnp.tanh(0.7978845608028654 * (y + 0.044715 * (y * y * y))))
    y_ref[...] = y * _sigmoid(_dot(y, gw_ref[...]) + gb_ref[...])


def _s5(u_tm, st_re, st_im, l, a_re, a_im, log_dt, b_re, b_im, c_re, c_im, d, glu_w, glu_b, *, nb, tl,
        final_step):
    rows = u_tm.shape[0]
    L = rows // nb
    eye = jnp.eye(S5_GROUPS, dtype=_F32)
    bd_in = lambda b: jnp.einsum('gpc,gh->gchp', b, eye).reshape(S5_WIDTH, S5_FLAT)
    bd_out = lambda c: jnp.einsum('gcp,gh->gphc', c, eye).reshape(S5_FLAT, S5_WIDTH)
    cmap = lambda t: (0, 0)
    row = lambda n: pl.BlockSpec((1, n), cmap)
    full = lambda a, b: pl.BlockSpec((a, b), cmap)
    state = pl.BlockSpec((None, nb, S5_FLAT), lambda t: (l, 0, 0))
    return pl.pallas_call(
        functools.partial(_s5_kernel, nb=nb, tl=tl, final_step=final_step),
        grid=(L // tl,),
        in_specs=[pl.BlockSpec((tl * nb, S5_WIDTH), lambda t: (t, 0)), state, state,
                  row(S5_FLAT), row(S5_FLAT), row(S5_FLAT),
                  full(S5_WIDTH, S5_FLAT), full(S5_WIDTH, S5_FLAT), full(S5_FLAT, S5_WIDTH), full(S5_FLAT, S5_WIDTH),
                  row(S5_WIDTH), full(S5_WIDTH, S5_WIDTH), row(S5_WIDTH)],
        out_specs=[pl.BlockSpec((tl * nb, S5_WIDTH), lambda t: (t, 0)), full(nb, S5_FLAT), full(nb, S5_FLAT)],
        out_shape=[jax.ShapeDtypeStruct((rows, S5_WIDTH), _F32),
                   jax.ShapeDtypeStruct((nb, S5_FLAT), _F32), jax.ShapeDtypeStruct((nb, S5_FLAT), _F32)],
        scratch_shapes=[pltpu.VMEM(((tl + 1) * nb, S5_FLAT), _F32), pltpu.VMEM(((tl + 1) * nb, S5_FLAT), _F32)],
        compiler_params=_params("arbitrary"),
    )(u_tm, st_re, st_im, a_re.reshape(1, S5_FLAT), a_im.reshape(1, S5_FLAT),
      jnp.repeat(log_dt, S5_STATE).reshape(1, S5_FLAT), bd_in(b_re), bd_in(b_im), bd_out(c_re), bd_out(c_im),
      d.reshape(1, S5_WIDTH), glu_w, glu_b.reshape(1, S5_WIDTH))


def _out_mlp_kernel(x_ref, yssd_ref, yrw_ref, ys5_ref, g1_ref, sh2_ref, sc2_ref, g2_ref, n2g_ref, fg_ref,
                    wo_ref, w1_ref, w2_ref, o_ref, hid_scr, *, final_norm):
    bb, tl, _ = x_ref.shape
    rows = bb * tl
    flat = lambda ref: ref[...].reshape(rows, ref.shape[-1]).astype(_BF16)
    mix = (_dot(flat(yssd_ref), wo_ref[0:512, :]) + _dot(flat(yrw_ref), wo_ref[512:768, :])
           + _dot(flat(ys5_ref), wo_ref[768:1024, :]))
    x1 = x_ref[...] + g1_ref[...] * mix.reshape(bb, tl, D_MODEL)
    xn = x1 * lax.rsqrt(jnp.mean(x1 * x1, axis=-1, keepdims=True) + NORM_EPS)
    h2 = (xn * n2g_ref[...] * (1.0 + sc2_ref[...]) + sh2_ref[...]).reshape(rows, D_MODEL).astype(_BF16)
    for c in range(D_FF // FF_TILE):
        cols = slice(c * FF_TILE, (c + 1) * FF_TILE)
        hid_scr[:, cols] = jnp.square(jnp.maximum(_dot(h2, w1_ref[:, cols]), 0.0)).astype(_BF16)
    ff = _dot(hid_scr[...], w2_ref[...])
    out = x1 + g2_ref[...] * ff.reshape(bb, tl, D_MODEL)
    if final_norm:
        out = out * lax.rsqrt(jnp.mean(out * out, axis=-1, keepdims=True) + NORM_EPS) * fg_ref[...]
    o_ref[...] = out


def _out_mlp(x, y_ssd, y_rw, y_s5, mod, norm2_g, final_g, w_out_b, w1_b, w2_b, l, *, bb, tl, time_major_s5,
             final_norm):
    nb, L, _ = x.shape
    grid = (nb // bb, L // tl)
    xmap = lambda i, t: (i, t, 0)
    cmap = lambda i, t: (0, 0)
    wmap = lambda i, t: (l, 0, 0)
    once = pl.Buffered(1)

    def modspec(j):
        return pl.BlockSpec((bb, 1, D_MODEL), lambda i, t: (i, 0, j))

    s5_spec = (pl.BlockSpec((tl, S5_WIDTH), lambda i, t: (t, i)) if time_major_s5
               else pl.BlockSpec((bb, tl, S5_WIDTH), xmap))
    return pl.pallas_call(
        functools.partial(_out_mlp_kernel, final_norm=final_norm),
        grid=grid,
        in_specs=[pl.BlockSpec((bb, tl, D_MODEL), xmap), pl.BlockSpec((bb, tl, SSD_WIDTH), xmap),
                  pl.BlockSpec((bb, tl, RWKV_WIDTH), xmap), s5_spec,
                  modspec(2), modspec(3), modspec(4), modspec(5),
                  pl.BlockSpec((1, D_MODEL), cmap), pl.BlockSpec((1, D_MODEL), cmap),
                  pl.BlockSpec((None, D_MODEL, D_MODEL), wmap, pipeline_mode=once),
                  pl.BlockSpec((None, D_MODEL, D_FF), wmap, pipeline_mode=once),
                  pl.BlockSpec((None, D_FF, D_MODEL), wmap, pipeline_mode=once)],
        out_specs=pl.BlockSpec((bb, tl, D_MODEL), xmap),
        out_shape=jax.ShapeDtypeStruct(x.shape, _F32),
        scratch_shapes=[pltpu.VMEM((bb * tl, D_FF), _BF16)],
        compiler_params=_params("arbitrary", "arbitrary"),
    )(x, y_ssd, y_rw, y_s5, mod, mod, mod, mod, norm2_g.reshape(1, D_MODEL), final_g.reshape(1, D_MODEL),
      w_out_b, w1_b, w2_b)


def _trunk(x, mod, states, W, final_g, *, prompt):
    st_ssd, st_conv, st_rwkv, st_shift, st_re, st_im = states
    nb, L, _ = x.shape
    if prompt:
        bb, tl, mlp_tl = 1, min(ROW_TILE, L), min(MLP_ROW_TILE, L)
        ssd_t, rwkv_t, s5_tl = min(SSD_CHUNK, L), min(RWKV_CHUNK, L), min(S5_TILE, L)
        ssd_bb, rwkv_bb = min(nb, SSD_PROMPT_SEQS), min(nb, RWKV_PROMPT_SEQS)
    else:
        bb, tl, mlp_tl = min(nb, ROW_TILE // L), L, L
        ssd_t = rwkv_t = s5_tl = L
        ssd_bb = rwkv_bb = min(nb, SAMPLE_SEQS)
    new = ([], [], [], [], [], [])
    for l in range(DEPTH):
        z, xbc, frw, us5, dt = _in_proj(x, mod[l], W['norm1_g'][l], W['w_in_p'], l, bb=bb, tl=tl,
                                        time_major_s5=prompt)
        v = W['valid']
        last = l == DEPTH - 1
        y_ssd, n_ssd, n_conv = _ssd(z, xbc, dt, st_ssd, st_conv, l, W['ssd_conv_w'][l], W['ssd_conv_b'][l],
                                    W['ssd_dt_bias'][l], W['ssd_a_log'][l], W['ssd_d'][l], W['ssd_norm_g'][l],
                                    bb=ssd_bb, T=ssd_t, valid=ssd_t if prompt else v,
                                    earlier=tuple(new[0]) if last else ())
        y_rw, n_rwkv, n_shift = _rwkv(frw, st_rwkv, st_shift, l, W['rwkv_mu'][l], W['rwkv_w0'][l],
                                      W['rwkv_w2'][l], W['rwkv_a0'][l], W['rwkv_a2'][l], W['rwkv_g2'][l],
                                      W['rwkv_k_k'][l], W['rwkv_k_a'][l], W['rwkv_r_k'][l], W['rwkv_ln_g'][l],
                                      W['rwkv_ln_b'][l], bb=rwkv_bb, T=rwkv_t, valid=rwkv_t if prompt else v,
                                      earlier=tuple(new[2]) if last else ())
        if prompt:
            u_tm = us5.reshape(L * nb, S5_WIDTH)
        else:
            u_tm = jnp.transpose(us5, (1, 0, 2)).reshape(L * nb, S5_WIDTH)
        y_s5, n_re, n_im = _s5(u_tm, st_re, st_im, l, W['s5_a_re'][l], W['s5_a_im'][l], W['s5_log_dt'][l],
                               W['s5_b_re'][l], W['s5_b_im'][l], W['s5_c_re'][l], W['s5_c_im'][l], W['s5_d'][l],
                               W['s5_glu_w'][l], W['s5_glu_b'][l], nb=nb, tl=s5_tl,
                               final_step=(s5_tl - 1) if prompt else (v - 1))
        if prompt:
            y_s5 = y_s5.reshape(L, nb * S5_WIDTH)
        else:
            y_s5 = jnp.transpose(y_s5.reshape(L, nb, S5_WIDTH), (1, 0, 2))
        x = _out_mlp(x, y_ssd, y_rw, y_s5, mod[l], W['norm2_g'][l], final_g, W['w_out_b'], W['w1_b'],
                     W['w2_b'], l, bb=bb, tl=mlp_tl, time_major_s5=prompt, final_norm=(l == DEPTH - 1))
        for lst, s in zip(new, (n_ssd, n_conv, n_rwkv, n_shift, n_re, n_im)):
            lst.append(s)
    by_kernel = (0, 2) if DEPTH > 1 else ()
    return x, [lst[-1] if k in by_kernel else jnp.stack(lst) for k, lst in enumerate(new)]


def _pack_states(st_ssd, st_conv, st_rwkv, st_shift, st_re, st_im):
    d, nb = st_ssd.shape[:2]
    conv8 = jnp.pad(st_conv, ((0, 0), (0, 0), (SUBLANES - (SSD_CONV - 1), 0), (0, 0)))
    shift8 = jnp.pad(st_shift[:, :, None, :], ((0, 0), (0, 0), (SUBLANES - 1, 0), (0, 0)))
    return (st_ssd, conv8, st_rwkv, shift8, st_re.reshape(d, nb, S5_FLAT), st_im.reshape(d, nb, S5_FLAT))


def _unpack_states(n_ssd, n_conv8, n_rwkv, n_shift8, n_re, n_im):
    d, nb = n_ssd.shape[:2]
    return (n_ssd, n_conv8[:, :, SUBLANES - (SSD_CONV - 1):, :], n_rwkv, n_shift8[:, :, SUBLANES - 1, :],
            n_re.reshape(d, nb, S5_GROUPS, S5_STATE), n_im.reshape(d, nb, S5_GROUPS, S5_STATE))


def kernel(x_prompt, x_sample, c_prompt, c_sample, state_ssd, state_ssd_conv, state_rwkv, state_rwkv_shift,
           state_s5_re, state_s5_im, ada_w, ada_b, norm1_g, norm2_g, w_in, ssd_conv_w, ssd_conv_b, ssd_dt_bias,
           ssd_a_log, ssd_d, ssd_norm_g, rwkv_mu, rwkv_w0, rwkv_w2, rwkv_a0, rwkv_a2, rwkv_g2, rwkv_k_k, rwkv_k_a,
           rwkv_r_k, rwkv_ln_g, rwkv_ln_b, s5_a_re, s5_a_im, s5_log_dt, s5_b_re, s5_b_im, s5_c_re, s5_c_im, s5_d,
           s5_glu_w, s5_glu_b, w_out, mlp_w1, mlp_w2, final_g):
    bp, lp, _ = x_prompt.shape
    bs, ls, _ = x_sample.shape
    ls_pad = -(-ls // SUBLANES) * SUBLANES
    w_in_p = jnp.concatenate(
        [w_in[:, :, _Z0:_DT0], w_in[:, :, _FRW0:_IN_COLS], w_in[:, :, _DT0:_FRW0],
         jnp.zeros((DEPTH, D_MODEL, DT_PAD - SSD_HEADS), w_in.dtype)], axis=-1).astype(_BF16)
    W = dict(norm1_g=norm1_g, norm2_g=norm2_g, w_in_p=w_in_p, ssd_conv_w=ssd_conv_w, ssd_conv_b=ssd_conv_b,
             ssd_dt_bias=ssd_dt_bias, ssd_a_log=ssd_a_log, ssd_d=ssd_d, ssd_norm_g=ssd_norm_g, rwkv_mu=rwkv_mu,
             rwkv_w0=rwkv_w0, rwkv_w2=rwkv_w2, rwkv_a0=rwkv_a0, rwkv_a2=rwkv_a2, rwkv_g2=rwkv_g2,
             rwkv_k_k=rwkv_k_k, rwkv_k_a=rwkv_k_a, rwkv_r_k=rwkv_r_k.reshape(DEPTH, RWKV_WIDTH),
             rwkv_ln_g=rwkv_ln_g, rwkv_ln_b=rwkv_ln_b, s5_a_re=s5_a_re, s5_a_im=s5_a_im, s5_log_dt=s5_log_dt,
             s5_b_re=s5_b_re, s5_b_im=s5_b_im, s5_c_re=s5_c_re, s5_c_im=s5_c_im, s5_d=s5_d, s5_glu_w=s5_glu_w,
             s5_glu_b=s5_glu_b, w_out_b=w_out.astype(_BF16), w1_b=mlp_w1.astype(_BF16), w2_b=mlp_w2.astype(_BF16),
             valid=ls)
    mod = _ada(jnp.concatenate([c_prompt, c_sample], axis=0), ada_w, ada_b)
    mod_p = mod[:, :bp].reshape(DEPTH, bp, 1, 6 * D_MODEL)
    mod_s = mod[:, bp:].reshape(DEPTH, bs, 1, 6 * D_MODEL)

    zeros = lambda shape: jnp.zeros((DEPTH, bp) + shape, _F32)
    p_states = _pack_states(zeros((SSD_HEADS, SSD_HEAD_DIM, SSD_STATE)), zeros((SSD_CONV - 1, SSD_CONV_DIM)),
                            zeros((RWKV_HEADS, RWKV_HEAD_DIM, RWKV_HEAD_DIM)), zeros((RWKV_PROJ,)),
                            zeros((S5_GROUPS, S5_STATE)), zeros((S5_GROUPS, S5_STATE)))
    y_prompt, sp = _trunk(x_prompt, mod_p, p_states, W, final_g, prompt=True)

    s_states = _pack_states(state_ssd, state_ssd_conv, state_rwkv, state_rwkv_shift, state_s5_re, state_s5_im)
    x_s = jnp.pad(x_sample, ((0, 0), (0, ls_pad - ls), (0, 0)))
    y_sample, ss = _trunk(x_s, mod_s, s_states, W, final_g, prompt=False)
    return (y_prompt, y_sample[:, :ls]) + _unpack_states(*sp) + _unpack_states(*ss)
```

```python
import functools

import jax
import jax.numpy as jnp
from jax import lax
from jax.experimental import pallas as pl
from jax.experimental.pallas import tpu as pltpu

D_MODEL = 1024
DEPTH = 2
SSD_WIDTH = 512
SSD_HEAD_DIM = 64
SSD_HEADS = 8
SSD_GROUPS = 2
SSD_HEADS_PER_GROUP = SSD_HEADS // SSD_GROUPS
SSD_STATE = 128
SSD_CONV = 4
SSD_CONV_DIM = SSD_WIDTH + 2 * SSD_GROUPS * SSD_STATE
RWKV_WIDTH = 256
RWKV_HEAD_DIM = 64
RWKV_HEADS = 4
RWKV_DECAY_LORA = 64
RWKV_A_LORA = 64
RWKV_GATE_LORA = 128
RWKV_PROJ = 3 * RWKV_WIDTH + RWKV_DECAY_LORA + RWKV_A_LORA + RWKV_GATE_LORA
RWKV_LN_EPS = 64e-5
S5_WIDTH = 256
S5_GROUP_CH = 16
S5_GROUPS = 16
S5_STATE = 64
S5_FLAT = S5_GROUPS * S5_STATE
D_FF = 4 * D_MODEL
NORM_EPS = 1e-6
DT_PAD = 128
SUBLANES = 8

_Z0, _XBC0, _DT0, _FRW0, _US50 = 0, SSD_WIDTH, SSD_WIDTH + SSD_CONV_DIM, SSD_WIDTH + SSD_CONV_DIM + SSD_HEADS, \
    SSD_WIDTH + SSD_CONV_DIM + SSD_HEADS + RWKV_PROJ
_IN_COLS = _US50 + S5_WIDTH
_PZ, _PXBC, _PFRW, _PUS5, _PDT = 0, 512, 1536, 2560, 2816
_P_COLS = _PDT + DT_PAD

SSD_CHUNK = 128
RWKV_CHUNK = 64
SSD_PROMPT_SEQS = 2
RWKV_PROMPT_SEQS = 4
SAMPLE_SEQS = 8
S5_TILE = 256
ROW_TILE = 512
MLP_ROW_TILE = 512
FF_TILE = 1024
VMEM_LIMIT = 56 * 1024 * 1024

_HI = lax.Precision.HIGHEST
P_AMAT = 1
P_INV = 1
P_S0 = 1
P_INTRA = 1
P_SOLVE = 1
P_STATE = 1
_F32 = jnp.float32
_BF16 = jnp.bfloat16


_NN = (((1,), (0,)), ((), ()))
_NT = (((1,), (1,)), ((), ()))
_TN = (((0,), (0,)), ((), ()))


def _split(x):
    hi = x.astype(_BF16)
    return hi, (x - hi.astype(_F32)).astype(_BF16)


def _mm(a, b, dims, passes):
    dg = lambda x, y, prec=None: lax.dot_general(x, y, dims, preferred_element_type=_F32, precision=prec)
    if passes == 6:
        return dg(a.astype(_F32), b.astype(_F32), _HI)
    if passes == 3:
        ah, al = _split(a)
        bh, bl = _split(b)
        return dg(ah, bh) + (dg(ah, bl) + dg(al, bh))
    return dg(a.astype(_BF16), b.astype(_BF16))


def _dot(a, b, passes=1):
    return _mm(a, b, _NN, passes)


def _dot_nt(a, b, passes=1):
    return _mm(a, b, _NT, passes)


def _dot_tn(a, b, passes=1):
    return _mm(a, b, _TN, passes)


def _mm_sel(sel, x, dims, x_is_rhs=True):
    hi = x.astype(_BF16)
    rest = x - hi.astype(_F32)
    mid = rest.astype(_BF16)
    lo = (rest - mid.astype(_F32)).astype(_BF16)
    sel = sel.astype(_BF16)
    if x_is_rhs:
        dg = lambda y: lax.dot_general(sel, y, dims, preferred_element_type=_F32)
    else:
        dg = lambda y: lax.dot_general(y, sel, dims, preferred_element_type=_F32)
    return dg(hi) + (dg(mid) + dg(lo))


def _iota(shape, dim):
    return lax.broadcasted_iota(jnp.int32, shape, dim)


def _sigmoid(x):
    return 1.0 / (1.0 + jnp.exp(-x))


def _softplus(x):
    return jnp.maximum(x, 0.0) + jnp.log1p(jnp.exp(-jnp.abs(x)))


def _silu(x):
    return x * _sigmoid(x)


def _params(*sems):
    return pltpu.CompilerParams(dimension_semantics=sems, vmem_limit_bytes=VMEM_LIMIT)


def _ada_kernel(c_ref, w_ref, b_ref, o_ref):
    c = c_ref[...]
    s = _silu(c).astype(_BF16)
    o_ref[0] = _dot(s, w_ref[0].astype(_BF16)) + b_ref[0]


def _ada(c_all, ada_w, ada_b):
    rows = c_all.shape[0]
    tn = 1536
    return pl.pallas_call(
        _ada_kernel,
        grid=(DEPTH, 6 * D_MODEL // tn),
        in_specs=[pl.BlockSpec((rows, D_MODEL), lambda l, j: (0, 0)),
                  pl.BlockSpec((1, D_MODEL, tn), lambda l, j: (l, 0, j)),
                  pl.BlockSpec((1, 1, tn), lambda l, j: (l, 0, j))],
        out_specs=pl.BlockSpec((1, rows, tn), lambda l, j: (l, 0, j)),
        out_shape=jax.ShapeDtypeStruct((DEPTH, rows, 6 * D_MODEL), _F32),
        compiler_params=_params("arbitrary", "arbitrary"),
    )(c_all, ada_w, ada_b.reshape(DEPTH, 1, 6 * D_MODEL))


def _in_proj_kernel(x_ref, sh_ref, sc_ref, g_ref, w_ref, z_ref, xbc_ref, frw_ref, us5_ref, dt_ref):
    x = x_ref[...]
    bb, tl, _ = x.shape
    xn = x * lax.rsqrt(jnp.mean(x * x, axis=-1, keepdims=True) + NORM_EPS)
    h = xn * g_ref[...] * (1.0 + sc_ref[...]) + sh_ref[...]
    h = h.reshape(bb * tl, D_MODEL).astype(_BF16)
    for o_ref, lo, hi in ((z_ref, _PZ, _PXBC), (xbc_ref, _PXBC, _PFRW), (frw_ref, _PFRW, _PUS5),
                          (us5_ref, _PUS5, _PDT), (dt_ref, _PDT, _P_COLS)):
        o_ref[...] = _dot(h, w_ref[:, lo:hi]).reshape(o_ref.shape)


def _in_proj(x, mod, P, l, *, bb, tl, time_major_s5):
    nb, L, _ = x.shape
    grid = (nb // bb, L // tl)
    xmap = lambda i, t: (i, t, 0)

    def modspec(j):
        return pl.BlockSpec((bb, 1, D_MODEL), lambda i, t: (i, 0, j))

    def out(n):
        return jax.ShapeDtypeStruct((nb, L, n), _F32), pl.BlockSpec((bb, tl, n), xmap)

    outs = [out(SSD_WIDTH), out(SSD_CONV_DIM), out(RWKV_PROJ), out(S5_WIDTH), out(DT_PAD)]
    if time_major_s5:
        assert bb == 1
        outs[3] = (jax.ShapeDtypeStruct((L, nb * S5_WIDTH), _F32), pl.BlockSpec((tl, S5_WIDTH), lambda i, t: (t, i)))
    return pl.pallas_call(
        _in_proj_kernel,
        grid=grid,
        in_specs=[pl.BlockSpec((bb, tl, D_MODEL), xmap), modspec(0), modspec(1),
                  pl.BlockSpec((None, 1, D_MODEL), lambda i, t: (l, 0, 0)),
                  pl.BlockSpec((None, D_MODEL, _P_COLS), lambda i, t: (l, 0, 0))],
        out_specs=[o[1] for o in outs],
        out_shape=[o[0] for o in outs],
        compiler_params=_params("arbitrary", "arbitrary"),
    )(x, mod, mod, P['norm1_g'], P['w_in_p'])


def _ssd_seq(j, z_ref, xbc_ref, dt_ref, cw_ref, cb_ref, dtb_ref, alog_ref, dexp_ref, ng_ref, eexp_ref,
             y_ref, sto_ref, cvo_ref, ext_scr, ybuf_scr, T, valid):
    u = xbc_ref[j]
    ext_scr[j, 0:SUBLANES, :] = cvo_ref[j]
    ext_scr[j, SUBLANES:SUBLANES + T, :] = u
    conv = cb_ref[...] + cw_ref[3:4, :] * u
    for back in (1, 2, 3):
        conv = conv + cw_ref[3 - back:4 - back, :] * ext_scr[j, SUBLANES - back:SUBLANES - back + T, :]
    cvo_ref[j] = ext_scr[j, valid:valid + SUBLANES, :]
    xbc = _silu(conv)
    xs = xbc[:, :SSD_WIDTH]
    row = _iota((T, 1), 0)
    dtv = _softplus(dt_ref[j] + dtb_ref[...])
    if valid < T:
        dtv = jnp.where(row < valid, dtv, 0.0)
    a_row = jnp.where(_iota((1, DT_PAD), 1) < SSD_HEADS, -jnp.exp(alog_ref[...]), 0.0)
    a = dtv * a_row
    yield
    tri_b = _iota((T, T), 0) >= _iota((T, T), 1)
    cs = _mm_sel(tri_b, a, _NN)
    eye = _iota((DT_PAD, DT_PAD), 0) == _iota((DT_PAD, DT_PAD), 1)
    xdt = xs * _mm_sel(eexp_ref[...], dtv, _NN, x_is_rhs=False)
    yield
    cs_t = _mm_sel(eye, cs, _NT)
    ecs = jnp.exp(cs)
    cs_last = cs[T - 1:T, :]
    dte = jnp.exp(cs_last - cs)
    elast = jnp.exp(cs_last)
    for g in range(SSD_GROUPS):
        bm = xbc[:, SSD_WIDTH + g * SSD_STATE:SSD_WIDTH + (g + 1) * SSD_STATE]
        cm = xbc[:, SSD_WIDTH + (SSD_GROUPS + g) * SSD_STATE:SSD_WIDTH + (SSD_GROUPS + g + 1) * SSD_STATE]
        cb = _dot_nt(cm, bm)
        for r in range(SSD_HEADS_PER_GROUP):
            yield
            h = g * SSD_HEADS_PER_GROUP + r
            lo, hi = h * SSD_HEAD_DIM, (h + 1) * SSD_HEAD_DIM
            decay = jnp.exp(jnp.where(tri_b, cs[:, h:h + 1] - cs_t[h:h + 1, :], -jnp.inf))
            xdt_h = xdt[:, lo:hi]
            st = sto_ref[j, h]
            y_h = _dot(cb * decay, xdt_h) + ecs[:, h:h + 1] * _dot_nt(cm, st)
            ybuf_scr[j, :, lo:hi] = y_h
            sto_ref[j, h] = elast[:, h:h + 1] * st + _dot_tn(xdt_h * dte[:, h:h + 1], bm)
    yield
    y = ybuf_scr[j] + xs * dexp_ref[...]
    y = y * _silu(z_ref[j])
    y = y * lax.rsqrt(jnp.mean(y * y, axis=-1, keepdims=True) + NORM_EPS) * ng_ref[...]
    y_ref[j] = y


def _ssd_kernel(z_ref, xbc_ref, dt_ref, st_ref, cv_ref, cw_ref, cb_ref, dtb_ref, alog_ref, dexp_ref, ng_ref,
                eexp_ref, *rest, bb, T, valid, n_earlier):
    earlier = rest[:n_earlier]
    y_ref, sto_all, cvo_ref, ext_scr, ybuf_scr = rest[n_earlier:]
    sto_ref = sto_all.at[n_earlier] if n_earlier else sto_all

    @pl.when(pl.program_id(1) == 0)
    def _():
        sto_ref[...] = st_ref[...]
        cvo_ref[...] = cv_ref[...]
        for k, e_ref in enumerate(earlier):
            sto_all[k] = e_ref[...]

    seq = functools.partial(_ssd_seq, z_ref=z_ref, xbc_ref=xbc_ref, dt_ref=dt_ref, cw_ref=cw_ref, cb_ref=cb_ref,
                            dtb_ref=dtb_ref, alog_ref=alog_ref, dexp_ref=dexp_ref, ng_ref=ng_ref, eexp_ref=eexp_ref,
                            y_ref=y_ref, sto_ref=sto_ref, cvo_ref=cvo_ref, ext_scr=ext_scr, ybuf_scr=ybuf_scr,
                            T=T, valid=valid)
    _interleave(seq(j) for j in range(bb))


def _ssd(z, xbc, dt, st_ssd, st_conv8, P, l, *, bb, T, valid, earlier=()):
    nb, L, _ = z.shape
    grid = (nb // bb, L // T)
    tmap = lambda i, c: (i, c, 0)
    smap = lambda i, c: (i, 0, 0, 0)
    cvmap = lambda i, c: (i, 0, 0)
    layer = lambda r, n: pl.BlockSpec((None, r, n), lambda i, c: (l, 0, 0))
    state_blk = (bb, SSD_HEADS, SSD_HEAD_DIM, SSD_STATE)
    n_e = len(earlier)
    if n_e:
        state_out = (pl.BlockSpec((n_e + 1,) + state_blk, lambda i, c: (0, i, 0, 0, 0)),
                     jax.ShapeDtypeStruct((n_e + 1,) + st_ssd.shape[1:], _F32))
    else:
        state_out = (pl.BlockSpec(state_blk, smap), jax.ShapeDtypeStruct(st_ssd.shape[1:], _F32))
    return pl.pallas_call(
        functools.partial(_ssd_kernel, bb=bb, T=T, valid=valid, n_earlier=n_e),
        grid=grid,
        in_specs=[pl.BlockSpec((bb, T, SSD_WIDTH), tmap), pl.BlockSpec((bb, T, SSD_CONV_DIM), tmap),
                  pl.BlockSpec((bb, T, DT_PAD), tmap),
                  pl.BlockSpec((None, bb, SSD_HEADS, SSD_HEAD_DIM, SSD_STATE), lambda i, c: (l, i, 0, 0, 0)),
                  pl.BlockSpec((None, bb, SUBLANES, SSD_CONV_DIM), lambda i, c: (l, i, 0, 0)),
                  layer(SSD_CONV, SSD_CONV_DIM), layer(1, SSD_CONV_DIM), layer(1, DT_PAD), layer(1, DT_PAD),
                  layer(1, SSD_WIDTH), layer(1, SSD_WIDTH),
                  pl.BlockSpec((DT_PAD, SSD_WIDTH), lambda i, c: (0, 0))] + [pl.BlockSpec(state_blk, smap)] * n_e,
        out_specs=[pl.BlockSpec((bb, T, SSD_WIDTH), tmap), state_out[0],
                   pl.BlockSpec((bb, SUBLANES, SSD_CONV_DIM), cvmap)],
        out_shape=[jax.ShapeDtypeStruct((nb, L, SSD_WIDTH), _F32), state_out[1],
                   jax.ShapeDtypeStruct(st_conv8.shape[1:], _F32)],
        scratch_shapes=[pltpu.VMEM((bb, T + SUBLANES, SSD_CONV_DIM), _F32), pltpu.VMEM((bb, T, SSD_WIDTH), _F32)],
        compiler_params=_params("arbitrary", "arbitrary"),
    )(z, xbc, dt, st_ssd, st_conv8, P['ssd_conv_w'], P['ssd_conv_b'], P['ssd_dt_bias'], P['ssd_a_log'],
      P['ssd_d'], P['ssd_norm_g'], P['ssd_head_expand'], *earlier)


def _stack_heads(x):
    return jnp.concatenate([x[:, h * RWKV_HEAD_DIM:(h + 1) * RWKV_HEAD_DIM] for h in range(RWKV_HEADS)], axis=0)


def _unstack_heads(x, T):
    return jnp.concatenate([x[h * T:(h + 1) * T, :] for h in range(RWKV_HEADS)], axis=1)


def _interleave(gens):
    gens = list(gens)
    while gens:
        alive = []
        for g in gens:
            try:
                next(g)
                alive.append(g)
            except StopIteration:
                pass
        gens = alive


def _rwkv_masks(T):
    HT = RWKV_HEADS * T
    rr = _iota((HT, HT), 0)
    cc = _iota((HT, HT), 1)
    same = (rr // T) == (cc // T)
    rt, ct = _iota((T, T), 0), _iota((T, T), 1)
    head_of_row = _iota((HT, RWKV_WIDTH), 0) // T
    head_of_col = _iota((HT, RWKV_WIDTH), 1) // RWKV_HEAD_DIM
    return dict(strict=jnp.logical_and(same, rr > cc), incl=jnp.logical_and(same, rr >= cc),
                eye=(rr == cc).astype(_F32), hmask=head_of_row == head_of_col,
                strict_t=rt > ct, incl_t=rt >= ct, eye_t=(rt == ct).astype(_F32), tri=(rt >= ct).astype(_BF16))


def _rwkv_inverse(eye, mats, T):
    inv = [eye + n for n in mats]
    p = list(mats)
    for _ in range(max((T - 1).bit_length() - 1, 0)):
        yield
        p = [_dot(x, x, P_INV) for x in p]
        yield
        inv = [i + _dot(i, x, P_INV) for i, x in zip(inv, p)]
    return inv


def _rwkv_core_per_head(j, masks, so_ref, T, at, rt, bt, kt, v4f, w_end):
    cast = lambda x: x.astype(_BF16)
    a4, r4, b4, k4 = (cast(_stack_heads(t)) for t in (at, rt, bt, kt))
    v4 = cast(v4f)
    bw4, kw4 = (cast(_stack_heads(t * w_end)) for t in (bt, kt))
    heads = range(RWKV_HEADS)
    blk = lambda x, h: x[h * T:(h + 1) * T]
    yield
    n_ab = [jnp.where(masks['strict_t'], _dot_nt(blk(a4, h), blk(b4, h), P_AMAT), 0.0) for h in heads]
    n_ak = [cast(jnp.where(masks['strict_t'], _dot_nt(blk(a4, h), blk(k4, h), P_AMAT), 0.0)) for h in heads]
    m_rb = [cast(jnp.where(masks['incl_t'], _dot_nt(blk(r4, h), blk(b4, h), P_AMAT), 0.0)) for h in heads]
    m_rk = [cast(jnp.where(masks['incl_t'], _dot_nt(blk(r4, h), blk(k4, h), P_AMAT), 0.0)) for h in heads]
    inv = yield from _rwkv_inverse(masks['eye_t'], n_ab, T)
    yield
    s0 = [so_ref[j, h] for h in heads]
    rhs = [_dot_nt(blk(a4, h), s0[h], P_S0) + _dot(n_ak[h], blk(v4, h), P_INTRA) for h in heads]
    o_own = [_dot_nt(blk(r4, h), s0[h], P_S0) + _dot(m_rk[h], blk(v4, h), P_INTRA) for h in heads]
    yield
    u = [cast(_dot(inv[h], rhs[h], P_SOLVE)) for h in heads]
    yield
    o4 = jnp.concatenate([o_own[h] + _dot(m_rb[h], u[h], P_INTRA) for h in heads], axis=0)
    for h in heads:
        so_ref[j, h] = (s0[h] * w_end[:, h * RWKV_HEAD_DIM:(h + 1) * RWKV_HEAD_DIM]
                        + _dot_tn(u[h], blk(bw4, h), P_STATE) + _dot_tn(blk(v4, h), blk(kw4, h), P_STATE))
    yield
    return o4


def _rwkv_core_stacked(j, masks, so_ref, T, at, rt, bt, kt, v4f, w_end):
    strict, incl, hmask = masks['strict'], masks['incl'], masks['hmask']
    a4, r4, b4, k4 = (_stack_heads(t) for t in (at, rt, bt, kt))
    yield
    n_ab = jnp.where(strict, _dot_nt(a4, b4, P_AMAT), 0.0)
    n_ak = jnp.where(strict, _dot_nt(a4, k4, P_AMAT), 0.0)
    m_rb = jnp.where(incl, _dot_nt(r4, b4, P_AMAT), 0.0)
    m_rk = jnp.where(incl, _dot_nt(r4, k4, P_AMAT), 0.0)
    (inv,) = yield from _rwkv_inverse(masks['eye'], [n_ab], T)

    def wide(x):
        return jnp.where(hmask, jnp.concatenate([x] * RWKV_HEADS, axis=0), 0.0)

    yield
    s0 = jnp.concatenate([so_ref[j, h] for h in range(RWKV_HEADS)], axis=1)
    rhs = _dot_nt(wide(at), s0, P_S0) + _dot(n_ak, v4f, P_INTRA)
    o4 = _dot_nt(wide(rt), s0, P_S0) + _dot(m_rk, v4f, P_INTRA)
    yield
    u4 = _dot(inv, rhs, P_SOLVE)
    yield
    o4 = o4 + _dot(m_rb, u4, P_INTRA)
    s1 = (s0 * w_end + _dot_tn(u4, wide(bt * w_end), P_STATE) + _dot_tn(v4f, wide(kt * w_end), P_STATE))
    for h in range(RWKV_HEADS):
        so_ref[j, h] = s1[:, h * RWKV_HEAD_DIM:(h + 1) * RWKV_HEAD_DIM]
    yield
    return o4


def _rwkv_seq(j, masks, f_ref, mu_ref, w0_ref, w2_ref, a0_ref, a2_ref, g2_ref, kk_ref, ka_ref, rk_ref, lng_ref,
              lnb_ref, y_ref, so_ref, sho_ref, ext_scr, T, valid):
    f = f_ref[j]
    ext_scr[j, 0:SUBLANES, :] = sho_ref[j]
    ext_scr[j, SUBLANES:SUBLANES + T, :] = f
    prev = ext_scr[j, SUBLANES - 1:SUBLANES - 1 + T, :]
    sho_ref[j] = ext_scr[j, valid:valid + SUBLANES, :]
    fm = f + (prev - f) * mu_ref[...]
    r = fm[:, 0:256]
    k = fm[:, 256:512]
    v = fm[:, 512:768]
    lora = fm[:, 768:896]
    gl = fm[:, 896:1024]
    w_log = -_softplus(-(w0_ref[...] + _dot(jnp.tanh(lora), w2_ref[...]))) - 0.5
    logw = -jnp.exp(w_log)
    a = _sigmoid(a0_ref[...] + _dot(lora, a2_ref[...]))
    g = _dot(_sigmoid(gl), g2_ref[...])
    k2 = k * (1.0 + (a - 1.0) * ka_ref[...])
    kk4 = _stack_heads(k * kk_ref[...])
    kk4 = kk4 * lax.rsqrt(jnp.maximum(jnp.sum(kk4 * kk4, axis=-1, keepdims=True), 1e-24))
    kkn = _unstack_heads(kk4, T)
    if valid < T:
        live = _iota((T, 1), 0) < valid
        logw = jnp.where(live, logw, 0.0)
        kkn = jnp.where(live, kkn, 0.0)
        k2 = jnp.where(live, k2, 0.0)
    yield
    cw = _mm_sel(masks['tri'], logw, _NN)
    w_inc = jnp.exp(cw)
    w_exc = jnp.exp(cw - logw)
    w_inv = jnp.exp(-cw)
    w_end = w_inc[T - 1:T, :]
    at = -kkn * w_exc
    rt = r * w_inc
    bt = kkn * a * w_inv
    kt = k2 * w_inv
    v4f = _stack_heads(v)
    core = _rwkv_core_per_head if T % 16 == 0 else _rwkv_core_stacked
    o4 = yield from core(j, masks, so_ref, T, at, rt, bt, kt, v4f, w_end)

    mean = jnp.mean(o4, axis=-1, keepdims=True)
    var = jnp.mean(jnp.square(o4 - mean), axis=-1, keepdims=True)
    o4 = (o4 - mean) * lax.rsqrt(var + RWKV_LN_EPS)
    bonus4 = jnp.sum(_stack_heads(r * k2 * rk_ref[...]), axis=-1, keepdims=True) * v4f
    o = _unstack_heads(o4, T) * lng_ref[...] + lnb_ref[...] + _unstack_heads(bonus4, T)
    y_ref[j] = o * g


def _rwkv_kernel(f_ref, s_ref, sh_ref, mu_ref, w0_ref, w2_ref, a0_ref, a2_ref, g2_ref, kk_ref, ka_ref, rk_ref,
                 lng_ref, lnb_ref, *rest, bb, T, valid, n_earlier):
    earlier = rest[:n_earlier]
    y_ref, so_all, sho_ref, ext_scr = rest[n_earlier:]
    so_ref = so_all.at[n_earlier] if n_earlier else so_all

    @pl.when(pl.program_id(1) == 0)
    def _():
        so_ref[...] = s_ref[...]
        sho_ref[...] = sh_ref[...]
        for k, e_ref in enumerate(earlier):
            so_all[k] = e_ref[...]

    seq = functools.partial(_rwkv_seq, f_ref=f_ref, mu_ref=mu_ref, w0_ref=w0_ref, w2_ref=w2_ref, a0_ref=a0_ref,
                            a2_ref=a2_ref, g2_ref=g2_ref, kk_ref=kk_ref, ka_ref=ka_ref, rk_ref=rk_ref,
                            lng_ref=lng_ref, lnb_ref=lnb_ref, y_ref=y_ref, so_ref=so_ref, sho_ref=sho_ref,
                            ext_scr=ext_scr, T=T, valid=valid)
    masks = _rwkv_masks(T)
    _interleave(seq(j, masks) for j in range(bb))


def _rwkv(f, st_rwkv, shift8, P, l, *, bb, T, valid, earlier=()):
    nb, L, _ = f.shape
    hd = RWKV_HEAD_DIM
    state_blk = (bb, RWKV_HEADS, hd, hd)
    smap = lambda i, c: (i, 0, 0, 0)
    n_e = len(earlier)
    if n_e:
        state_out = (pl.BlockSpec((n_e + 1,) + state_blk, lambda i, c: (0, i, 0, 0, 0)),
                     jax.ShapeDtypeStruct((n_e + 1,) + st_rwkv.shape[1:], _F32))
    else:
        state_out = (pl.BlockSpec(state_blk, smap), jax.ShapeDtypeStruct(st_rwkv.shape[1:], _F32))
    grid = (nb // bb, L // T)
    tmap = lambda i, c: (i, c, 0)
    bmap = lambda i, c: (i, 0, 0)
    layer = lambda r, n: pl.BlockSpec((None, r, n), lambda i, c: (l, 0, 0))
    row = lambda n: layer(1, n)
    lora_w = RWKV_DECAY_LORA + RWKV_A_LORA
    return pl.pallas_call(
        functools.partial(_rwkv_kernel, bb=bb, T=T, valid=valid, n_earlier=n_e),
        grid=grid,
        in_specs=[pl.BlockSpec((bb, T, RWKV_PROJ), tmap),
                  pl.BlockSpec((None, bb, RWKV_HEADS, hd, hd), lambda i, c: (l, i, 0, 0, 0)),
                  pl.BlockSpec((None, bb, SUBLANES, RWKV_PROJ), lambda i, c: (l, i, 0, 0)),
                  row(RWKV_PROJ), row(RWKV_WIDTH), layer(lora_w, RWKV_WIDTH),
                  row(RWKV_WIDTH), layer(lora_w, RWKV_WIDTH), layer(RWKV_GATE_LORA, RWKV_WIDTH),
                  row(RWKV_WIDTH), row(RWKV_WIDTH), row(RWKV_WIDTH), row(RWKV_WIDTH), row(RWKV_WIDTH)]
        + [pl.BlockSpec(state_blk, smap)] * n_e,
        out_specs=[pl.BlockSpec((bb, T, RWKV_WIDTH), tmap), state_out[0],
                   pl.BlockSpec((bb, SUBLANES, RWKV_PROJ), bmap)],
        out_shape=[jax.ShapeDtypeStruct((nb, L, RWKV_WIDTH), _F32), state_out[1],
                   jax.ShapeDtypeStruct(shift8.shape[1:], _F32)],
        scratch_shapes=[pltpu.VMEM((bb, T + SUBLANES, RWKV_PROJ), _F32)],
        compiler_params=_params("arbitrary", "arbitrary"),
    )(f, st_rwkv, shift8, P['rwkv_mu'], P['rwkv_w0'], P['rwkv_w2'], P['rwkv_a0'], P['rwkv_a2'], P['rwkv_g2'],
      P['rwkv_k_k'], P['rwkv_k_a'], P['rwkv_r_k'], P['rwkv_ln_g'], P['rwkv_ln_b'], *earlier)


def _s5_kernel(u_ref, sre_ref, sim_ref, lre_ref, lim_ref, dt_ref, bre_ref, bim_ref, cre_ref, cim_ref, d_ref,
               gw_ref, gb_ref, y_ref, hre_ref, him_ref, hre_scr, him_scr, *, nb, tl, final_step):
    @pl.when(pl.program_id(0) == 0)
    def _():
        hre_ref[...] = sre_ref[...]
        him_ref[...] = sim_ref[...]

    lre, lim, dt = lre_ref[...], lim_ref[...], jnp.exp(dt_ref[...])
    mag = jnp.exp(lre * dt)
    ang = lim * dt
    ab_re, ab_im = mag * jnp.cos(ang), mag * jnp.sin(ang)
    den = jnp.square(lre) + jnp.square(lim)
    q_re = ((ab_re - 1.0) * lre + ab_im * lim) / den
    q_im = (ab_im * lre - (ab_re - 1.0) * lim) / den
    b_re, b_im = bre_ref[...], bim_ref[...]
    bb_re = q_re * b_re - q_im * b_im
    bb_im = q_re * b_im + q_im * b_re
    u = u_ref[...]
    hre_scr[0:nb, :] = hre_ref[...]
    him_scr[0:nb, :] = him_ref[...]
    hre_scr[nb:, :] = _dot(u, bb_re)
    him_scr[nb:, :] = _dot(u, bb_im)
    ar = jnp.broadcast_to(ab_re, (nb, S5_FLAT))
    ai = jnp.broadcast_to(ab_im, (nb, S5_FLAT))

    if nb <= SUBLANES:
        def step(s, carry):
            pr, pi = carry
            cur = pl.ds(pl.multiple_of((s + 1) * nb, nb), nb)
            nr = ar * pr - ai * pi + hre_scr[cur, :]
            ni = ar * pi + ai * pr + him_scr[cur, :]
            hre_scr[cur, :] = nr
            him_scr[cur, :] = ni
            return nr, ni

        lax.fori_loop(0, tl, step, (hre_ref[...], him_ref[...]), unroll=2)
    else:
        def step(s, carry):
            prev = pl.ds(pl.multiple_of(s * nb, nb), nb)
            cur = pl.ds(pl.multiple_of((s + 1) * nb, nb), nb)
            pr, pi = hre_scr[prev, :], him_scr[prev, :]
            hre_scr[cur, :] = ar * pr - ai * pi + hre_scr[cur, :]
            him_scr[cur, :] = ar * pi + ai * pr + him_scr[cur, :]
            return carry

        lax.fori_loop(0, tl, step, 0)
    hre_ref[...] = hre_scr[(final_step + 1) * nb:(final_step + 2) * nb, :]
    him_ref[...] = him_scr[(final_step + 1) * nb:(final_step + 2) * nb, :]
    y = _dot(hre_scr[nb:, :], cre_ref[...]) - _dot(him_scr[nb:, :], cim_ref[...]) + d_ref[...] * u
    y = 0.5 * y * (1.0 + jnp.tanh(0.7978845608028654 * (y + 0.044715 * (y * y * y))))
    y_ref[...] = y * _sigmoid(_dot(y, gw_ref[...]) + gb_ref[...])


def _s5(u_tm, st_re, st_im, P, l, *, nb, tl, final_step):
    rows = u_tm.shape[0]
    L = rows // nb
    layer = lambda r, n: pl.BlockSpec((None, r, n), lambda t: (l, 0, 0))
    row = lambda n: layer(1, n)
    full = lambda a, b: pl.BlockSpec((a, b), lambda t: (0, 0))
    state = pl.BlockSpec((None, nb, S5_FLAT), lambda t: (l, 0, 0))
    return pl.pallas_call(
        functools.partial(_s5_kernel, nb=nb, tl=tl, final_step=final_step),
        grid=(L // tl,),
        in_specs=[pl.BlockSpec((tl * nb, S5_WIDTH), lambda t: (t, 0)), state, state,
                  row(S5_FLAT), row(S5_FLAT), row(S5_FLAT),
                  layer(S5_WIDTH, S5_FLAT), layer(S5_WIDTH, S5_FLAT), layer(S5_FLAT, S5_WIDTH),
                  layer(S5_FLAT, S5_WIDTH), row(S5_WIDTH), layer(S5_WIDTH, S5_WIDTH), row(S5_WIDTH)],
        out_specs=[pl.BlockSpec((tl * nb, S5_WIDTH), lambda t: (t, 0)), full(nb, S5_FLAT), full(nb, S5_FLAT)],
        out_shape=[jax.ShapeDtypeStruct((rows, S5_WIDTH), _F32),
                   jax.ShapeDtypeStruct((nb, S5_FLAT), _F32), jax.ShapeDtypeStruct((nb, S5_FLAT), _F32)],
        scratch_shapes=[pltpu.VMEM(((tl + 1) * nb, S5_FLAT), _F32), pltpu.VMEM(((tl + 1) * nb, S5_FLAT), _F32)],
        compiler_params=_params("arbitrary"),
    )(u_tm, st_re, st_im, P['s5_a_re'], P['s5_a_im'], P['s5_log_dt'], P['s5_b_re'], P['s5_b_im'], P['s5_c_re'],
      P['s5_c_im'], P['s5_d'], P['s5_glu_w'], P['s5_glu_b'])


def _out_mlp_kernel(x_ref, yssd_ref, yrw_ref, ys5_ref, g1_ref, sh2_ref, sc2_ref, g2_ref, n2g_ref, fg_ref,
                    wo_ref, w1_ref, w2_ref, o_ref, hid_scr, *, final_norm):
    bb, tl, _ = x_ref.shape
    rows = bb * tl
    flat = lambda ref: ref[...].reshape(rows, ref.shape[-1]).astype(_BF16)
    mix = (_dot(flat(yssd_ref), wo_ref[0:512, :]) + _dot(flat(yrw_ref), wo_ref[512:768, :])
           + _dot(flat(ys5_ref), wo_ref[768:1024, :]))
    x1 = x_ref[...] + g1_ref[...] * mix.reshape(bb, tl, D_MODEL)
    xn = x1 * lax.rsqrt(jnp.mean(x1 * x1, axis=-1, keepdims=True) + NORM_EPS)
    h2 = (xn * n2g_ref[...] * (1.0 + sc2_ref[...]) + sh2_ref[...]).reshape(rows, D_MODEL).astype(_BF16)
    for c in range(D_FF // FF_TILE):
        cols = slice(c * FF_TILE, (c + 1) * FF_TILE)
        hid_scr[:, cols] = jnp.square(jnp.maximum(_dot(h2, w1_ref[:, cols]), 0.0)).astype(_BF16)
    ff = _dot(hid_scr[...], w2_ref[...])
    out = x1 + g2_ref[...] * ff.reshape(bb, tl, D_MODEL)
    if final_norm:
        out = out * lax.rsqrt(jnp.mean(out * out, axis=-1, keepdims=True) + NORM_EPS) * fg_ref[...]
    o_ref[...] = out


def _out_mlp(x, y_ssd, y_rw, y_s5, mod, P, l, *, bb, tl, time_major_s5, final_norm):
    nb, L, _ = x.shape
    grid = (nb // bb, L // tl)
    xmap = lambda i, t: (i, t, 0)
    cmap = lambda i, t: (0, 0)
    wmap = lambda i, t: (l, 0, 0)
    once = pl.Buffered(1)

    def modspec(j):
        return pl.BlockSpec((bb, 1, D_MODEL), lambda i, t: (i, 0, j))

    s5_spec = (pl.BlockSpec((tl, S5_WIDTH), lambda i, t: (t, i)) if time_major_s5
               else pl.BlockSpec((bb, tl, S5_WIDTH), xmap))
    return pl.pallas_call(
        functools.partial(_out_mlp_kernel, final_norm=final_norm),
        grid=grid,
        in_specs=[pl.BlockSpec((bb, tl, D_MODEL), xmap), pl.BlockSpec((bb, tl, SSD_WIDTH), xmap),
                  pl.BlockSpec((bb, tl, RWKV_WIDTH), xmap), s5_spec,
                  modspec(2), modspec(3), modspec(4), modspec(5),
                  pl.BlockSpec((None, 1, D_MODEL), wmap), pl.BlockSpec((1, D_MODEL), cmap),
                  pl.BlockSpec((None, D_MODEL, D_MODEL), wmap, pipeline_mode=once),
                  pl.BlockSpec((None, D_MODEL, D_FF), wmap, pipeline_mode=once),
                  pl.BlockSpec((None, D_FF, D_MODEL), wmap, pipeline_mode=once)],
        out_specs=pl.BlockSpec((bb, tl, D_MODEL), xmap),
        out_shape=jax.ShapeDtypeStruct(x.shape, _F32),
        scratch_shapes=[pltpu.VMEM((bb * tl, D_FF), _BF16)],
        compiler_params=_params("arbitrary", "arbitrary"),
    )(x, y_ssd, y_rw, y_s5, mod, mod, mod, mod, P['norm2_g'], P['final_g'], P['w_out'], P['mlp_w1'], P['mlp_w2'])


def _trunk(x, mod, states, P, valid, *, prompt):
    st_ssd, st_conv, st_rwkv, st_shift, st_re, st_im = states
    nb, L, _ = x.shape
    if prompt:
        bb, tl, mlp_tl = 1, min(ROW_TILE, L), min(MLP_ROW_TILE, L)
        ssd_t, rwkv_t, s5_tl = min(SSD_CHUNK, L), min(RWKV_CHUNK, L), min(S5_TILE, L)
        ssd_bb, rwkv_bb = min(nb, SSD_PROMPT_SEQS), min(nb, RWKV_PROMPT_SEQS)
    else:
        bb, tl, mlp_tl = min(nb, ROW_TILE // L), L, L
        ssd_t = rwkv_t = s5_tl = L
        ssd_bb = rwkv_bb = min(nb, SAMPLE_SEQS)
    new = ([], [], [], [], [], [])
    for l in range(DEPTH):
        z, xbc, frw, us5, dt = _in_proj(x, mod[l], P, l, bb=bb, tl=tl, time_major_s5=prompt)
        v = valid
        last = l == DEPTH - 1
        y_ssd, n_ssd, n_conv = _ssd(z, xbc, dt, st_ssd, st_conv, P, l, bb=ssd_bb, T=ssd_t,
                                    valid=ssd_t if prompt else v, earlier=tuple(new[0]) if last else ())
        y_rw, n_rwkv, n_shift = _rwkv(frw, st_rwkv, st_shift, P, l, bb=rwkv_bb, T=rwkv_t,
                                      valid=rwkv_t if prompt else v, earlier=tuple(new[2]) if last else ())
        if prompt:
            u_tm = us5.reshape(L * nb, S5_WIDTH)
        else:
            u_tm = jnp.transpose(us5, (1, 0, 2)).reshape(L * nb, S5_WIDTH)
        y_s5, n_re, n_im = _s5(u_tm, st_re, st_im, P, l, nb=nb, tl=s5_tl,
                               final_step=(s5_tl - 1) if prompt else (v - 1))
        if prompt:
            y_s5 = y_s5.reshape(L, nb * S5_WIDTH)
        else:
            y_s5 = jnp.transpose(y_s5.reshape(L, nb, S5_WIDTH), (1, 0, 2))
        x = _out_mlp(x, y_ssd, y_rw, y_s5, mod[l], P, l, bb=bb, tl=mlp_tl, time_major_s5=prompt,
                     final_norm=(l == DEPTH - 1))
        for lst, s in zip(new, (n_ssd, n_conv, n_rwkv, n_shift, n_re, n_im)):
            lst.append(s)
    by_kernel = (0, 2) if DEPTH > 1 else ()
    return x, [lst[-1] if k in by_kernel else jnp.stack(lst) for k, lst in enumerate(new)]


def _pack_states(st_ssd, st_conv, st_rwkv, st_shift, st_re, st_im):
    d, nb = st_ssd.shape[:2]
    conv8 = jnp.pad(st_conv, ((0, 0), (0, 0), (SUBLANES - (SSD_CONV - 1), 0), (0, 0)))
    shift8 = jnp.pad(st_shift[:, :, None, :], ((0, 0), (0, 0), (SUBLANES - 1, 0), (0, 0)))
    return (st_ssd, conv8, st_rwkv, shift8, st_re.reshape(d, nb, S5_FLAT), st_im.reshape(d, nb, S5_FLAT))


def _unpack_states(n_ssd, n_conv8, n_rwkv, n_shift8, n_re, n_im):
    d, nb = n_ssd.shape[:2]
    return (n_ssd, n_conv8[:, :, SUBLANES - (SSD_CONV - 1):, :], n_rwkv, n_shift8[:, :, SUBLANES - 1, :],
            n_re.reshape(d, nb, S5_GROUPS, S5_STATE), n_im.reshape(d, nb, S5_GROUPS, S5_STATE))


def _prepare_params(norm1_g, norm2_g, w_in, ssd_conv_w, ssd_conv_b, ssd_dt_bias, ssd_a_log, ssd_d, ssd_norm_g, rwkv_mu,
                    rwkv_w0, rwkv_w2, rwkv_a0, rwkv_a2, rwkv_g2, rwkv_k_k, rwkv_k_a, rwkv_r_k, rwkv_ln_g, rwkv_ln_b,
                    s5_a_re, s5_a_im, s5_log_dt, s5_b_re, s5_b_im, s5_c_re, s5_c_im, s5_d, s5_glu_w, s5_glu_b, w_out,
                    mlp_w1, mlp_w2, final_g):
    rows = lambda a: a.reshape(DEPTH, 1, -1)
    head_pad = ((0, 0), (0, DT_PAD - SSD_HEADS))
    eye = jnp.eye(S5_GROUPS, dtype=_F32)
    zeros_lora = jnp.zeros((DEPTH, RWKV_A_LORA, RWKV_WIDTH), _F32)
    w_in_p = jnp.concatenate(
        [w_in[:, :, _Z0:_DT0], w_in[:, :, _FRW0:_IN_COLS], w_in[:, :, _DT0:_FRW0],
         jnp.zeros((DEPTH, D_MODEL, DT_PAD - SSD_HEADS), w_in.dtype)], axis=-1).astype(_BF16)
    return dict(
        norm1_g=rows(norm1_g), norm2_g=rows(norm2_g), final_g=final_g.reshape(1, D_MODEL), w_in_p=w_in_p,
        w_out=w_out.astype(_BF16), mlp_w1=mlp_w1.astype(_BF16), mlp_w2=mlp_w2.astype(_BF16),
        ssd_conv_w=ssd_conv_w, ssd_conv_b=rows(ssd_conv_b), ssd_dt_bias=rows(jnp.pad(ssd_dt_bias, head_pad)),
        ssd_a_log=rows(jnp.pad(ssd_a_log, head_pad)), ssd_d=rows(jnp.repeat(ssd_d, SSD_HEAD_DIM, axis=-1)),
        ssd_norm_g=rows(ssd_norm_g),
        ssd_head_expand=jnp.repeat(jnp.eye(DT_PAD, SSD_HEADS, dtype=_F32), SSD_HEAD_DIM, axis=1),
        rwkv_mu=rows(rwkv_mu), rwkv_w0=rows(rwkv_w0), rwkv_a0=rows(rwkv_a0), rwkv_k_k=rows(rwkv_k_k),
        rwkv_k_a=rows(rwkv_k_a), rwkv_r_k=rows(rwkv_r_k), rwkv_ln_g=rows(rwkv_ln_g), rwkv_ln_b=rows(rwkv_ln_b),
        rwkv_w2=jnp.concatenate([rwkv_w2, zeros_lora], axis=1),
        rwkv_a2=jnp.concatenate([zeros_lora, rwkv_a2], axis=1),
        rwkv_g2=rwkv_g2,
        s5_a_re=rows(s5_a_re), s5_a_im=rows(s5_a_im), s5_log_dt=rows(jnp.repeat(s5_log_dt, S5_STATE, axis=-1)),
        s5_b_re=jnp.einsum('lgpc,gh->lgchp', s5_b_re, eye).reshape(DEPTH, S5_WIDTH, S5_FLAT),
        s5_b_im=jnp.einsum('lgpc,gh->lgchp', s5_b_im, eye).reshape(DEPTH, S5_WIDTH, S5_FLAT),
        s5_c_re=jnp.einsum('lgcp,gh->lgphc', s5_c_re, eye).reshape(DEPTH, S5_FLAT, S5_WIDTH),
        s5_c_im=jnp.einsum('lgcp,gh->lgphc', s5_c_im, eye).reshape(DEPTH, S5_FLAT, S5_WIDTH),
        s5_d=rows(s5_d), s5_glu_w=s5_glu_w, s5_glu_b=rows(s5_glu_b))


def kernel(x_prompt, x_sample, c_prompt, c_sample, state_ssd, state_ssd_conv, state_rwkv, state_rwkv_shift,
           state_s5_re, state_s5_im, ada_w, ada_b, norm1_g, norm2_g, w_in, ssd_conv_w, ssd_conv_b, ssd_dt_bias,
           ssd_a_log, ssd_d, ssd_norm_g, rwkv_mu, rwkv_w0, rwkv_w2, rwkv_a0, rwkv_a2, rwkv_g2, rwkv_k_k, rwkv_k_a,
           rwkv_r_k, rwkv_ln_g, rwkv_ln_b, s5_a_re, s5_a_im, s5_log_dt, s5_b_re, s5_b_im, s5_c_re, s5_c_im, s5_d,
           s5_glu_w, s5_glu_b, w_out, mlp_w1, mlp_w2, final_g):
    bp, lp, _ = x_prompt.shape
    bs, ls, _ = x_sample.shape
    ls_pad = -(-ls // SUBLANES) * SUBLANES
    P = _prepare_params(norm1_g, norm2_g, w_in, ssd_conv_w, ssd_conv_b, ssd_dt_bias, ssd_a_log, ssd_d, ssd_norm_g,
                        rwkv_mu, rwkv_w0, rwkv_w2, rwkv_a0, rwkv_a2, rwkv_g2, rwkv_k_k, rwkv_k_a, rwkv_r_k, rwkv_ln_g,
                        rwkv_ln_b, s5_a_re, s5_a_im, s5_log_dt, s5_b_re, s5_b_im, s5_c_re, s5_c_im, s5_d, s5_glu_w,
                        s5_glu_b, w_out, mlp_w1, mlp_w2, final_g)
    mod = _ada(jnp.concatenate([c_prompt, c_sample], axis=0), ada_w, ada_b)
    mod_p = mod[:, :bp].reshape(DEPTH, bp, 1, 6 * D_MODEL)
    mod_s = mod[:, bp:].reshape(DEPTH, bs, 1, 6 * D_MODEL)

    zeros = lambda shape: jnp.zeros((DEPTH, bp) + shape, _F32)
    p_states = _pack_states(zeros((SSD_HEADS, SSD_HEAD_DIM, SSD_STATE)), zeros((SSD_CONV - 1, SSD_CONV_DIM)),
                            zeros((RWKV_HEADS, RWKV_HEAD_DIM, RWKV_HEAD_DIM)), zeros((RWKV_PROJ,)),
                            zeros((S5_GROUPS, S5_STATE)), zeros((S5_GROUPS, S5_STATE)))
    y_prompt, sp = _trunk(x_prompt, mod_p, p_states, P, ls, prompt=True)

    s_states = _pack_states(state_ssd, state_ssd_conv, state_rwkv, state_rwkv_shift, state_s5_re, state_s5_im)
    x_s = jnp.pad(x_sample, ((0, 0), (0, ls_pad - ls), (0, 0)))
    y_sample, ss = _trunk(x_s, mod_s, s_states, P, ls, prompt=False)
    return (y_prompt, y_sample[:, :ls]) + _unpack_states(*sp) + _unpack_states(*ss)
```

```python
import functools

import jax
import jax.numpy as jnp
from jax import lax
from jax.experimental import pallas as pl
from jax.experimental.pallas import tpu as pltpu

D_MODEL = 1024
DEPTH = 2
SSD_WIDTH = 512
SSD_HEAD_DIM = 64
SSD_HEADS = 8
SSD_GROUPS = 2
SSD_HEADS_PER_GROUP = SSD_HEADS // SSD_GROUPS
SSD_STATE = 128
SSD_CONV = 4
SSD_CONV_DIM = SSD_WIDTH + 2 * SSD_GROUPS * SSD_STATE
RWKV_WIDTH = 256
RWKV_HEAD_DIM = 64
RWKV_HEADS = 4
RWKV_DECAY_LORA = 64
RWKV_A_LORA = 64
RWKV_GATE_LORA = 128
RWKV_PROJ = 3 * RWKV_WIDTH + RWKV_DECAY_LORA + RWKV_A_LORA + RWKV_GATE_LORA
RWKV_LN_EPS = 64e-5
S5_WIDTH = 256
S5_GROUP_CH = 16
S5_GROUPS = 16
S5_STATE = 64
S5_FLAT = S5_GROUPS * S5_STATE
D_FF = 4 * D_MODEL
NORM_EPS = 1e-6
DT_PAD = 128
SUBLANES = 8

_Z0, _XBC0, _DT0, _FRW0, _US50 = 0, SSD_WIDTH, SSD_WIDTH + SSD_CONV_DIM, SSD_WIDTH + SSD_CONV_DIM + SSD_HEADS, \
    SSD_WIDTH + SSD_CONV_DIM + SSD_HEADS + RWKV_PROJ
_IN_COLS = _US50 + S5_WIDTH
_PZ, _PXBC, _PFRW, _PUS5, _PDT = 0, 512, 1536, 2560, 2816
_P_COLS = _PDT + DT_PAD

SSD_CHUNK = 128
RWKV_CHUNK = 64
SSD_PROMPT_SEQS = 2
RWKV_PROMPT_SEQS = 8
SAMPLE_SEQS = 8
S5_TILE = 256
ROW_TILE = 512
IN_ROW_TILE = 1024
MLP_ROW_TILE = 512
FF_TILE = 1024
VMEM_LIMIT = 56 * 1024 * 1024

_HI = lax.Precision.HIGHEST
P_AMAT = 1
P_INV = 1
P_S0 = 1
P_INTRA = 1
P_SOLVE = 1
P_STATE = 1
_F32 = jnp.float32
_BF16 = jnp.bfloat16


_NN = (((1,), (0,)), ((), ()))
_NT = (((1,), (1,)), ((), ()))
_TN = (((0,), (0,)), ((), ()))


def _split(x):
    hi = x.astype(_BF16)
    return hi, (x - hi.astype(_F32)).astype(_BF16)


def _mm(a, b, dims, passes):
    dg = lambda x, y, prec=None: lax.dot_general(x, y, dims, preferred_element_type=_F32, precision=prec)
    if passes == 6:
        return dg(a.astype(_F32), b.astype(_F32), _HI)
    if passes == 3:
        ah, al = _split(a)
        bh, bl = _split(b)
        return dg(ah, bh) + (dg(ah, bl) + dg(al, bh))
    return dg(a.astype(_BF16), b.astype(_BF16))


def _dot(a, b, passes=1):
    return _mm(a, b, _NN, passes)


def _dot_nt(a, b, passes=1):
    return _mm(a, b, _NT, passes)


def _dot_tn(a, b, passes=1):
    return _mm(a, b, _TN, passes)


def _mm_sel(sel, x, dims, x_is_rhs=True):
    hi = x.astype(_BF16)
    rest = x - hi.astype(_F32)
    mid = rest.astype(_BF16)
    lo = (rest - mid.astype(_F32)).astype(_BF16)
    sel = sel.astype(_BF16)
    if x_is_rhs:
        dg = lambda y: lax.dot_general(sel, y, dims, preferred_element_type=_F32)
    else:
        dg = lambda y: lax.dot_general(y, sel, dims, preferred_element_type=_F32)
    return dg(hi) + (dg(mid) + dg(lo))


def _iota(shape, dim):
    return lax.broadcasted_iota(jnp.int32, shape, dim)


def _sigmoid(x):
    return 1.0 / (1.0 + jnp.exp(-x))


def _softplus(x):
    return jnp.maximum(x, 0.0) + jnp.log1p(jnp.exp(-jnp.abs(x)))


def _silu(x):
    return x * _sigmoid(x)


def _params(*sems):
    return pltpu.CompilerParams(dimension_semantics=sems, vmem_limit_bytes=VMEM_LIMIT)


def _ada_kernel(c_ref, w_ref, b_ref, o_ref):
    c = c_ref[...]
    s = _silu(c).astype(_BF16)
    o_ref[0] = _dot(s, w_ref[0].astype(_BF16)) + b_ref[0]


def _ada(c_all, ada_w, ada_b):
    rows = c_all.shape[0]
    tn = 1536
    return pl.pallas_call(
        _ada_kernel,
        grid=(DEPTH, 6 * D_MODEL // tn),
        in_specs=[pl.BlockSpec((rows, D_MODEL), lambda l, j: (0, 0)),
                  pl.BlockSpec((1, D_MODEL, tn), lambda l, j: (l, 0, j)),
                  pl.BlockSpec((1, 1, tn), lambda l, j: (l, 0, j))],
        out_specs=pl.BlockSpec((1, rows, tn), lambda l, j: (l, 0, j)),
        out_shape=jax.ShapeDtypeStruct((DEPTH, rows, 6 * D_MODEL), _F32),
        compiler_params=_params("arbitrary", "arbitrary"),
    )(c_all, ada_w, ada_b.reshape(DEPTH, 1, 6 * D_MODEL))


def _in_proj_kernel(x_ref, sh_ref, sc_ref, g_ref, w_ref, z_ref, xbc_ref, frw_ref, us5_ref, dt_ref):
    x = x_ref[...]
    bb, tl, _ = x.shape
    xn = x * lax.rsqrt(jnp.mean(x * x, axis=-1, keepdims=True) + NORM_EPS)
    h = xn * g_ref[...] * (1.0 + sc_ref[...]) + sh_ref[...]
    h = h.reshape(bb * tl, D_MODEL).astype(_BF16)
    for o_ref, lo, hi in ((z_ref, _PZ, _PXBC), (xbc_ref, _PXBC, _PFRW), (frw_ref, _PFRW, _PUS5),
                          (us5_ref, _PUS5, _PDT), (dt_ref, _PDT, _P_COLS)):
        o_ref[...] = _dot(h, w_ref[:, lo:hi]).reshape(o_ref.shape)


def _in_proj(x, mod, P, l, *, bb, tl, time_major_s5):
    nb, L, _ = x.shape
    grid = (nb // bb, L // tl)
    xmap = lambda i, t: (i, t, 0)

    def modspec(j):
        return pl.BlockSpec((bb, 1, D_MODEL), lambda i, t: (i, 0, j))

    def out(n):
        return jax.ShapeDtypeStruct((nb, L, n), _F32), pl.BlockSpec((bb, tl, n), xmap)

    outs = [out(SSD_WIDTH), out(SSD_CONV_DIM), out(RWKV_PROJ), out(S5_WIDTH), out(DT_PAD)]
    if time_major_s5:
        assert bb == 1
        outs[3] = (jax.ShapeDtypeStruct((L, nb * S5_WIDTH), _F32), pl.BlockSpec((tl, S5_WIDTH), lambda i, t: (t, i)))
    return pl.pallas_call(
        _in_proj_kernel,
        grid=grid,
        in_specs=[pl.BlockSpec((bb, tl, D_MODEL), xmap), modspec(0), modspec(1),
                  pl.BlockSpec((None, 1, D_MODEL), lambda i, t: (l, 0, 0)),
                  pl.BlockSpec((None, D_MODEL, _P_COLS), lambda i, t: (l, 0, 0))],
        out_specs=[o[1] for o in outs],
        out_shape=[o[0] for o in outs],
        compiler_params=_params("arbitrary", "arbitrary"),
    )(x, mod, mod, P['norm1_g'], P['w_in_p'])


def _ssd_seq(j, z_ref, xbc_ref, dt_ref, cw_ref, cb_ref, dtb_ref, alog_ref, dexp_ref, ng_ref, eexp_ref,
             y_ref, sto_ref, cvo_ref, ext_scr, ybuf_scr, T, valid):
    u = xbc_ref[j]
    ext_scr[j, 0:SUBLANES, :] = cvo_ref[j]
    ext_scr[j, SUBLANES:SUBLANES + T, :] = u
    conv = cb_ref[...] + cw_ref[3:4, :] * u
    for back in (1, 2, 3):
        conv = conv + cw_ref[3 - back:4 - back, :] * ext_scr[j, SUBLANES - back:SUBLANES - back + T, :]
    cvo_ref[j] = ext_scr[j, valid:valid + SUBLANES, :]
    xbc = _silu(conv)
    xs = xbc[:, :SSD_WIDTH]
    row = _iota((T, 1), 0)
    dtv = _softplus(dt_ref[j] + dtb_ref[...])
    if valid < T:
        dtv = jnp.where(row < valid, dtv, 0.0)
    a_row = jnp.where(_iota((1, DT_PAD), 1) < SSD_HEADS, -jnp.exp(alog_ref[...]), 0.0)
    a = dtv * a_row
    yield
    tri_b = _iota((T, T), 0) >= _iota((T, T), 1)
    cs = _mm_sel(tri_b, a, _NN)
    eye = _iota((DT_PAD, DT_PAD), 0) == _iota((DT_PAD, DT_PAD), 1)
    xdt = xs * _mm_sel(eexp_ref[...], dtv, _NN, x_is_rhs=False)
    yield
    cs_t = _mm_sel(eye, cs, _NT)
    ecs = jnp.exp(cs)
    cs_last = cs[T - 1:T, :]
    dte = jnp.exp(cs_last - cs)
    elast = jnp.exp(cs_last)
    for g in range(SSD_GROUPS):
        bm = xbc[:, SSD_WIDTH + g * SSD_STATE:SSD_WIDTH + (g + 1) * SSD_STATE]
        cm = xbc[:, SSD_WIDTH + (SSD_GROUPS + g) * SSD_STATE:SSD_WIDTH + (SSD_GROUPS + g + 1) * SSD_STATE]
        cb = _dot_nt(cm, bm)
        for r in range(SSD_HEADS_PER_GROUP):
            if r == 0 or T >= 64:
                yield
            h = g * SSD_HEADS_PER_GROUP + r
            lo, hi = h * SSD_HEAD_DIM, (h + 1) * SSD_HEAD_DIM
            decay = jnp.exp(jnp.where(tri_b, cs[:, h:h + 1] - cs_t[h:h + 1, :], -jnp.inf))
            xdt_h = xdt[:, lo:hi]
            st = sto_ref[j, h]
            y_h = _dot(cb * decay, xdt_h) + ecs[:, h:h + 1] * _dot_nt(cm, st)
            ybuf_scr[j, :, lo:hi] = y_h
            sto_ref[j, h] = elast[:, h:h + 1] * st + _dot_tn(xdt_h * dte[:, h:h + 1], bm)
    yield
    y = ybuf_scr[j] + xs * dexp_ref[...]
    y = y * _silu(z_ref[j])
    y = y * lax.rsqrt(jnp.mean(y * y, axis=-1, keepdims=True) + NORM_EPS) * ng_ref[...]
    y_ref[j] = y


def _ssd_kernel(z_ref, xbc_ref, dt_ref, st_ref, cv_ref, cw_ref, cb_ref, dtb_ref, alog_ref, dexp_ref, ng_ref,
                eexp_ref, *rest, bb, T, valid, n_earlier):
    earlier = rest[:n_earlier]
    y_ref, sto_all, cvo_ref, ext_scr, ybuf_scr = rest[n_earlier:]
    sto_ref = sto_all.at[n_earlier] if n_earlier else sto_all

    @pl.when(pl.program_id(1) == 0)
    def _():
        sto_ref[...] = st_ref[...]
        cvo_ref[...] = cv_ref[...]
        for k, e_ref in enumerate(earlier):
            sto_all[k] = e_ref[...]

    seq = functools.partial(_ssd_seq, z_ref=z_ref, xbc_ref=xbc_ref, dt_ref=dt_ref, cw_ref=cw_ref, cb_ref=cb_ref,
                            dtb_ref=dtb_ref, alog_ref=alog_ref, dexp_ref=dexp_ref, ng_ref=ng_ref, eexp_ref=eexp_ref,
                            y_ref=y_ref, sto_ref=sto_ref, cvo_ref=cvo_ref, ext_scr=ext_scr, ybuf_scr=ybuf_scr,
                            T=T, valid=valid)
    _interleave(seq(j) for j in range(bb))


def _ssd(z, xbc, dt, st_ssd, st_conv8, P, l, *, bb, T, valid, earlier=()):
    nb, L, _ = z.shape
    grid = (nb // bb, L // T)
    tmap = lambda i, c: (i, c, 0)
    smap = lambda i, c: (i, 0, 0, 0)
    cvmap = lambda i, c: (i, 0, 0)
    layer = lambda r, n: pl.BlockSpec((None, r, n), lambda i, c: (l, 0, 0))
    state_blk = (bb, SSD_HEADS, SSD_HEAD_DIM, SSD_STATE)
    n_e = len(earlier)
    if n_e:
        state_out = (pl.BlockSpec((n_e + 1,) + state_blk, lambda i, c: (0, i, 0, 0, 0)),
                     jax.ShapeDtypeStruct((n_e + 1,) + st_ssd.shape[1:], _F32))
    else:
        state_out = (pl.BlockSpec(state_blk, smap), jax.ShapeDtypeStruct(st_ssd.shape[1:], _F32))
    return pl.pallas_call(
        functools.partial(_ssd_kernel, bb=bb, T=T, valid=valid, n_earlier=n_e),
        grid=grid,
        in_specs=[pl.BlockSpec((bb, T, SSD_WIDTH), tmap), pl.BlockSpec((bb, T, SSD_CONV_DIM), tmap),
                  pl.BlockSpec((bb, T, DT_PAD), tmap),
                  pl.BlockSpec((None, bb, SSD_HEADS, SSD_HEAD_DIM, SSD_STATE), lambda i, c: (l, i, 0, 0, 0)),
                  pl.BlockSpec((None, bb, SUBLANES, SSD_CONV_DIM), lambda i, c: (l, i, 0, 0)),
                  layer(SSD_CONV, SSD_CONV_DIM), layer(1, SSD_CONV_DIM), layer(1, DT_PAD), layer(1, DT_PAD),
                  layer(1, SSD_WIDTH), layer(1, SSD_WIDTH),
                  pl.BlockSpec((DT_PAD, SSD_WIDTH), lambda i, c: (0, 0))] + [pl.BlockSpec(state_blk, smap)] * n_e,
        out_specs=[pl.BlockSpec((bb, T, SSD_WIDTH), tmap), state_out[0],
                   pl.BlockSpec((bb, SUBLANES, SSD_CONV_DIM), cvmap)],
        out_shape=[jax.ShapeDtypeStruct((nb, L, SSD_WIDTH), _F32), state_out[1],
                   jax.ShapeDtypeStruct(st_conv8.shape[1:], _F32)],
        scratch_shapes=[pltpu.VMEM((bb, T + SUBLANES, SSD_CONV_DIM), _F32), pltpu.VMEM((bb, T, SSD_WIDTH), _F32)],
        compiler_params=_params("arbitrary", "arbitrary"),
    )(z, xbc, dt, st_ssd, st_conv8, P['ssd_conv_w'], P['ssd_conv_b'], P['ssd_dt_bias'], P['ssd_a_log'],
      P['ssd_d'], P['ssd_norm_g'], P['ssd_head_expand'], *earlier)


def _stack_heads(x):
    return jnp.concatenate([x[:, h * RWKV_HEAD_DIM:(h + 1) * RWKV_HEAD_DIM] for h in range(RWKV_HEADS)], axis=0)


def _unstack_heads(x, T):
    return jnp.concatenate([x[h * T:(h + 1) * T, :] for h in range(RWKV_HEADS)], axis=1)


def _interleave(gens):
    gens = list(gens)
    while gens:
        alive = []
        for g in gens:
            try:
                next(g)
                alive.append(g)
            except StopIteration:
                pass
        gens = alive


def _rwkv_masks(T):
    HT = RWKV_HEADS * T
    rr = _iota((HT, HT), 0)
    cc = _iota((HT, HT), 1)
    same = (rr // T) == (cc // T)
    rt, ct = _iota((T, T), 0), _iota((T, T), 1)
    head_of_row = _iota((HT, RWKV_WIDTH), 0) // T
    head_of_col = _iota((HT, RWKV_WIDTH), 1) // RWKV_HEAD_DIM
    return dict(strict=jnp.logical_and(same, rr > cc), incl=jnp.logical_and(same, rr >= cc),
                eye=(rr == cc).astype(_F32), hmask=head_of_row == head_of_col,
                strict_t=rt > ct, incl_t=rt >= ct, eye_t=(rt == ct).astype(_F32), tri=(rt >= ct).astype(_BF16))


def _rwkv_inverse(eye, mats, T):
    inv = [eye + n for n in mats]
    p = list(mats)
    for _ in range(max((T - 1).bit_length() - 1, 0)):
        yield
        p = [_dot(x, x, P_INV) for x in p]
        yield
        inv = [i + _dot(i, x, P_INV) for i, x in zip(inv, p)]
    return inv


def _rwkv_core_per_head(j, masks, so_ref, T, at, rt, bt, kt, v4f, w_end):
    cast = lambda x: x.astype(_BF16)
    a4, r4, b4, k4 = (cast(_stack_heads(t)) for t in (at, rt, bt, kt))
    v4 = cast(v4f)
    bw4, kw4 = (cast(_stack_heads(t * w_end)) for t in (bt, kt))
    heads = range(RWKV_HEADS)
    blk = lambda x, h: x[h * T:(h + 1) * T]
    yield
    n_ab = [jnp.where(masks['strict_t'], _dot_nt(blk(a4, h), blk(b4, h), P_AMAT), 0.0) for h in heads]
    n_ak = [cast(jnp.where(masks['strict_t'], _dot_nt(blk(a4, h), blk(k4, h), P_AMAT), 0.0)) for h in heads]
    m_rb = [cast(jnp.where(masks['incl_t'], _dot_nt(blk(r4, h), blk(b4, h), P_AMAT), 0.0)) for h in heads]
    m_rk = [cast(jnp.where(masks['incl_t'], _dot_nt(blk(r4, h), blk(k4, h), P_AMAT), 0.0)) for h in heads]
    inv = yield from _rwkv_inverse(masks['eye_t'], n_ab, T)
    yield
    s0 = [so_ref[j, h] for h in heads]
    rhs = [_dot_nt(blk(a4, h), s0[h], P_S0) + _dot(n_ak[h], blk(v4, h), P_INTRA) for h in heads]
    o_own = [_dot_nt(blk(r4, h), s0[h], P_S0) + _dot(m_rk[h], blk(v4, h), P_INTRA) for h in heads]
    yield
    u = [cast(_dot(inv[h], rhs[h], P_SOLVE)) for h in heads]
    yield
    o4 = jnp.concatenate([o_own[h] + _dot(m_rb[h], u[h], P_INTRA) for h in heads], axis=0)
    for h in heads:
        so_ref[j, h] = (s0[h] * w_end[:, h * RWKV_HEAD_DIM:(h + 1) * RWKV_HEAD_DIM]
                        + _dot_tn(u[h], blk(bw4, h), P_STATE) + _dot_tn(blk(v4, h), blk(kw4, h), P_STATE))
    yield
    return o4


def _rwkv_core_stacked(j, masks, so_ref, T, at, rt, bt, kt, v4f, w_end):
    strict, incl, hmask = masks['strict'], masks['incl'], masks['hmask']
    a4, r4, b4, k4 = (_stack_heads(t) for t in (at, rt, bt, kt))
    yield
    n_ab = jnp.where(strict, _dot_nt(a4, b4, P_AMAT), 0.0)
    n_ak = jnp.where(strict, _dot_nt(a4, k4, P_AMAT), 0.0)
    m_rb = jnp.where(incl, _dot_nt(r4, b4, P_AMAT), 0.0)
    m_rk = jnp.where(incl, _dot_nt(r4, k4, P_AMAT), 0.0)
    (inv,) = yield from _rwkv_inverse(masks['eye'], [n_ab], T)

    def wide(x):
        return jnp.where(hmask, jnp.concatenate([x] * RWKV_HEADS, axis=0), 0.0)

    yield
    s0 = jnp.concatenate([so_ref[j, h] for h in range(RWKV_HEADS)], axis=1)
    rhs = _dot_nt(wide(at), s0, P_S0) + _dot(n_ak, v4f, P_INTRA)
    o4 = _dot_nt(wide(rt), s0, P_S0) + _dot(m_rk, v4f, P_INTRA)
    yield
    u4 = _dot(inv, rhs, P_SOLVE)
    yield
    o4 = o4 + _dot(m_rb, u4, P_INTRA)
    s1 = (s0 * w_end + _dot_tn(u4, wide(bt * w_end), P_STATE) + _dot_tn(v4f, wide(kt * w_end), P_STATE))
    for h in range(RWKV_HEADS):
        so_ref[j, h] = s1[:, h * RWKV_HEAD_DIM:(h + 1) * RWKV_HEAD_DIM]
    yield
    return o4


def _rwkv_seq(j, masks, f_ref, mu_ref, w0_ref, w2_ref, a0_ref, a2_ref, g2_ref, kk_ref, ka_ref, rk_ref, lng_ref,
              lnb_ref, y_ref, so_ref, sho_ref, ext_scr, T, valid):
    f = f_ref[j]
    ext_scr[j, 0:SUBLANES, :] = sho_ref[j]
    ext_scr[j, SUBLANES:SUBLANES + T, :] = f
    prev = ext_scr[j, SUBLANES - 1:SUBLANES - 1 + T, :]
    sho_ref[j] = ext_scr[j, valid:valid + SUBLANES, :]
    fm = f + (prev - f) * mu_ref[...]
    r = fm[:, 0:256]
    k = fm[:, 256:512]
    v = fm[:, 512:768]
    lora = fm[:, 768:896]
    gl = fm[:, 896:1024]
    w_log = -_softplus(-(w0_ref[...] + _dot(jnp.tanh(lora), w2_ref[...]))) - 0.5
    logw = -jnp.exp(w_log)
    a = _sigmoid(a0_ref[...] + _dot(lora, a2_ref[...]))
    g = _dot(_sigmoid(gl), g2_ref[...])
    k2 = k * (1.0 + (a - 1.0) * ka_ref[...])
    kk4 = _stack_heads(k * kk_ref[...])
    kk4 = kk4 * lax.rsqrt(jnp.maximum(jnp.sum(kk4 * kk4, axis=-1, keepdims=True), 1e-24))
    kkn = _unstack_heads(kk4, T)
    if valid < T:
        live = _iota((T, 1), 0) < valid
        logw = jnp.where(live, logw, 0.0)
        kkn = jnp.where(live, kkn, 0.0)
        k2 = jnp.where(live, k2, 0.0)
    yield
    cw = _mm_sel(masks['tri'], logw, _NN)
    w_inc = jnp.exp(cw)
    w_exc = jnp.exp(cw - logw)
    w_inv = jnp.exp(-cw)
    w_end = w_inc[T - 1:T, :]
    at = -kkn * w_exc
    rt = r * w_inc
    bt = kkn * a * w_inv
    kt = k2 * w_inv
    v4f = _stack_heads(v)
    core = _rwkv_core_per_head if T % 16 == 0 else _rwkv_core_stacked
    o4 = yield from core(j, masks, so_ref, T, at, rt, bt, kt, v4f, w_end)

    mean = jnp.mean(o4, axis=-1, keepdims=True)
    var = jnp.mean(jnp.square(o4 - mean), axis=-1, keepdims=True)
    o4 = (o4 - mean) * lax.rsqrt(var + RWKV_LN_EPS)
    bonus4 = jnp.sum(_stack_heads(r * k2 * rk_ref[...]), axis=-1, keepdims=True) * v4f
    o = _unstack_heads(o4, T) * lng_ref[...] + lnb_ref[...] + _unstack_heads(bonus4, T)
    y_ref[j] = o * g


def _rwkv_kernel(f_ref, s_ref, sh_ref, mu_ref, w0_ref, w2_ref, a0_ref, a2_ref, g2_ref, kk_ref, ka_ref, rk_ref,
                 lng_ref, lnb_ref, *rest, bb, T, valid, n_earlier):
    earlier = rest[:n_earlier]
    y_ref, so_all, sho_ref, ext_scr = rest[n_earlier:]
    so_ref = so_all.at[n_earlier] if n_earlier else so_all

    @pl.when(pl.program_id(1) == 0)
    def _():
        so_ref[...] = s_ref[...]
        sho_ref[...] = sh_ref[...]
        for k, e_ref in enumerate(earlier):
            so_all[k] = e_ref[...]

    seq = functools.partial(_rwkv_seq, f_ref=f_ref, mu_ref=mu_ref, w0_ref=w0_ref, w2_ref=w2_ref, a0_ref=a0_ref,
                            a2_ref=a2_ref, g2_ref=g2_ref, kk_ref=kk_ref, ka_ref=ka_ref, rk_ref=rk_ref,
                            lng_ref=lng_ref, lnb_ref=lnb_ref, y_ref=y_ref, so_ref=so_ref, sho_ref=sho_ref,
                            ext_scr=ext_scr, T=T, valid=valid)
    masks = _rwkv_masks(T)
    _interleave(seq(j, masks) for j in range(bb))


def _rwkv(f, st_rwkv, shift8, P, l, *, bb, T, valid, earlier=()):
    nb, L, _ = f.shape
    hd = RWKV_HEAD_DIM
    state_blk = (bb, RWKV_HEADS, hd, hd)
    smap = lambda i, c: (i, 0, 0, 0)
    n_e = len(earlier)
    if n_e:
        state_out = (pl.BlockSpec((n_e + 1,) + state_blk, lambda i, c: (0, i, 0, 0, 0)),
                     jax.ShapeDtypeStruct((n_e + 1,) + st_rwkv.shape[1:], _F32))
    else:
        state_out = (pl.BlockSpec(state_blk, smap), jax.ShapeDtypeStruct(st_rwkv.shape[1:], _F32))
    grid = (nb // bb, L // T)
    tmap = lambda i, c: (i, c, 0)
    bmap = lambda i, c: (i, 0, 0)
    layer = lambda r, n: pl.BlockSpec((None, r, n), lambda i, c: (l, 0, 0))
    row = lambda n: layer(1, n)
    lora_w = RWKV_DECAY_LORA + RWKV_A_LORA
    return pl.pallas_call(
        functools.partial(_rwkv_kernel, bb=bb, T=T, valid=valid, n_earlier=n_e),
        grid=grid,
        in_specs=[pl.BlockSpec((bb, T, RWKV_PROJ), tmap),
                  pl.BlockSpec((None, bb, RWKV_HEADS, hd, hd), lambda i, c: (l, i, 0, 0, 0)),
                  pl.BlockSpec((None, bb, SUBLANES, RWKV_PROJ), lambda i, c: (l, i, 0, 0)),
                  row(RWKV_PROJ), row(RWKV_WIDTH), layer(lora_w, RWKV_WIDTH),
                  row(RWKV_WIDTH), layer(lora_w, RWKV_WIDTH), layer(RWKV_GATE_LORA, RWKV_WIDTH),
                  row(RWKV_WIDTH), row(RWKV_WIDTH), row(RWKV_WIDTH), row(RWKV_WIDTH), row(RWKV_WIDTH)]
        + [pl.BlockSpec(state_blk, smap)] * n_e,
        out_specs=[pl.BlockSpec((bb, T, RWKV_WIDTH), tmap), state_out[0],
                   pl.BlockSpec((bb, SUBLANES, RWKV_PROJ), bmap)],
        out_shape=[jax.ShapeDtypeStruct((nb, L, RWKV_WIDTH), _F32), state_out[1],
                   jax.ShapeDtypeStruct(shift8.shape[1:], _F32)],
        scratch_shapes=[pltpu.VMEM((bb, T + SUBLANES, RWKV_PROJ), _F32)],
        compiler_params=_params("arbitrary", "arbitrary"),
    )(f, st_rwkv, shift8, P['rwkv_mu'], P['rwkv_w0'], P['rwkv_w2'], P['rwkv_a0'], P['rwkv_a2'], P['rwkv_g2'],
      P['rwkv_k_k'], P['rwkv_k_a'], P['rwkv_r_k'], P['rwkv_ln_g'], P['rwkv_ln_b'], *earlier)


def _s5_kernel(u_ref, sre_ref, sim_ref, lre_ref, lim_ref, dt_ref, bre_ref, bim_ref, cre_ref, cim_ref, d_ref,
               gw_ref, gb_ref, y_ref, hre_ref, him_ref, hre_scr, him_scr, *, nb, tl, final_step):
    @pl.when(pl.program_id(0) == 0)
    def _():
        hre_ref[...] = sre_ref[...]
        him_ref[...] = sim_ref[...]

    lre, lim, dt = lre_ref[...], lim_ref[...], jnp.exp(dt_ref[...])
    mag = jnp.exp(lre * dt)
    ang = lim * dt
    ab_re, ab_im = mag * jnp.cos(ang), mag * jnp.sin(ang)
    den = jnp.square(lre) + jnp.square(lim)
    q_re = ((ab_re - 1.0) * lre + ab_im * lim) / den
    q_im = (ab_im * lre - (ab_re - 1.0) * lim) / den
    b_re, b_im = bre_ref[...], bim_ref[...]
    bb_re = q_re * b_re - q_im * b_im
    bb_im = q_re * b_im + q_im * b_re
    u = u_ref[...]
    hre_scr[0:nb, :] = hre_ref[...]
    him_scr[0:nb, :] = him_ref[...]
    hre_scr[nb:, :] = _dot(u, bb_re)
    him_scr[nb:, :] = _dot(u, bb_im)
    ar = jnp.broadcast_to(ab_re, (nb, S5_FLAT))
    ai = jnp.broadcast_to(ab_im, (nb, S5_FLAT))

    if nb <= SUBLANES:
        def step(s, carry):
            pr, pi = carry
            cur = pl.ds(pl.multiple_of((s + 1) * nb, nb), nb)
            nr = ar * pr - ai * pi + hre_scr[cur, :]
            ni = ar * pi + ai * pr + him_scr[cur, :]
            hre_scr[cur, :] = nr
            him_scr[cur, :] = ni
            return nr, ni

        lax.fori_loop(0, tl, step, (hre_ref[...], him_ref[...]), unroll=2)
    else:
        def step(s, carry):
            prev = pl.ds(pl.multiple_of(s * nb, nb), nb)
            cur = pl.ds(pl.multiple_of((s + 1) * nb, nb), nb)
            pr, pi = hre_scr[prev, :], him_scr[prev, :]
            hre_scr[cur, :] = ar * pr - ai * pi + hre_scr[cur, :]
            him_scr[cur, :] = ar * pi + ai * pr + him_scr[cur, :]
            return carry

        lax.fori_loop(0, tl, step, 0)
    hre_ref[...] = hre_scr[(final_step + 1) * nb:(final_step + 2) * nb, :]
    him_ref[...] = him_scr[(final_step + 1) * nb:(final_step + 2) * nb, :]
    y = _dot(hre_scr[nb:, :], cre_ref[...]) - _dot(him_scr[nb:, :], cim_ref[...]) + d_ref[...] * u
    y = 0.5 * y * (1.0 + jnp.tanh(0.7978845608028654 * (y + 0.044715 * (y * y * y))))
    y_ref[...] = y * _sigmoid(_dot(y, gw_ref[...]) + gb_ref[...])


def _s5(u_tm, st_re, st_im, P, l, *, nb, tl, final_step):
    rows = u_tm.shape[0]
    L = rows // nb
    layer = lambda r, n: pl.BlockSpec((None, r, n), lambda t: (l, 0, 0))
    row = lambda n: layer(1, n)
    full = lambda a, b: pl.BlockSpec((a, b), lambda t: (0, 0))
    state = pl.BlockSpec((None, nb, S5_FLAT), lambda t: (l, 0, 0))
    return pl.pallas_call(
        functools.partial(_s5_kernel, nb=nb, tl=tl, final_step=final_step),
        grid=(L // tl,),
        in_specs=[pl.BlockSpec((tl * nb, S5_WIDTH), lambda t: (t, 0)), state, state,
                  row(S5_FLAT), row(S5_FLAT), row(S5_FLAT),
                  layer(S5_WIDTH, S5_FLAT), layer(S5_WIDTH, S5_FLAT), layer(S5_FLAT, S5_WIDTH),
                  layer(S5_FLAT, S5_WIDTH), row(S5_WIDTH), layer(S5_WIDTH, S5_WIDTH), row(S5_WIDTH)],
        out_specs=[pl.BlockSpec((tl * nb, S5_WIDTH), lambda t: (t, 0)), full(nb, S5_FLAT), full(nb, S5_FLAT)],
        out_shape=[jax.ShapeDtypeStruct((rows, S5_WIDTH), _F32),
                   jax.ShapeDtypeStruct((nb, S5_FLAT), _F32), jax.ShapeDtypeStruct((nb, S5_FLAT), _F32)],
        scratch_shapes=[pltpu.VMEM(((tl + 1) * nb, S5_FLAT), _F32), pltpu.VMEM(((tl + 1) * nb, S5_FLAT), _F32)],
        compiler_params=_params("arbitrary"),
    )(u_tm, st_re, st_im, P['s5_a_re'], P['s5_a_im'], P['s5_log_dt'], P['s5_b_re'], P['s5_b_im'], P['s5_c_re'],
      P['s5_c_im'], P['s5_d'], P['s5_glu_w'], P['s5_glu_b'])


def _out_mlp_kernel(x_ref, yssd_ref, yrw_ref, ys5_ref, g1_ref, sh2_ref, sc2_ref, g2_ref, n2g_ref, fg_ref,
                    wo_ref, w1_ref, w2_ref, o_ref, hid_scr, *, final_norm):
    bb, tl, _ = x_ref.shape
    rows = bb * tl
    flat = lambda ref: ref[...].reshape(rows, ref.shape[-1]).astype(_BF16)
    mix = (_dot(flat(yssd_ref), wo_ref[0:512, :]) + _dot(flat(yrw_ref), wo_ref[512:768, :])
           + _dot(flat(ys5_ref), wo_ref[768:1024, :]))
    x1 = x_ref[...] + g1_ref[...] * mix.reshape(bb, tl, D_MODEL)
    xn = x1 * lax.rsqrt(jnp.mean(x1 * x1, axis=-1, keepdims=True) + NORM_EPS)
    h2 = (xn * n2g_ref[...] * (1.0 + sc2_ref[...]) + sh2_ref[...]).reshape(rows, D_MODEL).astype(_BF16)
    for c in range(D_FF // FF_TILE):
        cols = slice(c * FF_TILE, (c + 1) * FF_TILE)
        hid_scr[:, cols] = jnp.square(jnp.maximum(_dot(h2, w1_ref[:, cols]), 0.0)).astype(_BF16)
    ff = _dot(hid_scr[...], w2_ref[...])
    out = x1 + g2_ref[...] * ff.reshape(bb, tl, D_MODEL)
    if final_norm:
        out = out * lax.rsqrt(jnp.mean(out * out, axis=-1, keepdims=True) + NORM_EPS) * fg_ref[...]
    o_ref[...] = out


def _out_mlp(x, y_ssd, y_rw, y_s5, mod, P, l, *, bb, tl, time_major_s5, final_norm):
    nb, L, _ = x.shape
    grid = (nb // bb, L // tl)
    xmap = lambda i, t: (i, t, 0)
    cmap = lambda i, t: (0, 0)
    wmap = lambda i, t: (l, 0, 0)
    once = pl.Buffered(1)

    def modspec(j):
        return pl.BlockSpec((bb, 1, D_MODEL), lambda i, t: (i, 0, j))

    s5_spec = (pl.BlockSpec((tl, S5_WIDTH), lambda i, t: (t, i)) if time_major_s5
               else pl.BlockSpec((bb, tl, S5_WIDTH), xmap))
    return pl.pallas_call(
        functools.partial(_out_mlp_kernel, final_norm=final_norm),
        grid=grid,
        in_specs=[pl.BlockSpec((bb, tl, D_MODEL), xmap), pl.BlockSpec((bb, tl, SSD_WIDTH), xmap),
                  pl.BlockSpec((bb, tl, RWKV_WIDTH), xmap), s5_spec,
                  modspec(2), modspec(3), modspec(4), modspec(5),
                  pl.BlockSpec((None, 1, D_MODEL), wmap), pl.BlockSpec((1, D_MODEL), cmap),
                  pl.BlockSpec((None, D_MODEL, D_MODEL), wmap, pipeline_mode=once),
                  pl.BlockSpec((None, D_MODEL, D_FF), wmap, pipeline_mode=once),
                  pl.BlockSpec((None, D_FF, D_MODEL), wmap, pipeline_mode=once)],
        out_specs=pl.BlockSpec((bb, tl, D_MODEL), xmap),
        out_shape=jax.ShapeDtypeStruct(x.shape, _F32),
        scratch_shapes=[pltpu.VMEM((bb * tl, D_FF), _BF16)],
        compiler_params=_params("arbitrary", "arbitrary"),
    )(x, y_ssd, y_rw, y_s5, mod, mod, mod, mod, P['norm2_g'], P['final_g'], P['w_out'], P['mlp_w1'], P['mlp_w2'])


def _trunk(x, mod, states, P, valid, *, prompt):
    st_ssd, st_conv, st_rwkv, st_shift, st_re, st_im = states
    nb, L, _ = x.shape
    if prompt:
        bb, tl, mlp_tl = 1, min(IN_ROW_TILE, L), min(MLP_ROW_TILE, L)
        ssd_t, rwkv_t, s5_tl = min(SSD_CHUNK, L), min(RWKV_CHUNK, L), min(S5_TILE, L)
        ssd_bb, rwkv_bb = min(nb, SSD_PROMPT_SEQS), min(nb, RWKV_PROMPT_SEQS)
    else:
        bb, tl, mlp_tl = min(nb, ROW_TILE // L), L, L
        ssd_t = rwkv_t = s5_tl = L
        ssd_bb = rwkv_bb = min(nb, SAMPLE_SEQS)
    new = ([], [], [], [], [], [])
    for l in range(DEPTH):
        z, xbc, frw, us5, dt = _in_proj(x, mod[l], P, l, bb=bb, tl=tl, time_major_s5=prompt)
        v = valid
        last = l == DEPTH - 1
        y_ssd, n_ssd, n_conv = _ssd(z, xbc, dt, st_ssd, st_conv, P, l, bb=ssd_bb, T=ssd_t,
                                    valid=ssd_t if prompt else v, earlier=tuple(new[0]) if last else ())
        y_rw, n_rwkv, n_shift = _rwkv(frw, st_rwkv, st_shift, P, l, bb=rwkv_bb, T=rwkv_t,
                                      valid=rwkv_t if prompt else v, earlier=tuple(new[2]) if last else ())
        if prompt:
            u_tm = us5.reshape(L * nb, S5_WIDTH)
        else:
            u_tm = jnp.transpose(us5, (1, 0, 2)).reshape(L * nb, S5_WIDTH)
        y_s5, n_re, n_im = _s5(u_tm, st_re, st_im, P, l, nb=nb, tl=s5_tl,
                               final_step=(s5_tl - 1) if prompt else (v - 1))
        if prompt:
            y_s5 = y_s5.reshape(L, nb * S5_WIDTH)
        else:
            y_s5 = jnp.transpose(y_s5.reshape(L, nb, S5_WIDTH), (1, 0, 2))
        x = _out_mlp(x, y_ssd, y_rw, y_s5, mod[l], P, l, bb=bb, tl=mlp_tl, time_major_s5=prompt,
                     final_norm=(l == DEPTH - 1))
        for lst, s in zip(new, (n_ssd, n_conv, n_rwkv, n_shift, n_re, n_im)):
            lst.append(s)
    by_kernel = (0, 2) if DEPTH > 1 else ()
    return x, [lst[-1] if k in by_kernel else jnp.stack(lst) for k, lst in enumerate(new)]


def _pack_states(st_ssd, st_conv, st_rwkv, st_shift, st_re, st_im):
    d, nb = st_ssd.shape[:2]
    conv8 = jnp.pad(st_conv, ((0, 0), (0, 0), (SUBLANES - (SSD_CONV - 1), 0), (0, 0)))
    shift8 = jnp.pad(st_shift[:, :, None, :], ((0, 0), (0, 0), (SUBLANES - 1, 0), (0, 0)))
    return (st_ssd, conv8, st_rwkv, shift8, st_re.reshape(d, nb, S5_FLAT), st_im.reshape(d, nb, S5_FLAT))


def _unpack_states(n_ssd, n_conv8, n_rwkv, n_shift8, n_re, n_im):
    d, nb = n_ssd.shape[:2]
    return (n_ssd, n_conv8[:, :, SUBLANES - (SSD_CONV - 1):, :], n_rwkv, n_shift8[:, :, SUBLANES - 1, :],
            n_re.reshape(d, nb, S5_GROUPS, S5_STATE), n_im.reshape(d, nb, S5_GROUPS, S5_STATE))


def _prepare_params(norm1_g, norm2_g, w_in, ssd_conv_w, ssd_conv_b, ssd_dt_bias, ssd_a_log, ssd_d, ssd_norm_g, rwkv_mu,
                    rwkv_w0, rwkv_w2, rwkv_a0, rwkv_a2, rwkv_g2, rwkv_k_k, rwkv_k_a, rwkv_r_k, rwkv_ln_g, rwkv_ln_b,
                    s5_a_re, s5_a_im, s5_log_dt, s5_b_re, s5_b_im, s5_c_re, s5_c_im, s5_d, s5_glu_w, s5_glu_b, w_out,
                    mlp_w1, mlp_w2, final_g):
    rows = lambda a: a.reshape(DEPTH, 1, -1)
    head_pad = ((0, 0), (0, DT_PAD - SSD_HEADS))
    eye = jnp.eye(S5_GROUPS, dtype=_F32)
    zeros_lora = jnp.zeros((DEPTH, RWKV_A_LORA, RWKV_WIDTH), _F32)
    w_in_p = jnp.concatenate(
        [w_in[:, :, _Z0:_DT0], w_in[:, :, _FRW0:_IN_COLS], w_in[:, :, _DT0:_FRW0],
         jnp.zeros((DEPTH, D_MODEL, DT_PAD - SSD_HEADS), w_in.dtype)], axis=-1).astype(_BF16)
    return dict(
        norm1_g=rows(norm1_g), norm2_g=rows(norm2_g), final_g=final_g.reshape(1, D_MODEL), w_in_p=w_in_p,
        w_out=w_out.astype(_BF16), mlp_w1=mlp_w1.astype(_BF16), mlp_w2=mlp_w2.astype(_BF16),
        ssd_conv_w=ssd_conv_w, ssd_conv_b=rows(ssd_conv_b), ssd_dt_bias=rows(jnp.pad(ssd_dt_bias, head_pad)),
        ssd_a_log=rows(jnp.pad(ssd_a_log, head_pad)), ssd_d=rows(jnp.repeat(ssd_d, SSD_HEAD_DIM, axis=-1)),
        ssd_norm_g=rows(ssd_norm_g),
        ssd_head_expand=jnp.repeat(jnp.eye(DT_PAD, SSD_HEADS, dtype=_F32), SSD_HEAD_DIM, axis=1),
        rwkv_mu=rows(rwkv_mu), rwkv_w0=rows(rwkv_w0), rwkv_a0=rows(rwkv_a0), rwkv_k_k=rows(rwkv_k_k),
        rwkv_k_a=rows(rwkv_k_a), rwkv_r_k=rows(rwkv_r_k), rwkv_ln_g=rows(rwkv_ln_g), rwkv_ln_b=rows(rwkv_ln_b),
        rwkv_w2=jnp.concatenate([rwkv_w2, zeros_lora], axis=1),
        rwkv_a2=jnp.concatenate([zeros_lora, rwkv_a2], axis=1),
        rwkv_g2=rwkv_g2,
        s5_a_re=rows(s5_a_re), s5_a_im=rows(s5_a_im), s5_log_dt=rows(jnp.repeat(s5_log_dt, S5_STATE, axis=-1)),
        s5_b_re=jnp.einsum('lgpc,gh->lgchp', s5_b_re, eye).reshape(DEPTH, S5_WIDTH, S5_FLAT),
        s5_b_im=jnp.einsum('lgpc,gh->lgchp', s5_b_im, eye).reshape(DEPTH, S5_WIDTH, S5_FLAT),
        s5_c_re=jnp.einsum('lgcp,gh->lgphc', s5_c_re, eye).reshape(DEPTH, S5_FLAT, S5_WIDTH),
        s5_c_im=jnp.einsum('lgcp,gh->lgphc', s5_c_im, eye).reshape(DEPTH, S5_FLAT, S5_WIDTH),
        s5_d=rows(s5_d), s5_glu_w=s5_glu_w, s5_glu_b=rows(s5_glu_b))


def kernel(x_prompt, x_sample, c_prompt, c_sample, state_ssd, state_ssd_conv, state_rwkv, state_rwkv_shift,
           state_s5_re, state_s5_im, ada_w, ada_b, norm1_g, norm2_g, w_in, ssd_conv_w, ssd_conv_b, ssd_dt_bias,
           ssd_a_log, ssd_d, ssd_norm_g, rwkv_mu, rwkv_w0, rwkv_w2, rwkv_a0, rwkv_a2, rwkv_g2, rwkv_k_k, rwkv_k_a,
           rwkv_r_k, rwkv_ln_g, rwkv_ln_b, s5_a_re, s5_a_im, s5_log_dt, s5_b_re, s5_b_im, s5_c_re, s5_c_im, s5_d,
           s5_glu_w, s5_glu_b, w_out, mlp_w1, mlp_w2, final_g):
    bp, lp, _ = x_prompt.shape
    bs, ls, _ = x_sample.shape
    ls_pad = -(-ls // SUBLANES) * SUBLANES
    P = _prepare_params(norm1_g, norm2_g, w_in, ssd_conv_w, ssd_conv_b, ssd_dt_bias, ssd_a_log, ssd_d, ssd_norm_g,
                        rwkv_mu, rwkv_w0, rwkv_w2, rwkv_a0, rwkv_a2, rwkv_g2, rwkv_k_k, rwkv_k_a, rwkv_r_k, rwkv_ln_g,
                        rwkv_ln_b, s5_a_re, s5_a_im, s5_log_dt, s5_b_re, s5_b_im, s5_c_re, s5_c_im, s5_d, s5_glu_w,
                        s5_glu_b, w_out, mlp_w1, mlp_w2, final_g)
    mod = _ada(jnp.concatenate([c_prompt, c_sample], axis=0), ada_w, ada_b)
    mod_p = mod[:, :bp].reshape(DEPTH, bp, 1, 6 * D_MODEL)
    mod_s = mod[:, bp:].reshape(DEPTH, bs, 1, 6 * D_MODEL)

    zeros = lambda shape: jnp.zeros((DEPTH, bp) + shape, _F32)
    p_states = _pack_states(zeros((SSD_HEADS, SSD_HEAD_DIM, SSD_STATE)), zeros((SSD_CONV - 1, SSD_CONV_DIM)),
                            zeros((RWKV_HEADS, RWKV_HEAD_DIM, RWKV_HEAD_DIM)), zeros((RWKV_PROJ,)),
                            zeros((S5_GROUPS, S5_STATE)), zeros((S5_GROUPS, S5_STATE)))
    y_prompt, sp = _trunk(x_prompt, mod_p, p_states, P, ls, prompt=True)

    s_states = _pack_states(state_ssd, state_ssd_conv, state_rwkv, state_rwkv_shift, state_s5_re, state_s5_im)
    x_s = jnp.pad(x_sample, ((0, 0), (0, ls_pad - ls), (0, 0)))
    y_sample, ss = _trunk(x_s, mod_s, s_states, P, ls, prompt=False)
    return (y_prompt, y_sample[:, :ls]) + _unpack_states(*sp) + _unpack_states(*ss)
```

```python
import functools

import jax
import jax.numpy as jnp
from jax import lax
from jax.experimental import pallas as pl
from jax.experimental.pallas import tpu as pltpu

D_MODEL = 1024
DEPTH = 2
SSD_WIDTH = 512
SSD_HEAD_DIM = 64
SSD_HEADS = 8
SSD_GROUPS = 2
SSD_HEADS_PER_GROUP = SSD_HEADS // SSD_GROUPS
SSD_STATE = 128
SSD_CONV = 4
SSD_CONV_DIM = SSD_WIDTH + 2 * SSD_GROUPS * SSD_STATE
RWKV_WIDTH = 256
RWKV_HEAD_DIM = 64
RWKV_HEADS = 4
RWKV_DECAY_LORA = 64
RWKV_A_LORA = 64
RWKV_GATE_LORA = 128
RWKV_PROJ = 3 * RWKV_WIDTH + RWKV_DECAY_LORA + RWKV_A_LORA + RWKV_GATE_LORA
RWKV_LN_EPS = 64e-5
S5_WIDTH = 256
S5_GROUP_CH = 16
S5_GROUPS = 16
S5_STATE = 64
S5_FLAT = S5_GROUPS * S5_STATE
D_FF = 4 * D_MODEL
NORM_EPS = 1e-6
DT_PAD = 128
SUBLANES = 8

_Z0, _XBC0, _DT0, _FRW0, _US50 = 0, SSD_WIDTH, SSD_WIDTH + SSD_CONV_DIM, SSD_WIDTH + SSD_CONV_DIM + SSD_HEADS, \
    SSD_WIDTH + SSD_CONV_DIM + SSD_HEADS + RWKV_PROJ
_IN_COLS = _US50 + S5_WIDTH
_PZ, _PXBC, _PFRW, _PUS5, _PDT = 0, 512, 1536, 2560, 2816
_P_COLS = _PDT + DT_PAD

SSD_CHUNK = 128
RWKV_CHUNK = 64
SSD_PROMPT_SEQS = 2
RWKV_PROMPT_SEQS = 8
MIX_PROMPT_SEQS = 4
SAMPLE_SEQS = 8
S5_TILE = 256
ROW_TILE = 512
IN_ROW_TILE = 1024
MLP_ROW_TILE = 512
FF_TILE = 1024
VMEM_LIMIT = 56 * 1024 * 1024

_HI = lax.Precision.HIGHEST
P_AMAT = 1
P_INV = 1
P_S0 = 1
P_INTRA = 1
P_SOLVE = 1
P_STATE = 1
_F32 = jnp.float32
_BF16 = jnp.bfloat16


_NN = (((1,), (0,)), ((), ()))
_NT = (((1,), (1,)), ((), ()))
_TN = (((0,), (0,)), ((), ()))


def _split(x):
    hi = x.astype(_BF16)
    return hi, (x - hi.astype(_F32)).astype(_BF16)


def _mm(a, b, dims, passes):
    dg = lambda x, y, prec=None: lax.dot_general(x, y, dims, preferred_element_type=_F32, precision=prec)
    if passes == 6:
        return dg(a.astype(_F32), b.astype(_F32), _HI)
    if passes == 3:
        ah, al = _split(a)
        bh, bl = _split(b)
        return dg(ah, bh) + (dg(ah, bl) + dg(al, bh))
    return dg(a.astype(_BF16), b.astype(_BF16))


def _dot(a, b, passes=1):
    return _mm(a, b, _NN, passes)


def _dot_nt(a, b, passes=1):
    return _mm(a, b, _NT, passes)


def _dot_tn(a, b, passes=1):
    return _mm(a, b, _TN, passes)


def _mm_sel(sel, x, dims, x_is_rhs=True):
    hi = x.astype(_BF16)
    rest = x - hi.astype(_F32)
    mid = rest.astype(_BF16)
    lo = (rest - mid.astype(_F32)).astype(_BF16)
    sel = sel.astype(_BF16)
    if x_is_rhs:
        dg = lambda y: lax.dot_general(sel, y, dims, preferred_element_type=_F32)
    else:
        dg = lambda y: lax.dot_general(y, sel, dims, preferred_element_type=_F32)
    return dg(hi) + (dg(mid) + dg(lo))


def _iota(shape, dim):
    return lax.broadcasted_iota(jnp.int32, shape, dim)


def _sigmoid(x):
    return 1.0 / (1.0 + jnp.exp(-x))


def _softplus(x):
    return jnp.maximum(x, 0.0) + jnp.log1p(jnp.exp(-jnp.abs(x)))


def _silu(x):
    return x * _sigmoid(x)


def _params(*sems):
    return pltpu.CompilerParams(dimension_semantics=sems, vmem_limit_bytes=VMEM_LIMIT)


def _ada_kernel(c_ref, w_ref, b_ref, o_ref):
    c = c_ref[...]
    s = _silu(c).astype(_BF16)
    o_ref[0] = _dot(s, w_ref[0].astype(_BF16)) + b_ref[0]


def _ada(c_all, ada_w, ada_b):
    rows = c_all.shape[0]
    tn = 1536
    return pl.pallas_call(
        _ada_kernel,
        grid=(DEPTH, 6 * D_MODEL // tn),
        in_specs=[pl.BlockSpec((rows, D_MODEL), lambda l, j: (0, 0)),
                  pl.BlockSpec((1, D_MODEL, tn), lambda l, j: (l, 0, j)),
                  pl.BlockSpec((1, 1, tn), lambda l, j: (l, 0, j))],
        out_specs=pl.BlockSpec((1, rows, tn), lambda l, j: (l, 0, j)),
        out_shape=jax.ShapeDtypeStruct((DEPTH, rows, 6 * D_MODEL), _F32),
        compiler_params=_params("arbitrary", "arbitrary"),
    )(c_all, ada_w, ada_b.reshape(DEPTH, 1, 6 * D_MODEL))


def _in_proj_kernel(x_ref, sh_ref, sc_ref, g_ref, w_ref, z_ref, xbc_ref, frw_ref, us5_ref, dt_ref):
    x = x_ref[...]
    bb, tl, _ = x.shape
    xn = x * lax.rsqrt(jnp.mean(x * x, axis=-1, keepdims=True) + NORM_EPS)
    h = xn * g_ref[...] * (1.0 + sc_ref[...]) + sh_ref[...]
    h = h.reshape(bb * tl, D_MODEL).astype(_BF16)
    for o_ref, lo, hi in ((z_ref, _PZ, _PXBC), (xbc_ref, _PXBC, _PFRW), (frw_ref, _PFRW, _PUS5),
                          (us5_ref, _PUS5, _PDT), (dt_ref, _PDT, _P_COLS)):
        o_ref[...] = _dot(h, w_ref[:, lo:hi]).reshape(o_ref.shape)


def _in_proj(x, mod, P, l, *, bb, tl, time_major_s5):
    nb, L, _ = x.shape
    grid = (nb // bb, L // tl)
    xmap = lambda i, t: (i, t, 0)

    def modspec(j):
        return pl.BlockSpec((bb, 1, D_MODEL), lambda i, t: (i, 0, j))

    def out(n):
        return jax.ShapeDtypeStruct((nb, L, n), _F32), pl.BlockSpec((bb, tl, n), xmap)

    outs = [out(SSD_WIDTH), out(SSD_CONV_DIM), out(RWKV_PROJ), out(S5_WIDTH), out(DT_PAD)]
    if time_major_s5:
        assert bb == 1
        outs[3] = (jax.ShapeDtypeStruct((L, nb * S5_WIDTH), _F32), pl.BlockSpec((tl, S5_WIDTH), lambda i, t: (t, i)))
    return pl.pallas_call(
        _in_proj_kernel,
        grid=grid,
        in_specs=[pl.BlockSpec((bb, tl, D_MODEL), xmap), modspec(0), modspec(1),
                  pl.BlockSpec((None, 1, D_MODEL), lambda i, t: (l, 0, 0)),
                  pl.BlockSpec((None, D_MODEL, _P_COLS), lambda i, t: (l, 0, 0))],
        out_specs=[o[1] for o in outs],
        out_shape=[o[0] for o in outs],
        compiler_params=_params("arbitrary", "arbitrary"),
    )(x, mod, mod, P['norm1_g'], P['w_in_p'])


def _ssd_seq(j, z_ref, xbc_ref, dt_ref, cw_ref, cb_ref, dtb_ref, alog_ref, dexp_ref, ng_ref, eexp_ref,
             y_ref, sto_ref, cvo_ref, ext_scr, ybuf_scr, T, valid):
    u = xbc_ref[j]
    ext_scr[j, 0:SUBLANES, :] = cvo_ref[j]
    ext_scr[j, SUBLANES:SUBLANES + T, :] = u
    conv = cb_ref[...] + cw_ref[3:4, :] * u
    for back in (1, 2, 3):
        conv = conv + cw_ref[3 - back:4 - back, :] * ext_scr[j, SUBLANES - back:SUBLANES - back + T, :]
    cvo_ref[j] = ext_scr[j, valid:valid + SUBLANES, :]
    xbc = _silu(conv)
    xs = xbc[:, :SSD_WIDTH]
    row = _iota((T, 1), 0)
    dtv = _softplus(dt_ref[j] + dtb_ref[...])
    if valid < T:
        dtv = jnp.where(row < valid, dtv, 0.0)
    a_row = jnp.where(_iota((1, DT_PAD), 1) < SSD_HEADS, -jnp.exp(alog_ref[...]), 0.0)
    a = dtv * a_row
    yield
    tri_b = _iota((T, T), 0) >= _iota((T, T), 1)
    cs = _mm_sel(tri_b, a, _NN)
    eye = _iota((DT_PAD, DT_PAD), 0) == _iota((DT_PAD, DT_PAD), 1)
    xdt = xs * _mm_sel(eexp_ref[...], dtv, _NN, x_is_rhs=False)
    yield
    cs_t = _mm_sel(eye, cs, _NT)
    ecs = jnp.exp(cs)
    cs_last = cs[T - 1:T, :]
    dte = jnp.exp(cs_last - cs)
    elast = jnp.exp(cs_last)
    for g in range(SSD_GROUPS):
        bm = xbc[:, SSD_WIDTH + g * SSD_STATE:SSD_WIDTH + (g + 1) * SSD_STATE]
        cm = xbc[:, SSD_WIDTH + (SSD_GROUPS + g) * SSD_STATE:SSD_WIDTH + (SSD_GROUPS + g + 1) * SSD_STATE]
        cb = _dot_nt(cm, bm)
        for r in range(SSD_HEADS_PER_GROUP):
            if r == 0 or T >= 64:
                yield
            h = g * SSD_HEADS_PER_GROUP + r
            lo, hi = h * SSD_HEAD_DIM, (h + 1) * SSD_HEAD_DIM
            decay = jnp.exp(jnp.where(tri_b, cs[:, h:h + 1] - cs_t[h:h + 1, :], -jnp.inf))
            xdt_h = xdt[:, lo:hi]
            st = sto_ref[j, h]
            y_h = _dot(cb * decay, xdt_h) + ecs[:, h:h + 1] * _dot_nt(cm, st)
            ybuf_scr[j, :, lo:hi] = y_h
            sto_ref[j, h] = elast[:, h:h + 1] * st + _dot_tn(xdt_h * dte[:, h:h + 1], bm)
    yield
    y = ybuf_scr[j] + xs * dexp_ref[...]
    y = y * _silu(z_ref[j])
    y = y * lax.rsqrt(jnp.mean(y * y, axis=-1, keepdims=True) + NORM_EPS) * ng_ref[...]
    y_ref[j] = y


def _ssd_kernel(z_ref, xbc_ref, dt_ref, st_ref, cv_ref, cw_ref, cb_ref, dtb_ref, alog_ref, dexp_ref, ng_ref,
                eexp_ref, *rest, bb, T, valid, n_earlier):
    earlier = rest[:n_earlier]
    y_ref, sto_all, cvo_ref, ext_scr, ybuf_scr = rest[n_earlier:]
    sto_ref = sto_all.at[n_earlier] if n_earlier else sto_all

    @pl.when(pl.program_id(1) == 0)
    def _():
        sto_ref[...] = st_ref[...]
        cvo_ref[...] = cv_ref[...]
        for k, e_ref in enumerate(earlier):
            sto_all[k] = e_ref[...]

    seq = functools.partial(_ssd_seq, z_ref=z_ref, xbc_ref=xbc_ref, dt_ref=dt_ref, cw_ref=cw_ref, cb_ref=cb_ref,
                            dtb_ref=dtb_ref, alog_ref=alog_ref, dexp_ref=dexp_ref, ng_ref=ng_ref, eexp_ref=eexp_ref,
                            y_ref=y_ref, sto_ref=sto_ref, cvo_ref=cvo_ref, ext_scr=ext_scr, ybuf_scr=ybuf_scr,
                            T=T, valid=valid)
    _interleave(seq(j) for j in range(bb))


def _ssd(z, xbc, dt, st_ssd, st_conv8, P, l, *, bb, T, valid, earlier=()):
    nb, L, _ = z.shape
    grid = (nb // bb, L // T)
    tmap = lambda i, c: (i, c, 0)
    smap = lambda i, c: (i, 0, 0, 0)
    cvmap = lambda i, c: (i, 0, 0)
    layer = lambda r, n: pl.BlockSpec((None, r, n), lambda i, c: (l, 0, 0))
    state_blk = (bb, SSD_HEADS, SSD_HEAD_DIM, SSD_STATE)
    n_e = len(earlier)
    if n_e:
        state_out = (pl.BlockSpec((n_e + 1,) + state_blk, lambda i, c: (0, i, 0, 0, 0)),
                     jax.ShapeDtypeStruct((n_e + 1,) + st_ssd.shape[1:], _F32))
    else:
        state_out = (pl.BlockSpec(state_blk, smap), jax.ShapeDtypeStruct(st_ssd.shape[1:], _F32))
    return pl.pallas_call(
        functools.partial(_ssd_kernel, bb=bb, T=T, valid=valid, n_earlier=n_e),
        grid=grid,
        in_specs=[pl.BlockSpec((bb, T, SSD_WIDTH), tmap), pl.BlockSpec((bb, T, SSD_CONV_DIM), tmap),
                  pl.BlockSpec((bb, T, DT_PAD), tmap),
                  pl.BlockSpec((None, bb, SSD_HEADS, SSD_HEAD_DIM, SSD_STATE), lambda i, c: (l, i, 0, 0, 0)),
                  pl.BlockSpec((None, bb, SUBLANES, SSD_CONV_DIM), lambda i, c: (l, i, 0, 0)),
                  layer(SSD_CONV, SSD_CONV_DIM), layer(1, SSD_CONV_DIM), layer(1, DT_PAD), layer(1, DT_PAD),
                  layer(1, SSD_WIDTH), layer(1, SSD_WIDTH),
                  pl.BlockSpec((DT_PAD, SSD_WIDTH), lambda i, c: (0, 0))] + [pl.BlockSpec(state_blk, smap)] * n_e,
        out_specs=[pl.BlockSpec((bb, T, SSD_WIDTH), tmap), state_out[0],
                   pl.BlockSpec((bb, SUBLANES, SSD_CONV_DIM), cvmap)],
        out_shape=[jax.ShapeDtypeStruct((nb, L, SSD_WIDTH), _F32), state_out[1],
                   jax.ShapeDtypeStruct(st_conv8.shape[1:], _F32)],
        scratch_shapes=[pltpu.VMEM((bb, T + SUBLANES, SSD_CONV_DIM), _F32), pltpu.VMEM((bb, T, SSD_WIDTH), _F32)],
        compiler_params=_params("arbitrary", "arbitrary"),
    )(z, xbc, dt, st_ssd, st_conv8, P['ssd_conv_w'], P['ssd_conv_b'], P['ssd_dt_bias'], P['ssd_a_log'],
      P['ssd_d'], P['ssd_norm_g'], P['ssd_head_expand'], *earlier)


def _stack_heads(x):
    return jnp.concatenate([x[:, h * RWKV_HEAD_DIM:(h + 1) * RWKV_HEAD_DIM] for h in range(RWKV_HEADS)], axis=0)


def _unstack_heads(x, T):
    return jnp.concatenate([x[h * T:(h + 1) * T, :] for h in range(RWKV_HEADS)], axis=1)


def _interleave(gens):
    gens = list(gens)
    while gens:
        alive = []
        for g in gens:
            try:
                next(g)
                alive.append(g)
            except StopIteration:
                pass
        gens = alive


def _rwkv_masks(T):
    HT = RWKV_HEADS * T
    rr = _iota((HT, HT), 0)
    cc = _iota((HT, HT), 1)
    same = (rr // T) == (cc // T)
    rt, ct = _iota((T, T), 0), _iota((T, T), 1)
    head_of_row = _iota((HT, RWKV_WIDTH), 0) // T
    head_of_col = _iota((HT, RWKV_WIDTH), 1) // RWKV_HEAD_DIM
    return dict(strict=jnp.logical_and(same, rr > cc), incl=jnp.logical_and(same, rr >= cc),
                eye=(rr == cc).astype(_F32), hmask=head_of_row == head_of_col,
                strict_t=rt > ct, incl_t=rt >= ct, eye_t=(rt == ct).astype(_F32), tri=(rt >= ct).astype(_BF16))


def _rwkv_inverse(eye, mats, T):
    inv = [eye + n for n in mats]
    p = list(mats)
    for _ in range(max((T - 1).bit_length() - 1, 0)):
        yield
        p = [_dot(x, x, P_INV) for x in p]
        yield
        inv = [i + _dot(i, x, P_INV) for i, x in zip(inv, p)]
    return inv


def _rwkv_core_per_head(j, masks, so_ref, T, at, rt, bt, kt, v4f, w_end):
    cast = lambda x: x.astype(_BF16)
    a4, r4, b4, k4 = (cast(_stack_heads(t)) for t in (at, rt, bt, kt))
    v4 = cast(v4f)
    bw4, kw4 = (cast(_stack_heads(t * w_end)) for t in (bt, kt))
    heads = range(RWKV_HEADS)
    blk = lambda x, h: x[h * T:(h + 1) * T]
    yield
    n_ab = [jnp.where(masks['strict_t'], _dot_nt(blk(a4, h), blk(b4, h), P_AMAT), 0.0) for h in heads]
    n_ak = [cast(jnp.where(masks['strict_t'], _dot_nt(blk(a4, h), blk(k4, h), P_AMAT), 0.0)) for h in heads]
    m_rb = [cast(jnp.where(masks['incl_t'], _dot_nt(blk(r4, h), blk(b4, h), P_AMAT), 0.0)) for h in heads]
    m_rk = [cast(jnp.where(masks['incl_t'], _dot_nt(blk(r4, h), blk(k4, h), P_AMAT), 0.0)) for h in heads]
    inv = yield from _rwkv_inverse(masks['eye_t'], n_ab, T)
    yield
    s0 = [so_ref[j, h] for h in heads]
    rhs = [_dot_nt(blk(a4, h), s0[h], P_S0) + _dot(n_ak[h], blk(v4, h), P_INTRA) for h in heads]
    o_own = [_dot_nt(blk(r4, h), s0[h], P_S0) + _dot(m_rk[h], blk(v4, h), P_INTRA) for h in heads]
    yield
    u = [cast(_dot(inv[h], rhs[h], P_SOLVE)) for h in heads]
    yield
    o4 = jnp.concatenate([o_own[h] + _dot(m_rb[h], u[h], P_INTRA) for h in heads], axis=0)
    for h in heads:
        so_ref[j, h] = (s0[h] * w_end[:, h * RWKV_HEAD_DIM:(h + 1) * RWKV_HEAD_DIM]
                        + _dot_tn(u[h], blk(bw4, h), P_STATE) + _dot_tn(blk(v4, h), blk(kw4, h), P_STATE))
    yield
    return o4


def _rwkv_core_stacked(j, masks, so_ref, T, at, rt, bt, kt, v4f, w_end):
    strict, incl, hmask = masks['strict'], masks['incl'], masks['hmask']
    a4, r4, b4, k4 = (_stack_heads(t) for t in (at, rt, bt, kt))
    yield
    n_ab = jnp.where(strict, _dot_nt(a4, b4, P_AMAT), 0.0)
    n_ak = jnp.where(strict, _dot_nt(a4, k4, P_AMAT), 0.0)
    m_rb = jnp.where(incl, _dot_nt(r4, b4, P_AMAT), 0.0)
    m_rk = jnp.where(incl, _dot_nt(r4, k4, P_AMAT), 0.0)
    (inv,) = yield from _rwkv_inverse(masks['eye'], [n_ab], T)

    def wide(x):
        return jnp.where(hmask, jnp.concatenate([x] * RWKV_HEADS, axis=0), 0.0)

    yield
    s0 = jnp.concatenate([so_ref[j, h] for h in range(RWKV_HEADS)], axis=1)
    rhs = _dot_nt(wide(at), s0, P_S0) + _dot(n_ak, v4f, P_INTRA)
    o4 = _dot_nt(wide(rt), s0, P_S0) + _dot(m_rk, v4f, P_INTRA)
    yield
    u4 = _dot(inv, rhs, P_SOLVE)
    yield
    o4 = o4 + _dot(m_rb, u4, P_INTRA)
    s1 = (s0 * w_end + _dot_tn(u4, wide(bt * w_end), P_STATE) + _dot_tn(v4f, wide(kt * w_end), P_STATE))
    for h in range(RWKV_HEADS):
        so_ref[j, h] = s1[:, h * RWKV_HEAD_DIM:(h + 1) * RWKV_HEAD_DIM]
    yield
    return o4


def _rwkv_seq(j, masks, f_ref, mu_ref, w0_ref, w2_ref, a0_ref, a2_ref, g2_ref, kk_ref, ka_ref, rk_ref, lng_ref,
              lnb_ref, y_ref, so_ref, sho_ref, ext_scr, T, valid, row0=0):
    f = f_ref[j, row0:row0 + T, :]
    ext_scr[j, 0:SUBLANES, :] = sho_ref[j]
    ext_scr[j, SUBLANES:SUBLANES + T, :] = f
    prev = ext_scr[j, SUBLANES - 1:SUBLANES - 1 + T, :]
    sho_ref[j] = ext_scr[j, valid:valid + SUBLANES, :]
    fm = f + (prev - f) * mu_ref[...]
    r = fm[:, 0:256]
    k = fm[:, 256:512]
    v = fm[:, 512:768]
    lora = fm[:, 768:896]
    gl = fm[:, 896:1024]
    w_log = -_softplus(-(w0_ref[...] + _dot(jnp.tanh(lora), w2_ref[...]))) - 0.5
    logw = -jnp.exp(w_log)
    a = _sigmoid(a0_ref[...] + _dot(lora, a2_ref[...]))
    g = _dot(_sigmoid(gl), g2_ref[...])
    k2 = k * (1.0 + (a - 1.0) * ka_ref[...])
    kk4 = _stack_heads(k * kk_ref[...])
    kk4 = kk4 * lax.rsqrt(jnp.maximum(jnp.sum(kk4 * kk4, axis=-1, keepdims=True), 1e-24))
    kkn = _unstack_heads(kk4, T)
    if valid < T:
        live = _iota((T, 1), 0) < valid
        logw = jnp.where(live, logw, 0.0)
        kkn = jnp.where(live, kkn, 0.0)
        k2 = jnp.where(live, k2, 0.0)
    yield
    cw = _mm_sel(masks['tri'], logw, _NN)
    w_inc = jnp.exp(cw)
    w_exc = jnp.exp(cw - logw)
    w_inv = jnp.exp(-cw)
    w_end = w_inc[T - 1:T, :]
    at = -kkn * w_exc
    rt = r * w_inc
    bt = kkn * a * w_inv
    kt = k2 * w_inv
    v4f = _stack_heads(v)
    core = _rwkv_core_per_head if T % 16 == 0 else _rwkv_core_stacked
    o4 = yield from core(j, masks, so_ref, T, at, rt, bt, kt, v4f, w_end)

    mean = jnp.mean(o4, axis=-1, keepdims=True)
    var = jnp.mean(jnp.square(o4 - mean), axis=-1, keepdims=True)
    o4 = (o4 - mean) * lax.rsqrt(var + RWKV_LN_EPS)
    bonus4 = jnp.sum(_stack_heads(r * k2 * rk_ref[...]), axis=-1, keepdims=True) * v4f
    o = _unstack_heads(o4, T) * lng_ref[...] + lnb_ref[...] + _unstack_heads(bonus4, T)
    y_ref[j, row0:row0 + T, :] = o * g


def _rwkv_kernel(f_ref, s_ref, sh_ref, mu_ref, w0_ref, w2_ref, a0_ref, a2_ref, g2_ref, kk_ref, ka_ref, rk_ref,
                 lng_ref, lnb_ref, *rest, bb, T, valid, n_earlier):
    earlier = rest[:n_earlier]
    y_ref, so_all, sho_ref, ext_scr = rest[n_earlier:]
    so_ref = so_all.at[n_earlier] if n_earlier else so_all

    @pl.when(pl.program_id(1) == 0)
    def _():
        so_ref[...] = s_ref[...]
        sho_ref[...] = sh_ref[...]
        for k, e_ref in enumerate(earlier):
            so_all[k] = e_ref[...]

    seq = functools.partial(_rwkv_seq, f_ref=f_ref, mu_ref=mu_ref, w0_ref=w0_ref, w2_ref=w2_ref, a0_ref=a0_ref,
                            a2_ref=a2_ref, g2_ref=g2_ref, kk_ref=kk_ref, ka_ref=ka_ref, rk_ref=rk_ref,
                            lng_ref=lng_ref, lnb_ref=lnb_ref, y_ref=y_ref, so_ref=so_ref, sho_ref=sho_ref,
                            ext_scr=ext_scr, T=T, valid=valid)
    masks = _rwkv_masks(T)
    _interleave(seq(j, masks) for j in range(bb))


def _rwkv(f, st_rwkv, shift8, P, l, *, bb, T, valid, earlier=()):
    nb, L, _ = f.shape
    hd = RWKV_HEAD_DIM
    state_blk = (bb, RWKV_HEADS, hd, hd)
    smap = lambda i, c: (i, 0, 0, 0)
    n_e = len(earlier)
    if n_e:
        state_out = (pl.BlockSpec((n_e + 1,) + state_blk, lambda i, c: (0, i, 0, 0, 0)),
                     jax.ShapeDtypeStruct((n_e + 1,) + st_rwkv.shape[1:], _F32))
    else:
        state_out = (pl.BlockSpec(state_blk, smap), jax.ShapeDtypeStruct(st_rwkv.shape[1:], _F32))
    grid = (nb // bb, L // T)
    tmap = lambda i, c: (i, c, 0)
    bmap = lambda i, c: (i, 0, 0)
    layer = lambda r, n: pl.BlockSpec((None, r, n), lambda i, c: (l, 0, 0))
    row = lambda n: layer(1, n)
    lora_w = RWKV_DECAY_LORA + RWKV_A_LORA
    return pl.pallas_call(
        functools.partial(_rwkv_kernel, bb=bb, T=T, valid=valid, n_earlier=n_e),
        grid=grid,
        in_specs=[pl.BlockSpec((bb, T, RWKV_PROJ), tmap),
                  pl.BlockSpec((None, bb, RWKV_HEADS, hd, hd), lambda i, c: (l, i, 0, 0, 0)),
                  pl.BlockSpec((None, bb, SUBLANES, RWKV_PROJ), lambda i, c: (l, i, 0, 0)),
                  row(RWKV_PROJ), row(RWKV_WIDTH), layer(lora_w, RWKV_WIDTH),
                  row(RWKV_WIDTH), layer(lora_w, RWKV_WIDTH), layer(RWKV_GATE_LORA, RWKV_WIDTH),
                  row(RWKV_WIDTH), row(RWKV_WIDTH), row(RWKV_WIDTH), row(RWKV_WIDTH), row(RWKV_WIDTH)]
        + [pl.BlockSpec(state_blk, smap)] * n_e,
        out_specs=[pl.BlockSpec((bb, T, RWKV_WIDTH), tmap), state_out[0],
                   pl.BlockSpec((bb, SUBLANES, RWKV_PROJ), bmap)],
        out_shape=[jax.ShapeDtypeStruct((nb, L, RWKV_WIDTH), _F32), state_out[1],
                   jax.ShapeDtypeStruct(shift8.shape[1:], _F32)],
        scratch_shapes=[pltpu.VMEM((bb, T + SUBLANES, RWKV_PROJ), _F32)],
        compiler_params=_params("arbitrary", "arbitrary"),
    )(f, st_rwkv, shift8, P['rwkv_mu'], P['rwkv_w0'], P['rwkv_w2'], P['rwkv_a0'], P['rwkv_a2'], P['rwkv_g2'],
      P['rwkv_k_k'], P['rwkv_k_a'], P['rwkv_r_k'], P['rwkv_ln_g'], P['rwkv_ln_b'], *earlier)


def _mix_kernel(z_ref, xbc_ref, dt_ref, st_ref, cv_ref, cw_ref, cb_ref, dtb_ref, alog_ref, dexp_ref, ng_ref, eexp_ref,
                f_ref, s_ref, sh_ref, mu_ref, w0_ref, w2_ref, a0_ref, a2_ref, g2_ref, kk_ref, ka_ref, rk_ref,
                lng_ref, lnb_ref, *rest, bb, T, RT, n_earlier):
    e_ssd, e_rw = rest[:n_earlier], rest[n_earlier:2 * n_earlier]
    y_ref, sto_all, cvo_ref, yrw_ref, so_all, sho_ref, ext_scr, ybuf_scr, rext_scr = rest[2 * n_earlier:]
    sto_ref = sto_all.at[n_earlier] if n_earlier else sto_all
    so_ref = so_all.at[n_earlier] if n_earlier else so_all

    @pl.when(pl.program_id(1) == 0)
    def _():
        sto_ref[...] = st_ref[...]
        cvo_ref[...] = cv_ref[...]
        so_ref[...] = s_ref[...]
        sho_ref[...] = sh_ref[...]
        for k in range(n_earlier):
            sto_all[k] = e_ssd[k][...]
            so_all[k] = e_rw[k][...]

    ssd = functools.partial(_ssd_seq, z_ref=z_ref, xbc_ref=xbc_ref, dt_ref=dt_ref, cw_ref=cw_ref, cb_ref=cb_ref,
                            dtb_ref=dtb_ref, alog_ref=alog_ref, dexp_ref=dexp_ref, ng_ref=ng_ref, eexp_ref=eexp_ref,
                            y_ref=y_ref, sto_ref=sto_ref, cvo_ref=cvo_ref, ext_scr=ext_scr, ybuf_scr=ybuf_scr,
                            T=T, valid=T)
    rwkv = functools.partial(_rwkv_seq, f_ref=f_ref, mu_ref=mu_ref, w0_ref=w0_ref, w2_ref=w2_ref, a0_ref=a0_ref,
                             a2_ref=a2_ref, g2_ref=g2_ref, kk_ref=kk_ref, ka_ref=ka_ref, rk_ref=rk_ref,
                             lng_ref=lng_ref, lnb_ref=lnb_ref, y_ref=yrw_ref, so_ref=so_ref, sho_ref=sho_ref,
                             ext_scr=rext_scr, T=RT, valid=RT)
    masks = _rwkv_masks(RT)

    def rwkv_chunks(j):
        for c in range(T // RT):
            yield from rwkv(j, masks, row0=c * RT)

    gens = []
    for j in range(bb):
        gens += [ssd(j), rwkv_chunks(j)]
    _interleave(gens)


def _mix(z, xbc, dt, f, st_ssd, st_conv8, st_rwkv, shift8, P, l, *, bb, T, RT, earlier_ssd=(), earlier_rwkv=()):
    nb, L, _ = z.shape
    hd = RWKV_HEAD_DIM
    grid = (nb // bb, L // T)
    tmap = lambda i, c: (i, c, 0)
    bmap = lambda i, c: (i, 0, 0)
    smap = lambda i, c: (i, 0, 0, 0)
    layer = lambda r, n: pl.BlockSpec((None, r, n), lambda i, c: (l, 0, 0))
    row = lambda n: layer(1, n)
    lora_w = RWKV_DECAY_LORA + RWKV_A_LORA
    ssd_blk = (bb, SSD_HEADS, SSD_HEAD_DIM, SSD_STATE)
    rw_blk = (bb, RWKV_HEADS, hd, hd)
    n_e = len(earlier_ssd)
    assert len(earlier_rwkv) == n_e

    def state_out(blk, full_shape):
        if n_e:
            return (pl.BlockSpec((n_e + 1,) + blk, lambda i, c: (0, i, 0, 0, 0)),
                    jax.ShapeDtypeStruct((n_e + 1,) + full_shape, _F32))
        return pl.BlockSpec(blk, smap), jax.ShapeDtypeStruct(full_shape, _F32)

    ssd_out, rw_out = state_out(ssd_blk, st_ssd.shape[1:]), state_out(rw_blk, st_rwkv.shape[1:])
    outs = pl.pallas_call(
        functools.partial(_mix_kernel, bb=bb, T=T, RT=RT, n_earlier=n_e),
        grid=grid,
        in_specs=[pl.BlockSpec((bb, T, SSD_WIDTH), tmap), pl.BlockSpec((bb, T, SSD_CONV_DIM), tmap),
                  pl.BlockSpec((bb, T, DT_PAD), tmap),
                  pl.BlockSpec((None,) + ssd_blk, lambda i, c: (l, i, 0, 0, 0)),
                  pl.BlockSpec((None, bb, SUBLANES, SSD_CONV_DIM), lambda i, c: (l, i, 0, 0)),
                  layer(SSD_CONV, SSD_CONV_DIM), layer(1, SSD_CONV_DIM), layer(1, DT_PAD), layer(1, DT_PAD),
                  layer(1, SSD_WIDTH), layer(1, SSD_WIDTH), pl.BlockSpec((DT_PAD, SSD_WIDTH), lambda i, c: (0, 0)),
                  pl.BlockSpec((bb, T, RWKV_PROJ), tmap),
                  pl.BlockSpec((None,) + rw_blk, lambda i, c: (l, i, 0, 0, 0)),
                  pl.BlockSpec((None, bb, SUBLANES, RWKV_PROJ), lambda i, c: (l, i, 0, 0)),
                  row(RWKV_PROJ), row(RWKV_WIDTH), layer(lora_w, RWKV_WIDTH),
                  row(RWKV_WIDTH), layer(lora_w, RWKV_WIDTH), layer(RWKV_GATE_LORA, RWKV_WIDTH),
                  row(RWKV_WIDTH), row(RWKV_WIDTH), row(RWKV_WIDTH), row(RWKV_WIDTH), row(RWKV_WIDTH)]
        + [pl.BlockSpec(ssd_blk, smap)] * n_e + [pl.BlockSpec(rw_blk, smap)] * n_e,
        out_specs=[pl.BlockSpec((bb, T, SSD_WIDTH), tmap), ssd_out[0],
                   pl.BlockSpec((bb, SUBLANES, SSD_CONV_DIM), bmap),
                   pl.BlockSpec((bb, T, RWKV_WIDTH), tmap), rw_out[0],
                   pl.BlockSpec((bb, SUBLANES, RWKV_PROJ), bmap)],
        out_shape=[jax.ShapeDtypeStruct((nb, L, SSD_WIDTH), _F32), ssd_out[1],
                   jax.ShapeDtypeStruct(st_conv8.shape[1:], _F32),
                   jax.ShapeDtypeStruct((nb, L, RWKV_WIDTH), _F32), rw_out[1],
                   jax.ShapeDtypeStruct(shift8.shape[1:], _F32)],
        scratch_shapes=[pltpu.VMEM((bb, T + SUBLANES, SSD_CONV_DIM), _F32), pltpu.VMEM((bb, T, SSD_WIDTH), _F32),
                        pltpu.VMEM((bb, RT + SUBLANES, RWKV_PROJ), _F32)],
        compiler_params=_params("arbitrary", "arbitrary"),
    )(z, xbc, dt, st_ssd, st_conv8, P['ssd_conv_w'], P['ssd_conv_b'], P['ssd_dt_bias'], P['ssd_a_log'],
      P['ssd_d'], P['ssd_norm_g'], P['ssd_head_expand'],
      f, st_rwkv, shift8, P['rwkv_mu'], P['rwkv_w0'], P['rwkv_w2'], P['rwkv_a0'], P['rwkv_a2'], P['rwkv_g2'],
      P['rwkv_k_k'], P['rwkv_k_a'], P['rwkv_r_k'], P['rwkv_ln_g'], P['rwkv_ln_b'], *earlier_ssd, *earlier_rwkv)
    return outs


def _s5_kernel(u_ref, sre_ref, sim_ref, lre_ref, lim_ref, dt_ref, bre_ref, bim_ref, cre_ref, cim_ref, d_ref,
               gw_ref, gb_ref, y_ref, hre_ref, him_ref, hre_scr, him_scr, *, nb, tl, final_step):
    @pl.when(pl.program_id(0) == 0)
    def _():
        hre_ref[...] = sre_ref[...]
        him_ref[...] = sim_ref[...]

    lre, lim, dt = lre_ref[...], lim_ref[...], jnp.exp(dt_ref[...])
    mag = jnp.exp(lre * dt)
    ang = lim * dt
    ab_re, ab_im = mag * jnp.cos(ang), mag * jnp.sin(ang)
    den = jnp.square(lre) + jnp.square(lim)
    q_re = ((ab_re - 1.0) * lre + ab_im * lim) / den
    q_im = (ab_im * lre - (ab_re - 1.0) * lim) / den
    b_re, b_im = bre_ref[...], bim_ref[...]
    bb_re = q_re * b_re - q_im * b_im
    bb_im = q_re * b_im + q_im * b_re
    u = u_ref[...]
    hre_scr[0:nb, :] = hre_ref[...]
    him_scr[0:nb, :] = him_ref[...]
    hre_scr[nb:, :] = _dot(u, bb_re)
    him_scr[nb:, :] = _dot(u, bb_im)
    ar = jnp.broadcast_to(ab_re, (nb, S5_FLAT))
    ai = jnp.broadcast_to(ab_im, (nb, S5_FLAT))

    if nb <= SUBLANES:
        def step(s, carry):
            pr, pi = carry
            cur = pl.ds(pl.multiple_of((s + 1) * nb, nb), nb)
            nr = ar * pr - ai * pi + hre_scr[cur, :]
            ni = ar * pi + ai * pr + him_scr[cur, :]
            hre_scr[cur, :] = nr
            him_scr[cur, :] = ni
            return nr, ni

        lax.fori_loop(0, tl, step, (hre_ref[...], him_ref[...]), unroll=2)
    else:
        def step(s, carry):
            prev = pl.ds(pl.multiple_of(s * nb, nb), nb)
            cur = pl.ds(pl.multiple_of((s + 1) * nb, nb), nb)
            pr, pi = hre_scr[prev, :], him_scr[prev, :]
            hre_scr[cur, :] = ar * pr - ai * pi + hre_scr[cur, :]
            him_scr[cur, :] = ar * pi + ai * pr + him_scr[cur, :]
            return carry

        lax.fori_loop(0, tl, step, 0)
    hre_ref[...] = hre_scr[(final_step + 1) * nb:(final_step + 2) * nb, :]
    him_ref[...] = him_scr[(final_step + 1) * nb:(final_step + 2) * nb, :]
    y = _dot(hre_scr[nb:, :], cre_ref[...]) - _dot(him_scr[nb:, :], cim_ref[...]) + d_ref[...] * u
    y = 0.5 * y * (1.0 + jnp.tanh(0.7978845608028654 * (y + 0.044715 * (y * y * y))))
    y_ref[...] = y * _sigmoid(_dot(y, gw_ref[...]) + gb_ref[...])


def _s5(u_tm, st_re, st_im, P, l, *, nb, tl, final_step):
    rows = u_tm.shape[0]
    L = rows // nb
    layer = lambda r, n: pl.BlockSpec((None, r, n), lambda t: (l, 0, 0))
    row = lambda n: layer(1, n)
    full = lambda a, b: pl.BlockSpec((a, b), lambda t: (0, 0))
    state = pl.BlockSpec((None, nb, S5_FLAT), lambda t: (l, 0, 0))
    return pl.pallas_call(
        functools.partial(_s5_kernel, nb=nb, tl=tl, final_step=final_step),
        grid=(L // tl,),
        in_specs=[pl.BlockSpec((tl * nb, S5_WIDTH), lambda t: (t, 0)), state, state,
                  row(S5_FLAT), row(S5_FLAT), row(S5_FLAT),
                  layer(S5_WIDTH, S5_FLAT), layer(S5_WIDTH, S5_FLAT), layer(S5_FLAT, S5_WIDTH),
                  layer(S5_FLAT, S5_WIDTH), row(S5_WIDTH), layer(S5_WIDTH, S5_WIDTH), row(S5_WIDTH)],
        out_specs=[pl.BlockSpec((tl * nb, S5_WIDTH), lambda t: (t, 0)), full(nb, S5_FLAT), full(nb, S5_FLAT)],
        out_shape=[jax.ShapeDtypeStruct((rows, S5_WIDTH), _F32),
                   jax.ShapeDtypeStruct((nb, S5_FLAT), _F32), jax.ShapeDtypeStruct((nb, S5_FLAT), _F32)],
        scratch_shapes=[pltpu.VMEM(((tl + 1) * nb, S5_FLAT), _F32), pltpu.VMEM(((tl + 1) * nb, S5_FLAT), _F32)],
        compiler_params=_params("arbitrary"),
    )(u_tm, st_re, st_im, P['s5_a_re'], P['s5_a_im'], P['s5_log_dt'], P['s5_b_re'], P['s5_b_im'], P['s5_c_re'],
      P['s5_c_im'], P['s5_d'], P['s5_glu_w'], P['s5_glu_b'])


def _out_mlp_kernel(x_ref, yssd_ref, yrw_ref, ys5_ref, g1_ref, sh2_ref, sc2_ref, g2_ref, n2g_ref, fg_ref,
                    wo_ref, w1_ref, w2_ref, o_ref, hid_scr, *, final_norm):
    bb, tl, _ = x_ref.shape
    rows = bb * tl
    flat = lambda ref: ref[...].reshape(rows, ref.shape[-1]).astype(_BF16)
    mix = (_dot(flat(yssd_ref), wo_ref[0:512, :]) + _dot(flat(yrw_ref), wo_ref[512:768, :])
           + _dot(flat(ys5_ref), wo_ref[768:1024, :]))
    x1 = x_ref[...] + g1_ref[...] * mix.reshape(bb, tl, D_MODEL)
    xn = x1 * lax.rsqrt(jnp.mean(x1 * x1, axis=-1, keepdims=True) + NORM_EPS)
    h2 = (xn * n2g_ref[...] * (1.0 + sc2_ref[...]) + sh2_ref[...]).reshape(rows, D_MODEL).astype(_BF16)
    for c in range(D_FF // FF_TILE):
        cols = slice(c * FF_TILE, (c + 1) * FF_TILE)
        hid_scr[:, cols] = jnp.square(jnp.maximum(_dot(h2, w1_ref[:, cols]), 0.0)).astype(_BF16)
    ff = _dot(hid_scr[...], w2_ref[...])
    out = x1 + g2_ref[...] * ff.reshape(bb, tl, D_MODEL)
    if final_norm:
        out = out * lax.rsqrt(jnp.mean(out * out, axis=-1, keepdims=True) + NORM_EPS) * fg_ref[...]
    o_ref[...] = out


def _out_mlp(x, y_ssd, y_rw, y_s5, mod, P, l, *, bb, tl, time_major_s5, final_norm):
    nb, L, _ = x.shape
    grid = (nb // bb, L // tl)
    xmap = lambda i, t: (i, t, 0)
    cmap = lambda i, t: (0, 0)
    wmap = lambda i, t: (l, 0, 0)
    once = pl.Buffered(1)

    def modspec(j):
        return pl.BlockSpec((bb, 1, D_MODEL), lambda i, t: (i, 0, j))

    s5_spec = (pl.BlockSpec((tl, S5_WIDTH), lambda i, t: (t, i)) if time_major_s5
               else pl.BlockSpec((bb, tl, S5_WIDTH), xmap))
    return pl.pallas_call(
        functools.partial(_out_mlp_kernel, final_norm=final_norm),
        grid=grid,
        in_specs=[pl.BlockSpec((bb, tl, D_MODEL), xmap), pl.BlockSpec((bb, tl, SSD_WIDTH), xmap),
                  pl.BlockSpec((bb, tl, RWKV_WIDTH), xmap), s5_spec,
                  modspec(2), modspec(3), modspec(4), modspec(5),
                  pl.BlockSpec((None, 1, D_MODEL), wmap), pl.BlockSpec((1, D_MODEL), cmap),
                  pl.BlockSpec((None, D_MODEL, D_MODEL), wmap, pipeline_mode=once),
                  pl.BlockSpec((None, D_MODEL, D_FF), wmap, pipeline_mode=once),
                  pl.BlockSpec((None, D_FF, D_MODEL), wmap, pipeline_mode=once)],
        out_specs=pl.BlockSpec((bb, tl, D_MODEL), xmap),
        out_shape=jax.ShapeDtypeStruct(x.shape, _F32),
        scratch_shapes=[pltpu.VMEM((bb * tl, D_FF), _BF16)],
        compiler_params=_params("arbitrary", "arbitrary"),
    )(x, y_ssd, y_rw, y_s5, mod, mod, mod, mod, P['norm2_g'], P['final_g'], P['w_out'], P['mlp_w1'], P['mlp_w2'])


def _trunk(x, mod, states, P, valid, *, prompt):
    st_ssd, st_conv, st_rwkv, st_shift, st_re, st_im = states
    nb, L, _ = x.shape
    if prompt:
        bb, tl, mlp_tl = 1, min(IN_ROW_TILE, L), min(MLP_ROW_TILE, L)
        ssd_t, rwkv_t, s5_tl = min(SSD_CHUNK, L), min(RWKV_CHUNK, L), min(S5_TILE, L)
        ssd_bb, rwkv_bb = min(nb, SSD_PROMPT_SEQS), min(nb, RWKV_PROMPT_SEQS)
    else:
        bb, tl, mlp_tl = min(nb, ROW_TILE // L), L, L
        ssd_t = rwkv_t = s5_tl = L
        ssd_bb = rwkv_bb = min(nb, SAMPLE_SEQS)
    new = ([], [], [], [], [], [])
    for l in range(DEPTH):
        z, xbc, frw, us5, dt = _in_proj(x, mod[l], P, l, bb=bb, tl=tl, time_major_s5=prompt)
        v = valid
        last = l == DEPTH - 1
        e_ssd, e_rwkv = (tuple(new[0]), tuple(new[2])) if last else ((), ())
        if prompt and ssd_t % rwkv_t == 0:
            y_ssd, n_ssd, n_conv, y_rw, n_rwkv, n_shift = _mix(
                z, xbc, dt, frw, st_ssd, st_conv, st_rwkv, st_shift, P, l, bb=min(nb, MIX_PROMPT_SEQS), T=ssd_t,
                RT=rwkv_t, earlier_ssd=e_ssd, earlier_rwkv=e_rwkv)
        else:
            y_ssd, n_ssd, n_conv = _ssd(z, xbc, dt, st_ssd, st_conv, P, l, bb=ssd_bb, T=ssd_t,
                                        valid=ssd_t if prompt else v, earlier=e_ssd)
            y_rw, n_rwkv, n_shift = _rwkv(frw, st_rwkv, st_shift, P, l, bb=rwkv_bb, T=rwkv_t,
                                          valid=rwkv_t if prompt else v, earlier=e_rwkv)
        if prompt:
            u_tm = us5.reshape(L * nb, S5_WIDTH)
        else:
            u_tm = jnp.transpose(us5, (1, 0, 2)).reshape(L * nb, S5_WIDTH)
        y_s5, n_re, n_im = _s5(u_tm, st_re, st_im, P, l, nb=nb, tl=s5_tl,
                               final_step=(s5_tl - 1) if prompt else (v - 1))
        if prompt:
            y_s5 = y_s5.reshape(L, nb * S5_WIDTH)
        else:
            y_s5 = jnp.transpose(y_s5.reshape(L, nb, S5_WIDTH), (1, 0, 2))
        x = _out_mlp(x, y_ssd, y_rw, y_s5, mod[l], P, l, bb=bb, tl=mlp_tl, time_major_s5=prompt,
                     final_norm=(l == DEPTH - 1))
        for lst, s in zip(new, (n_ssd, n_conv, n_rwkv, n_shift, n_re, n_im)):
            lst.append(s)
    by_kernel = (0, 2) if DEPTH > 1 else ()
    return x, [lst[-1] if k in by_kernel else jnp.stack(lst) for k, lst in enumerate(new)]


def _pack_states(st_ssd, st_conv, st_rwkv, st_shift, st_re, st_im):
    d, nb = st_ssd.shape[:2]
    conv8 = jnp.pad(st_conv, ((0, 0), (0, 0), (SUBLANES - (SSD_CONV - 1), 0), (0, 0)))
    shift8 = jnp.pad(st_shift[:, :, None, :], ((0, 0), (0, 0), (SUBLANES - 1, 0), (0, 0)))
    return (st_ssd, conv8, st_rwkv, shift8, st_re.reshape(d, nb, S5_FLAT), st_im.reshape(d, nb, S5_FLAT))


def _unpack_states(n_ssd, n_conv8, n_rwkv, n_shift8, n_re, n_im):
    d, nb = n_ssd.shape[:2]
    return (n_ssd, n_conv8[:, :, SUBLANES - (SSD_CONV - 1):, :], n_rwkv, n_shift8[:, :, SUBLANES - 1, :],
            n_re.reshape(d, nb, S5_GROUPS, S5_STATE), n_im.reshape(d, nb, S5_GROUPS, S5_STATE))


def _prepare_params(norm1_g, norm2_g, w_in, ssd_conv_w, ssd_conv_b, ssd_dt_bias, ssd_a_log, ssd_d, ssd_norm_g, rwkv_mu,
                    rwkv_w0, rwkv_w2, rwkv_a0, rwkv_a2, rwkv_g2, rwkv_k_k, rwkv_k_a, rwkv_r_k, rwkv_ln_g, rwkv_ln_b,
                    s5_a_re, s5_a_im, s5_log_dt, s5_b_re, s5_b_im, s5_c_re, s5_c_im, s5_d, s5_glu_w, s5_glu_b, w_out,
                    mlp_w1, mlp_w2, final_g):
    rows = lambda a: a.reshape(DEPTH, 1, -1)
    head_pad = ((0, 0), (0, DT_PAD - SSD_HEADS))
    eye = jnp.eye(S5_GROUPS, dtype=_F32)
    zeros_lora = jnp.zeros((DEPTH, RWKV_A_LORA, RWKV_WIDTH), _F32)
    w_in_p = jnp.concatenate(
        [w_in[:, :, _Z0:_DT0], w_in[:, :, _FRW0:_IN_COLS], w_in[:, :, _DT0:_FRW0],
         jnp.zeros((DEPTH, D_MODEL, DT_PAD - SSD_HEADS), w_in.dtype)], axis=-1).astype(_BF16)
    return dict(
        norm1_g=rows(norm1_g), norm2_g=rows(norm2_g), final_g=final_g.reshape(1, D_MODEL), w_in_p=w_in_p,
        w_out=w_out.astype(_BF16), mlp_w1=mlp_w1.astype(_BF16), mlp_w2=mlp_w2.astype(_BF16),
        ssd_conv_w=ssd_conv_w, ssd_conv_b=rows(ssd_conv_b), ssd_dt_bias=rows(jnp.pad(ssd_dt_bias, head_pad)),
        ssd_a_log=rows(jnp.pad(ssd_a_log, head_pad)), ssd_d=rows(jnp.repeat(ssd_d, SSD_HEAD_DIM, axis=-1)),
        ssd_norm_g=rows(ssd_norm_g),
        ssd_head_expand=jnp.repeat(jnp.eye(DT_PAD, SSD_HEADS, dtype=_F32), SSD_HEAD_DIM, axis=1),
        rwkv_mu=rows(rwkv_mu), rwkv_w0=rows(rwkv_w0), rwkv_a0=rows(rwkv_a0), rwkv_k_k=rows(rwkv_k_k),
        rwkv_k_a=rows(rwkv_k_a), rwkv_r_k=rows(rwkv_r_k), rwkv_ln_g=rows(rwkv_ln_g), rwkv_ln_b=rows(rwkv_ln_b),
        rwkv_w2=jnp.concatenate([rwkv_w2, zeros_lora], axis=1),
        rwkv_a2=jnp.concatenate([zeros_lora, rwkv_a2], axis=1),
        rwkv_g2=rwkv_g2,
        s5_a_re=rows(s5_a_re), s5_a_im=rows(s5_a_im), s5_log_dt=rows(jnp.repeat(s5_log_dt, S5_STATE, axis=-1)),
        s5_b_re=jnp.einsum('lgpc,gh->lgchp', s5_b_re, eye).reshape(DEPTH, S5_WIDTH, S5_FLAT),
        s5_b_im=jnp.einsum('lgpc,gh->lgchp', s5_b_im, eye).reshape(DEPTH, S5_WIDTH, S5_FLAT),
        s5_c_re=jnp.einsum('lgcp,gh->lgphc', s5_c_re, eye).reshape(DEPTH, S5_FLAT, S5_WIDTH),
        s5_c_im=jnp.einsum('lgcp,gh->lgphc', s5_c_im, eye).reshape(DEPTH, S5_FLAT, S5_WIDTH),
        s5_d=rows(s5_d), s5_glu_w=s5_glu_w, s5_glu_b=rows(s5_glu_b))


def kernel(x_prompt, x_sample, c_prompt, c_sample, state_ssd, state_ssd_conv, state_rwkv, state_rwkv_shift,
           state_s5_re, state_s5_im, ada_w, ada_b, norm1_g, norm2_g, w_in, ssd_conv_w, ssd_conv_b, ssd_dt_bias,
           ssd_a_log, ssd_d, ssd_norm_g, rwkv_mu, rwkv_w0, rwkv_w2, rwkv_a0, rwkv_a2, rwkv_g2, rwkv_k_k, rwkv_k_a,
           rwkv_r_k, rwkv_ln_g, rwkv_ln_b, s5_a_re, s5_a_im, s5_log_dt, s5_b_re, s5_b_im, s5_c_re, s5_c_im, s5_d,
           s5_glu_w, s5_glu_b, w_out, mlp_w1, mlp_w2, final_g):
    bp, lp, _ = x_prompt.shape
    bs, ls, _ = x_sample.shape
    ls_pad = -(-ls // SUBLANES) * SUBLANES
    P = _prepare_params(norm1_g, norm2_g, w_in, ssd_conv_w, ssd_conv_b, ssd_dt_bias, ssd_a_log, ssd_d, ssd_norm_g,
                        rwkv_mu, rwkv_w0, rwkv_w2, rwkv_a0, rwkv_a2, rwkv_g2, rwkv_k_k, rwkv_k_a, rwkv_r_k, rwkv_ln_g,
                        rwkv_ln_b, s5_a_re, s5_a_im, s5_log_dt, s5_b_re, s5_b_im, s5_c_re, s5_c_im, s5_d, s5_glu_w,
                        s5_glu_b, w_out, mlp_w1, mlp_w2, final_g)
    mod = _ada(jnp.concatenate([c_prompt, c_sample], axis=0), ada_w, ada_b)
    mod_p = mod[:, :bp].reshape(DEPTH, bp, 1, 6 * D_MODEL)
    mod_s = mod[:, bp:].reshape(DEPTH, bs, 1, 6 * D_MODEL)

    zeros = lambda shape: jnp.zeros((DEPTH, bp) + shape, _F32)
    p_states = _pack_states(zeros((SSD_HEADS, SSD_HEAD_DIM, SSD_STATE)), zeros((SSD_CONV - 1, SSD_CONV_DIM)),
                            zeros((RWKV_HEADS, RWKV_HEAD_DIM, RWKV_HEAD_DIM)), zeros((RWKV_PROJ,)),
                            zeros((S5_GROUPS, S5_STATE)), zeros((S5_GROUPS, S5_STATE)))
    y_prompt, sp = _trunk(x_prompt, mod_p, p_states, P, ls, prompt=True)

    s_states = _pack_states(state_ssd, state_ssd_conv, state_rwkv, state_rwkv_shift, state_s5_re, state_s5_im)
    x_s = jnp.pad(x_sample, ((0, 0), (0, ls_pad - ls), (0, 0)))
    y_sample, ss = _trunk(x_s, mod_s, s_states, P, ls, prompt=False)
    return (y_prompt, y_sample[:, :ls]) + _unpack_states(*sp) + _unpack_states(*ss)
```

```python
import functools

import jax
import jax.numpy as jnp
from jax import lax
from jax.experimental import pallas as pl
from jax.experimental.pallas import tpu as pltpu

D_MODEL = 1024
DEPTH = 2
SSD_WIDTH = 512
SSD_HEAD_DIM = 64
SSD_HEADS = 8
SSD_GROUPS = 2
SSD_HEADS_PER_GROUP = SSD_HEADS // SSD_GROUPS
SSD_STATE = 128
SSD_CONV = 4
SSD_CONV_DIM = SSD_WIDTH + 2 * SSD_GROUPS * SSD_STATE
RWKV_WIDTH = 256
RWKV_HEAD_DIM = 64
RWKV_HEADS = 4
RWKV_DECAY_LORA = 64
RWKV_A_LORA = 64
RWKV_GATE_LORA = 128
RWKV_PROJ = 3 * RWKV_WIDTH + RWKV_DECAY_LORA + RWKV_A_LORA + RWKV_GATE_LORA
RWKV_LN_EPS = 64e-5
S5_WIDTH = 256
S5_GROUP_CH = 16
S5_GROUPS = 16
S5_STATE = 64
S5_FLAT = S5_GROUPS * S5_STATE
D_FF = 4 * D_MODEL
NORM_EPS = 1e-6
DT_PAD = 128
SUBLANES = 8

_Z0, _XBC0, _DT0, _FRW0, _US50 = 0, SSD_WIDTH, SSD_WIDTH + SSD_CONV_DIM, SSD_WIDTH + SSD_CONV_DIM + SSD_HEADS, \
    SSD_WIDTH + SSD_CONV_DIM + SSD_HEADS + RWKV_PROJ
_IN_COLS = _US50 + S5_WIDTH
_PZ, _PXBC, _PFRW, _PUS5, _PDT = 0, 512, 1536, 2560, 2816
_P_COLS = _PDT + DT_PAD

SSD_CHUNK = 128
RWKV_CHUNK = 64
MIX_PROMPT_SEQS = 4
SAMPLE_SEQS = 8
S5_TILE = 256
ROW_TILE = 512
IN_ROW_TILE = 1024
MLP_ROW_TILE = 512
FF_TILE = 1024
VMEM_LIMIT = 56 * 1024 * 1024

_F32 = jnp.float32
_BF16 = jnp.bfloat16


_NN = (((1,), (0,)), ((), ()))
_NT = (((1,), (1,)), ((), ()))
_TN = (((0,), (0,)), ((), ()))


def _mm(a, b, dims):
    return lax.dot_general(a.astype(_BF16), b.astype(_BF16), dims, preferred_element_type=_F32)


def _dot(a, b):
    return _mm(a, b, _NN)


def _dot_nt(a, b):
    return _mm(a, b, _NT)


def _dot_tn(a, b):
    return _mm(a, b, _TN)


def _mm_sel(sel, x, dims, x_is_rhs=True):
    hi = x.astype(_BF16)
    rest = x - hi.astype(_F32)
    mid = rest.astype(_BF16)
    lo = (rest - mid.astype(_F32)).astype(_BF16)
    sel = sel.astype(_BF16)
    if x_is_rhs:
        dg = lambda y: lax.dot_general(sel, y, dims, preferred_element_type=_F32)
    else:
        dg = lambda y: lax.dot_general(y, sel, dims, preferred_element_type=_F32)
    return dg(hi) + (dg(mid) + dg(lo))


def _iota(shape, dim):
    return lax.broadcasted_iota(jnp.int32, shape, dim)


def _sigmoid(x):
    return 1.0 / (1.0 + jnp.exp(-x))


def _softplus(x):
    return jnp.maximum(x, 0.0) + jnp.log1p(jnp.exp(-jnp.abs(x)))


def _silu(x):
    return x * _sigmoid(x)


def _params(*sems):
    return pltpu.CompilerParams(dimension_semantics=sems, vmem_limit_bytes=VMEM_LIMIT)


def _ada_kernel(c_ref, w_ref, b_ref, o_ref):
    c = c_ref[...]
    s = _silu(c).astype(_BF16)
    o_ref[0] = _dot(s, w_ref[0].astype(_BF16)) + b_ref[0]


def _ada(c_all, ada_w, ada_b):
    rows = c_all.shape[0]
    tn = 1536
    return pl.pallas_call(
        _ada_kernel,
        grid=(DEPTH, 6 * D_MODEL // tn),
        in_specs=[pl.BlockSpec((rows, D_MODEL), lambda l, j: (0, 0)),
                  pl.BlockSpec((1, D_MODEL, tn), lambda l, j: (l, 0, j)),
                  pl.BlockSpec((1, 1, tn), lambda l, j: (l, 0, j))],
        out_specs=pl.BlockSpec((1, rows, tn), lambda l, j: (l, 0, j)),
        out_shape=jax.ShapeDtypeStruct((DEPTH, rows, 6 * D_MODEL), _F32),
        compiler_params=_params("arbitrary", "arbitrary"),
    )(c_all, ada_w, ada_b.reshape(DEPTH, 1, 6 * D_MODEL))


def _in_proj_kernel(x_ref, sh_ref, sc_ref, g_ref, w_ref, z_ref, xbc_ref, frw_ref, us5_ref, dt_ref):
    x = x_ref[...]
    bb, tl, _ = x.shape
    xn = x * lax.rsqrt(jnp.mean(x * x, axis=-1, keepdims=True) + NORM_EPS)
    h = xn * g_ref[...] * (1.0 + sc_ref[...]) + sh_ref[...]
    h = h.reshape(bb * tl, D_MODEL).astype(_BF16)
    for o_ref, lo, hi in ((z_ref, _PZ, _PXBC), (xbc_ref, _PXBC, _PFRW), (frw_ref, _PFRW, _PUS5),
                          (us5_ref, _PUS5, _PDT), (dt_ref, _PDT, _P_COLS)):
        o_ref[...] = _dot(h, w_ref[:, lo:hi]).reshape(o_ref.shape)


def _in_proj(x, mod, P, l, *, bb, tl, time_major_s5):
    nb, L, _ = x.shape
    grid = (nb // bb, L // tl)
    xmap = lambda i, t: (i, t, 0)

    def modspec(j):
        return pl.BlockSpec((bb, 1, D_MODEL), lambda i, t: (i, 0, j))

    def out(n):
        return jax.ShapeDtypeStruct((nb, L, n), _F32), pl.BlockSpec((bb, tl, n), xmap)

    outs = [out(SSD_WIDTH), out(SSD_CONV_DIM), out(RWKV_PROJ), out(S5_WIDTH), out(DT_PAD)]
    if time_major_s5:
        assert bb == 1
        outs[3] = (jax.ShapeDtypeStruct((L, nb * S5_WIDTH), _F32), pl.BlockSpec((tl, S5_WIDTH), lambda i, t: (t, i)))
    return pl.pallas_call(
        _in_proj_kernel,
        grid=grid,
        in_specs=[pl.BlockSpec((bb, tl, D_MODEL), xmap), modspec(0), modspec(1),
                  pl.BlockSpec((None, 1, D_MODEL), lambda i, t: (l, 0, 0)),
                  pl.BlockSpec((None, D_MODEL, _P_COLS), lambda i, t: (l, 0, 0))],
        out_specs=[o[1] for o in outs],
        out_shape=[o[0] for o in outs],
        compiler_params=_params("arbitrary", "arbitrary"),
    )(x, mod, mod, P['norm1_g'], P['w_in_p'])


def _ssd_seq(j, z_ref, xbc_ref, dt_ref, cw_ref, cb_ref, dtb_ref, alog_ref, dexp_ref, ng_ref, eexp_ref,
             y_ref, sto_ref, cvo_ref, ext_scr, ybuf_scr, T, valid):
    u = xbc_ref[j]
    ext_scr[j, 0:SUBLANES, :] = cvo_ref[j]
    ext_scr[j, SUBLANES:SUBLANES + T, :] = u
    conv = cb_ref[...] + cw_ref[3:4, :] * u
    for back in (1, 2, 3):
        conv = conv + cw_ref[3 - back:4 - back, :] * ext_scr[j, SUBLANES - back:SUBLANES - back + T, :]
    cvo_ref[j] = ext_scr[j, valid:valid + SUBLANES, :]
    xbc = _silu(conv)
    xs = xbc[:, :SSD_WIDTH]
    row = _iota((T, 1), 0)
    dtv = _softplus(dt_ref[j] + dtb_ref[...])
    if valid < T:
        dtv = jnp.where(row < valid, dtv, 0.0)
    a_row = jnp.where(_iota((1, DT_PAD), 1) < SSD_HEADS, -jnp.exp(alog_ref[...]), 0.0)
    a = dtv * a_row
    yield
    tri_b = _iota((T, T), 0) >= _iota((T, T), 1)
    cs = _mm_sel(tri_b, a, _NN)
    eye = _iota((DT_PAD, DT_PAD), 0) == _iota((DT_PAD, DT_PAD), 1)
    xdt = xs * _mm_sel(eexp_ref[...], dtv, _NN, x_is_rhs=False)
    yield
    cs_t = _mm_sel(eye, cs, _NT)
    ecs = jnp.exp(cs)
    cs_last = cs[T - 1:T, :]
    dte = jnp.exp(cs_last - cs)
    elast = jnp.exp(cs_last)
    for g in range(SSD_GROUPS):
        bm = xbc[:, SSD_WIDTH + g * SSD_STATE:SSD_WIDTH + (g + 1) * SSD_STATE]
        cm = xbc[:, SSD_WIDTH + (SSD_GROUPS + g) * SSD_STATE:SSD_WIDTH + (SSD_GROUPS + g + 1) * SSD_STATE]
        cb = _dot_nt(cm, bm)
        for r in range(SSD_HEADS_PER_GROUP):
            if r == 0 or T >= 64:
                yield
            h = g * SSD_HEADS_PER_GROUP + r
            lo, hi = h * SSD_HEAD_DIM, (h + 1) * SSD_HEAD_DIM
            decay = jnp.exp(jnp.where(tri_b, cs[:, h:h + 1] - cs_t[h:h + 1, :], -jnp.inf))
            xdt_h = xdt[:, lo:hi]
            st = sto_ref[j, h]
            y_h = _dot(cb * decay, xdt_h) + ecs[:, h:h + 1] * _dot_nt(cm, st)
            ybuf_scr[j, :, lo:hi] = y_h
            sto_ref[j, h] = elast[:, h:h + 1] * st + _dot_tn(xdt_h * dte[:, h:h + 1], bm)
    yield
    y = ybuf_scr[j] + xs * dexp_ref[...]
    y = y * _silu(z_ref[j])
    y = y * lax.rsqrt(jnp.mean(y * y, axis=-1, keepdims=True) + NORM_EPS) * ng_ref[...]
    y_ref[j] = y


def _ssd_kernel(z_ref, xbc_ref, dt_ref, st_ref, cv_ref, cw_ref, cb_ref, dtb_ref, alog_ref, dexp_ref, ng_ref,
                eexp_ref, *rest, bb, T, valid, n_earlier):
    earlier = rest[:n_earlier]
    y_ref, sto_all, cvo_ref, ext_scr, ybuf_scr = rest[n_earlier:]
    sto_ref = sto_all.at[n_earlier] if n_earlier else sto_all

    @pl.when(pl.program_id(1) == 0)
    def _():
        sto_ref[...] = st_ref[...]
        cvo_ref[...] = cv_ref[...]
        for k, e_ref in enumerate(earlier):
            sto_all[k] = e_ref[...]

    seq = functools.partial(_ssd_seq, z_ref=z_ref, xbc_ref=xbc_ref, dt_ref=dt_ref, cw_ref=cw_ref, cb_ref=cb_ref,
                            dtb_ref=dtb_ref, alog_ref=alog_ref, dexp_ref=dexp_ref, ng_ref=ng_ref, eexp_ref=eexp_ref,
                            y_ref=y_ref, sto_ref=sto_ref, cvo_ref=cvo_ref, ext_scr=ext_scr, ybuf_scr=ybuf_scr,
                            T=T, valid=valid)
    _interleave(seq(j) for j in range(bb))


def _ssd(z, xbc, dt, st_ssd, st_conv8, P, l, *, bb, T, valid, earlier=()):
    nb, L, _ = z.shape
    grid = (nb // bb, L // T)
    tmap = lambda i, c: (i, c, 0)
    smap = lambda i, c: (i, 0, 0, 0)
    cvmap = lambda i, c: (i, 0, 0)
    layer = lambda r, n: pl.BlockSpec((None, r, n), lambda i, c: (l, 0, 0))
    state_blk = (bb, SSD_HEADS, SSD_HEAD_DIM, SSD_STATE)
    n_e = len(earlier)
    if n_e:
        state_out = (pl.BlockSpec((n_e + 1,) + state_blk, lambda i, c: (0, i, 0, 0, 0)),
                     jax.ShapeDtypeStruct((n_e + 1,) + st_ssd.shape[1:], _F32))
    else:
        state_out = (pl.BlockSpec(state_blk, smap), jax.ShapeDtypeStruct(st_ssd.shape[1:], _F32))
    return pl.pallas_call(
        functools.partial(_ssd_kernel, bb=bb, T=T, valid=valid, n_earlier=n_e),
        grid=grid,
        in_specs=[pl.BlockSpec((bb, T, SSD_WIDTH), tmap), pl.BlockSpec((bb, T, SSD_CONV_DIM), tmap),
                  pl.BlockSpec((bb, T, DT_PAD), tmap),
                  pl.BlockSpec((None, bb, SSD_HEADS, SSD_HEAD_DIM, SSD_STATE), lambda i, c: (l, i, 0, 0, 0)),
                  pl.BlockSpec((None, bb, SUBLANES, SSD_CONV_DIM), lambda i, c: (l, i, 0, 0)),
                  layer(SSD_CONV, SSD_CONV_DIM), layer(1, SSD_CONV_DIM), layer(1, DT_PAD), layer(1, DT_PAD),
                  layer(1, SSD_WIDTH), layer(1, SSD_WIDTH),
                  pl.BlockSpec((DT_PAD, SSD_WIDTH), lambda i, c: (0, 0))] + [pl.BlockSpec(state_blk, smap)] * n_e,
        out_specs=[pl.BlockSpec((bb, T, SSD_WIDTH), tmap), state_out[0],
                   pl.BlockSpec((bb, SUBLANES, SSD_CONV_DIM), cvmap)],
        out_shape=[jax.ShapeDtypeStruct((nb, L, SSD_WIDTH), _F32), state_out[1],
                   jax.ShapeDtypeStruct(st_conv8.shape[1:], _F32)],
        scratch_shapes=[pltpu.VMEM((bb, T + SUBLANES, SSD_CONV_DIM), _F32), pltpu.VMEM((bb, T, SSD_WIDTH), _F32)],
        compiler_params=_params("arbitrary", "arbitrary"),
    )(z, xbc, dt, st_ssd, st_conv8, P['ssd_conv_w'], P['ssd_conv_b'], P['ssd_dt_bias'], P['ssd_a_log'],
      P['ssd_d'], P['ssd_norm_g'], P['ssd_head_expand'], *earlier)


def _stack_heads(x):
    return jnp.concatenate([x[:, h * RWKV_HEAD_DIM:(h + 1) * RWKV_HEAD_DIM] for h in range(RWKV_HEADS)], axis=0)


def _unstack_heads(x, T):
    return jnp.concatenate([x[h * T:(h + 1) * T, :] for h in range(RWKV_HEADS)], axis=1)


def _interleave(gens):
    gens = list(gens)
    while gens:
        alive = []
        for g in gens:
            try:
                next(g)
                alive.append(g)
            except StopIteration:
                pass
        gens = alive


def _rwkv_masks(T):
    HT = RWKV_HEADS * T
    rr = _iota((HT, HT), 0)
    cc = _iota((HT, HT), 1)
    same = (rr // T) == (cc // T)
    rt, ct = _iota((T, T), 0), _iota((T, T), 1)
    head_of_row = _iota((HT, RWKV_WIDTH), 0) // T
    head_of_col = _iota((HT, RWKV_WIDTH), 1) // RWKV_HEAD_DIM
    return dict(strict=jnp.logical_and(same, rr > cc), incl=jnp.logical_and(same, rr >= cc),
                eye=(rr == cc).astype(_F32), hmask=head_of_row == head_of_col,
                strict_t=rt > ct, incl_t=rt >= ct, eye_t=(rt == ct).astype(_F32), tri=(rt >= ct).astype(_BF16))


def _rwkv_inverse(eye, mats, T):
    inv = [eye + n for n in mats]
    p = list(mats)
    for _ in range(max((T - 1).bit_length() - 1, 0)):
        yield
        p = [_dot(x, x) for x in p]
        yield
        inv = [i + _dot(i, x) for i, x in zip(inv, p)]
    return inv


def _rwkv_core_per_head(j, masks, so_ref, T, at, rt, bt, kt, v4f, w_end):
    cast = lambda x: x.astype(_BF16)
    a4, r4, b4, k4 = (cast(_stack_heads(t)) for t in (at, rt, bt, kt))
    v4 = cast(v4f)
    bw4, kw4 = (cast(_stack_heads(t * w_end)) for t in (bt, kt))
    heads = range(RWKV_HEADS)
    blk = lambda x, h: x[h * T:(h + 1) * T]
    yield
    n_ab = [jnp.where(masks['strict_t'], _dot_nt(blk(a4, h), blk(b4, h)), 0.0) for h in heads]
    n_ak = [cast(jnp.where(masks['strict_t'], _dot_nt(blk(a4, h), blk(k4, h)), 0.0)) for h in heads]
    m_rb = [cast(jnp.where(masks['incl_t'], _dot_nt(blk(r4, h), blk(b4, h)), 0.0)) for h in heads]
    m_rk = [cast(jnp.where(masks['incl_t'], _dot_nt(blk(r4, h), blk(k4, h)), 0.0)) for h in heads]
    inv = yield from _rwkv_inverse(masks['eye_t'], n_ab, T)
    yield
    s0 = [so_ref[j, h] for h in heads]
    rhs = [_dot_nt(blk(a4, h), s0[h]) + _dot(n_ak[h], blk(v4, h)) for h in heads]
    o_own = [_dot_nt(blk(r4, h), s0[h]) + _dot(m_rk[h], blk(v4, h)) for h in heads]
    yield
    u = [cast(_dot(inv[h], rhs[h])) for h in heads]
    yield
    o4 = jnp.concatenate([o_own[h] + _dot(m_rb[h], u[h]) for h in heads], axis=0)
    for h in heads:
        so_ref[j, h] = (s0[h] * w_end[:, h * RWKV_HEAD_DIM:(h + 1) * RWKV_HEAD_DIM]
                        + _dot_tn(u[h], blk(bw4, h)) + _dot_tn(blk(v4, h), blk(kw4, h)))
    yield
    return o4


def _rwkv_core_stacked(j, masks, so_ref, T, at, rt, bt, kt, v4f, w_end):
    strict, incl, hmask = masks['strict'], masks['incl'], masks['hmask']
    a4, r4, b4, k4 = (_stack_heads(t) for t in (at, rt, bt, kt))
    yield
    n_ab = jnp.where(strict, _dot_nt(a4, b4), 0.0)
    n_ak = jnp.where(strict, _dot_nt(a4, k4), 0.0)
    m_rb = jnp.where(incl, _dot_nt(r4, b4), 0.0)
    m_rk = jnp.where(incl, _dot_nt(r4, k4), 0.0)
    (inv,) = yield from _rwkv_inverse(masks['eye'], [n_ab], T)

    def wide(x):
        return jnp.where(hmask, jnp.concatenate([x] * RWKV_HEADS, axis=0), 0.0)

    yield
    s0 = jnp.concatenate([so_ref[j, h] for h in range(RWKV_HEADS)], axis=1)
    rhs = _dot_nt(wide(at), s0) + _dot(n_ak, v4f)
    o4 = _dot_nt(wide(rt), s0) + _dot(m_rk, v4f)
    yield
    u4 = _dot(inv, rhs)
    yield
    o4 = o4 + _dot(m_rb, u4)
    s1 = (s0 * w_end + _dot_tn(u4, wide(bt * w_end)) + _dot_tn(v4f, wide(kt * w_end)))
    for h in range(RWKV_HEADS):
        so_ref[j, h] = s1[:, h * RWKV_HEAD_DIM:(h + 1) * RWKV_HEAD_DIM]
    yield
    return o4


def _rwkv_seq(j, masks, f_ref, mu_ref, w0_ref, w2_ref, a0_ref, a2_ref, g2_ref, kk_ref, ka_ref, rk_ref, lng_ref,
              lnb_ref, y_ref, so_ref, sho_ref, ext_scr, T, valid, row0=0):
    f = f_ref[j, row0:row0 + T, :]
    ext_scr[j, 0:SUBLANES, :] = sho_ref[j]
    ext_scr[j, SUBLANES:SUBLANES + T, :] = f
    prev = ext_scr[j, SUBLANES - 1:SUBLANES - 1 + T, :]
    sho_ref[j] = ext_scr[j, valid:valid + SUBLANES, :]
    fm = f + (prev - f) * mu_ref[...]
    r = fm[:, 0:256]
    k = fm[:, 256:512]
    v = fm[:, 512:768]
    lora = fm[:, 768:896]
    gl = fm[:, 896:1024]
    w_log = -_softplus(-(w0_ref[...] + _dot(jnp.tanh(lora), w2_ref[...]))) - 0.5
    logw = -jnp.exp(w_log)
    a = _sigmoid(a0_ref[...] + _dot(lora, a2_ref[...]))
    g = _dot(_sigmoid(gl), g2_ref[...])
    k2 = k * (1.0 + (a - 1.0) * ka_ref[...])
    kk4 = _stack_heads(k * kk_ref[...])
    kk4 = kk4 * lax.rsqrt(jnp.maximum(jnp.sum(kk4 * kk4, axis=-1, keepdims=True), 1e-24))
    kkn = _unstack_heads(kk4, T)
    if valid < T:
        live = _iota((T, 1), 0) < valid
        logw = jnp.where(live, logw, 0.0)
        kkn = jnp.where(live, kkn, 0.0)
        k2 = jnp.where(live, k2, 0.0)
    yield
    cw = _mm_sel(masks['tri'], logw, _NN)
    w_inc = jnp.exp(cw)
    w_exc = jnp.exp(cw - logw)
    w_inv = jnp.exp(-cw)
    w_end = w_inc[T - 1:T, :]
    at = -kkn * w_exc
    rt = r * w_inc
    bt = kkn * a * w_inv
    kt = k2 * w_inv
    v4f = _stack_heads(v)
    core = _rwkv_core_per_head if T % 16 == 0 else _rwkv_core_stacked
    o4 = yield from core(j, masks, so_ref, T, at, rt, bt, kt, v4f, w_end)

    mean = jnp.mean(o4, axis=-1, keepdims=True)
    var = jnp.mean(jnp.square(o4 - mean), axis=-1, keepdims=True)
    o4 = (o4 - mean) * lax.rsqrt(var + RWKV_LN_EPS)
    bonus4 = jnp.sum(_stack_heads(r * k2 * rk_ref[...]), axis=-1, keepdims=True) * v4f
    o = _unstack_heads(o4, T) * lng_ref[...] + lnb_ref[...] + _unstack_heads(bonus4, T)
    y_ref[j, row0:row0 + T, :] = o * g


def _rwkv_kernel(f_ref, s_ref, sh_ref, mu_ref, w0_ref, w2_ref, a0_ref, a2_ref, g2_ref, kk_ref, ka_ref, rk_ref,
                 lng_ref, lnb_ref, *rest, bb, T, valid, n_earlier):
    earlier = rest[:n_earlier]
    y_ref, so_all, sho_ref, ext_scr = rest[n_earlier:]
    so_ref = so_all.at[n_earlier] if n_earlier else so_all

    @pl.when(pl.program_id(1) == 0)
    def _():
        so_ref[...] = s_ref[...]
        sho_ref[...] = sh_ref[...]
        for k, e_ref in enumerate(earlier):
            so_all[k] = e_ref[...]

    seq = functools.partial(_rwkv_seq, f_ref=f_ref, mu_ref=mu_ref, w0_ref=w0_ref, w2_ref=w2_ref, a0_ref=a0_ref,
                            a2_ref=a2_ref, g2_ref=g2_ref, kk_ref=kk_ref, ka_ref=ka_ref, rk_ref=rk_ref,
                            lng_ref=lng_ref, lnb_ref=lnb_ref, y_ref=y_ref, so_ref=so_ref, sho_ref=sho_ref,
                            ext_scr=ext_scr, T=T, valid=valid)
    masks = _rwkv_masks(T)
    _interleave(seq(j, masks) for j in range(bb))


def _rwkv(f, st_rwkv, shift8, P, l, *, bb, T, valid, earlier=()):
    nb, L, _ = f.shape
    hd = RWKV_HEAD_DIM
    state_blk = (bb, RWKV_HEADS, hd, hd)
    smap = lambda i, c: (i, 0, 0, 0)
    n_e = len(earlier)
    if n_e:
        state_out = (pl.BlockSpec((n_e + 1,) + state_blk, lambda i, c: (0, i, 0, 0, 0)),
                     jax.ShapeDtypeStruct((n_e + 1,) + st_rwkv.shape[1:], _F32))
    else:
        state_out = (pl.BlockSpec(state_blk, smap), jax.ShapeDtypeStruct(st_rwkv.shape[1:], _F32))
    grid = (nb // bb, L // T)
    tmap = lambda i, c: (i, c, 0)
    bmap = lambda i, c: (i, 0, 0)
    layer = lambda r, n: pl.BlockSpec((None, r, n), lambda i, c: (l, 0, 0))
    row = lambda n: layer(1, n)
    lora_w = RWKV_DECAY_LORA + RWKV_A_LORA
    return pl.pallas_call(
        functools.partial(_rwkv_kernel, bb=bb, T=T, valid=valid, n_earlier=n_e),
        grid=grid,
        in_specs=[pl.BlockSpec((bb, T, RWKV_PROJ), tmap),
                  pl.BlockSpec((None, bb, RWKV_HEADS, hd, hd), lambda i, c: (l, i, 0, 0, 0)),
                  pl.BlockSpec((None, bb, SUBLANES, RWKV_PROJ), lambda i, c: (l, i, 0, 0)),
                  row(RWKV_PROJ), row(RWKV_WIDTH), layer(lora_w, RWKV_WIDTH),
                  row(RWKV_WIDTH), layer(lora_w, RWKV_WIDTH), layer(RWKV_GATE_LORA, RWKV_WIDTH),
                  row(RWKV_WIDTH), row(RWKV_WIDTH), row(RWKV_WIDTH), row(RWKV_WIDTH), row(RWKV_WIDTH)]
        + [pl.BlockSpec(state_blk, smap)] * n_e,
        out_specs=[pl.BlockSpec((bb, T, RWKV_WIDTH), tmap), state_out[0],
                   pl.BlockSpec((bb, SUBLANES, RWKV_PROJ), bmap)],
        out_shape=[jax.ShapeDtypeStruct((nb, L, RWKV_WIDTH), _F32), state_out[1],
                   jax.ShapeDtypeStruct(shift8.shape[1:], _F32)],
        scratch_shapes=[pltpu.VMEM((bb, T + SUBLANES, RWKV_PROJ), _F32)],
        compiler_params=_params("arbitrary", "arbitrary"),
    )(f, st_rwkv, shift8, P['rwkv_mu'], P['rwkv_w0'], P['rwkv_w2'], P['rwkv_a0'], P['rwkv_a2'], P['rwkv_g2'],
      P['rwkv_k_k'], P['rwkv_k_a'], P['rwkv_r_k'], P['rwkv_ln_g'], P['rwkv_ln_b'], *earlier)


def _mix_kernel(z_ref, xbc_ref, dt_ref, st_ref, cv_ref, cw_ref, cb_ref, dtb_ref, alog_ref, dexp_ref, ng_ref, eexp_ref,
                f_ref, s_ref, sh_ref, mu_ref, w0_ref, w2_ref, a0_ref, a2_ref, g2_ref, kk_ref, ka_ref, rk_ref,
                lng_ref, lnb_ref, *rest, bb, T, RT, n_earlier):
    e_ssd, e_rw = rest[:n_earlier], rest[n_earlier:2 * n_earlier]
    y_ref, sto_all, cvo_ref, yrw_ref, so_all, sho_ref, ext_scr, ybuf_scr, rext_scr = rest[2 * n_earlier:]
    sto_ref = sto_all.at[n_earlier] if n_earlier else sto_all
    so_ref = so_all.at[n_earlier] if n_earlier else so_all

    @pl.when(pl.program_id(1) == 0)
    def _():
        sto_ref[...] = st_ref[...]
        cvo_ref[...] = cv_ref[...]
        so_ref[...] = s_ref[...]
        sho_ref[...] = sh_ref[...]
        for k in range(n_earlier):
            sto_all[k] = e_ssd[k][...]
            so_all[k] = e_rw[k][...]

    ssd = functools.partial(_ssd_seq, z_ref=z_ref, xbc_ref=xbc_ref, dt_ref=dt_ref, cw_ref=cw_ref, cb_ref=cb_ref,
                            dtb_ref=dtb_ref, alog_ref=alog_ref, dexp_ref=dexp_ref, ng_ref=ng_ref, eexp_ref=eexp_ref,
                            y_ref=y_ref, sto_ref=sto_ref, cvo_ref=cvo_ref, ext_scr=ext_scr, ybuf_scr=ybuf_scr,
                            T=T, valid=T)
    rwkv = functools.partial(_rwkv_seq, f_ref=f_ref, mu_ref=mu_ref, w0_ref=w0_ref, w2_ref=w2_ref, a0_ref=a0_ref,
                             a2_ref=a2_ref, g2_ref=g2_ref, kk_ref=kk_ref, ka_ref=ka_ref, rk_ref=rk_ref,
                             lng_ref=lng_ref, lnb_ref=lnb_ref, y_ref=yrw_ref, so_ref=so_ref, sho_ref=sho_ref,
                             ext_scr=rext_scr, T=RT, valid=RT)
    masks = _rwkv_masks(RT)

    def rwkv_chunks(j):
        for c in range(T // RT):
            yield from rwkv(j, masks, row0=c * RT)

    _interleave([rwkv_chunks(j) for j in range(bb)] + [ssd(j) for j in range(bb)])


def _mix(z, xbc, dt, f, st_ssd, st_conv8, st_rwkv, shift8, P, l, *, bb, T, RT, earlier_ssd=(), earlier_rwkv=()):
    assert T % RT == 0
    nb, L, _ = z.shape
    hd = RWKV_HEAD_DIM
    grid = (nb // bb, L // T)
    tmap = lambda i, c: (i, c, 0)
    bmap = lambda i, c: (i, 0, 0)
    smap = lambda i, c: (i, 0, 0, 0)
    layer = lambda r, n: pl.BlockSpec((None, r, n), lambda i, c: (l, 0, 0))
    row = lambda n: layer(1, n)
    lora_w = RWKV_DECAY_LORA + RWKV_A_LORA
    ssd_blk = (bb, SSD_HEADS, SSD_HEAD_DIM, SSD_STATE)
    rw_blk = (bb, RWKV_HEADS, hd, hd)
    n_e = len(earlier_ssd)
    assert len(earlier_rwkv) == n_e

    def state_out(blk, full_shape):
        if n_e:
            return (pl.BlockSpec((n_e + 1,) + blk, lambda i, c: (0, i, 0, 0, 0)),
                    jax.ShapeDtypeStruct((n_e + 1,) + full_shape, _F32))
        return pl.BlockSpec(blk, smap), jax.ShapeDtypeStruct(full_shape, _F32)

    ssd_out, rw_out = state_out(ssd_blk, st_ssd.shape[1:]), state_out(rw_blk, st_rwkv.shape[1:])
    outs = pl.pallas_call(
        functools.partial(_mix_kernel, bb=bb, T=T, RT=RT, n_earlier=n_e),
        grid=grid,
        in_specs=[pl.BlockSpec((bb, T, SSD_WIDTH), tmap), pl.BlockSpec((bb, T, SSD_CONV_DIM), tmap),
                  pl.BlockSpec((bb, T, DT_PAD), tmap),
                  pl.BlockSpec((None,) + ssd_blk, lambda i, c: (l, i, 0, 0, 0)),
                  pl.BlockSpec((None, bb, SUBLANES, SSD_CONV_DIM), lambda i, c: (l, i, 0, 0)),
                  layer(SSD_CONV, SSD_CONV_DIM), layer(1, SSD_CONV_DIM), layer(1, DT_PAD), layer(1, DT_PAD),
                  layer(1, SSD_WIDTH), layer(1, SSD_WIDTH), pl.BlockSpec((DT_PAD, SSD_WIDTH), lambda i, c: (0, 0)),
                  pl.BlockSpec((bb, T, RWKV_PROJ), tmap),
                  pl.BlockSpec((None,) + rw_blk, lambda i, c: (l, i, 0, 0, 0)),
                  pl.BlockSpec((None, bb, SUBLANES, RWKV_PROJ), lambda i, c: (l, i, 0, 0)),
                  row(RWKV_PROJ), row(RWKV_WIDTH), layer(lora_w, RWKV_WIDTH),
                  row(RWKV_WIDTH), layer(lora_w, RWKV_WIDTH), layer(RWKV_GATE_LORA, RWKV_WIDTH),
                  row(RWKV_WIDTH), row(RWKV_WIDTH), row(RWKV_WIDTH), row(RWKV_WIDTH), row(RWKV_WIDTH)]
        + [pl.BlockSpec(ssd_blk, smap)] * n_e + [pl.BlockSpec(rw_blk, smap)] * n_e,
        out_specs=[pl.BlockSpec((bb, T, SSD_WIDTH), tmap), ssd_out[0],
                   pl.BlockSpec((bb, SUBLANES, SSD_CONV_DIM), bmap),
                   pl.BlockSpec((bb, T, RWKV_WIDTH), tmap), rw_out[0],
                   pl.BlockSpec((bb, SUBLANES, RWKV_PROJ), bmap)],
        out_shape=[jax.ShapeDtypeStruct((nb, L, SSD_WIDTH), _F32), ssd_out[1],
                   jax.ShapeDtypeStruct(st_conv8.shape[1:], _F32),
                   jax.ShapeDtypeStruct((nb, L, RWKV_WIDTH), _F32), rw_out[1],
                   jax.ShapeDtypeStruct(shift8.shape[1:], _F32)],
        scratch_shapes=[pltpu.VMEM((bb, T + SUBLANES, SSD_CONV_DIM), _F32), pltpu.VMEM((bb, T, SSD_WIDTH), _F32),
                        pltpu.VMEM((bb, RT + SUBLANES, RWKV_PROJ), _F32)],
        compiler_params=_params("arbitrary", "arbitrary"),
    )(z, xbc, dt, st_ssd, st_conv8, P['ssd_conv_w'], P['ssd_conv_b'], P['ssd_dt_bias'], P['ssd_a_log'],
      P['ssd_d'], P['ssd_norm_g'], P['ssd_head_expand'],
      f, st_rwkv, shift8, P['rwkv_mu'], P['rwkv_w0'], P['rwkv_w2'], P['rwkv_a0'], P['rwkv_a2'], P['rwkv_g2'],
      P['rwkv_k_k'], P['rwkv_k_a'], P['rwkv_r_k'], P['rwkv_ln_g'], P['rwkv_ln_b'], *earlier_ssd, *earlier_rwkv)
    return outs


def _s5_kernel(u_ref, sre_ref, sim_ref, lre_ref, lim_ref, dt_ref, bre_ref, bim_ref, cre_ref, cim_ref, d_ref,
               gw_ref, gb_ref, y_ref, hre_ref, him_ref, hre_scr, him_scr, *, nb, tl, final_step):
    @pl.when(pl.program_id(0) == 0)
    def _():
        hre_ref[...] = sre_ref[...]
        him_ref[...] = sim_ref[...]

    lre, lim, dt = lre_ref[...], lim_ref[...], jnp.exp(dt_ref[...])
    mag = jnp.exp(lre * dt)
    ang = lim * dt
    ab_re, ab_im = mag * jnp.cos(ang), mag * jnp.sin(ang)
    den = jnp.square(lre) + jnp.square(lim)
    q_re = ((ab_re - 1.0) * lre + ab_im * lim) / den
    q_im = (ab_im * lre - (ab_re - 1.0) * lim) / den
    b_re, b_im = bre_ref[...], bim_ref[...]
    bb_re = q_re * b_re - q_im * b_im
    bb_im = q_re * b_im + q_im * b_re
    u = u_ref[...]
    hre_scr[0:nb, :] = hre_ref[...]
    him_scr[0:nb, :] = him_ref[...]
    hre_scr[nb:, :] = _dot(u, bb_re)
    him_scr[nb:, :] = _dot(u, bb_im)
    ar = jnp.broadcast_to(ab_re, (nb, S5_FLAT))
    ai = jnp.broadcast_to(ab_im, (nb, S5_FLAT))

    if nb <= SUBLANES:
        def step(s, carry):
            pr, pi = carry
            cur = pl.ds(pl.multiple_of((s + 1) * nb, nb), nb)
            nr = ar * pr - ai * pi + hre_scr[cur, :]
            ni = ar * pi + ai * pr + him_scr[cur, :]
            hre_scr[cur, :] = nr
            him_scr[cur, :] = ni
            return nr, ni

        lax.fori_loop(0, tl, step, (hre_ref[...], him_ref[...]), unroll=2)
    else:
        def step(s, carry):
            prev = pl.ds(pl.multiple_of(s * nb, nb), nb)
            cur = pl.ds(pl.multiple_of((s + 1) * nb, nb), nb)
            pr, pi = hre_scr[prev, :], him_scr[prev, :]
            hre_scr[cur, :] = ar * pr - ai * pi + hre_scr[cur, :]
            him_scr[cur, :] = ar * pi + ai * pr + him_scr[cur, :]
            return carry

        lax.fori_loop(0, tl, step, 0)
    hre_ref[...] = hre_scr[(final_step + 1) * nb:(final_step + 2) * nb, :]
    him_ref[...] = him_scr[(final_step + 1) * nb:(final_step + 2) * nb, :]
    y = _dot(hre_scr[nb:, :], cre_ref[...]) - _dot(him_scr[nb:, :], cim_ref[...]) + d_ref[...] * u
    y = 0.5 * y * (1.0 + jnp.tanh(0.7978845608028654 * (y + 0.044715 * (y * y * y))))
    y_ref[...] = y * _sigmoid(_dot(y, gw_ref[...]) + gb_ref[...])


def _s5(u_tm, st_re, st_im, P, l, *, nb, tl, final_step):
    rows = u_tm.shape[0]
    L = rows // nb
    layer = lambda r, n: pl.BlockSpec((None, r, n), lambda t: (l, 0, 0))
    row = lambda n: layer(1, n)
    full = lambda a, b: pl.BlockSpec((a, b), lambda t: (0, 0))
    state = pl.BlockSpec((None, nb, S5_FLAT), lambda t: (l, 0, 0))
    return pl.pallas_call(
        functools.partial(_s5_kernel, nb=nb, tl=tl, final_step=final_step),
        grid=(L // tl,),
        in_specs=[pl.BlockSpec((tl * nb, S5_WIDTH), lambda t: (t, 0)), state, state,
                  row(S5_FLAT), row(S5_FLAT), row(S5_FLAT),
                  layer(S5_WIDTH, S5_FLAT), layer(S5_WIDTH, S5_FLAT), layer(S5_FLAT, S5_WIDTH),
                  layer(S5_FLAT, S5_WIDTH), row(S5_WIDTH), layer(S5_WIDTH, S5_WIDTH), row(S5_WIDTH)],
        out_specs=[pl.BlockSpec((tl * nb, S5_WIDTH), lambda t: (t, 0)), full(nb, S5_FLAT), full(nb, S5_FLAT)],
        out_shape=[jax.ShapeDtypeStruct((rows, S5_WIDTH), _F32),
                   jax.ShapeDtypeStruct((nb, S5_FLAT), _F32), jax.ShapeDtypeStruct((nb, S5_FLAT), _F32)],
        scratch_shapes=[pltpu.VMEM(((tl + 1) * nb, S5_FLAT), _F32), pltpu.VMEM(((tl + 1) * nb, S5_FLAT), _F32)],
        compiler_params=_params("arbitrary"),
    )(u_tm, st_re, st_im, P['s5_a_re'], P['s5_a_im'], P['s5_log_dt'], P['s5_b_re'], P['s5_b_im'], P['s5_c_re'],
      P['s5_c_im'], P['s5_d'], P['s5_glu_w'], P['s5_glu_b'])


def _out_mlp_kernel(x_ref, yssd_ref, yrw_ref, ys5_ref, g1_ref, sh2_ref, sc2_ref, g2_ref, n2g_ref, fg_ref,
                    wo_ref, w1_ref, w2_ref, o_ref, hid_scr, *, final_norm):
    bb, tl, _ = x_ref.shape
    rows = bb * tl
    flat = lambda ref: ref[...].reshape(rows, ref.shape[-1]).astype(_BF16)
    mix = (_dot(flat(yssd_ref), wo_ref[0:512, :]) + _dot(flat(yrw_ref), wo_ref[512:768, :])
           + _dot(flat(ys5_ref), wo_ref[768:1024, :]))
    x1 = x_ref[...] + g1_ref[...] * mix.reshape(bb, tl, D_MODEL)
    xn = x1 * lax.rsqrt(jnp.mean(x1 * x1, axis=-1, keepdims=True) + NORM_EPS)
    h2 = (xn * n2g_ref[...] * (1.0 + sc2_ref[...]) + sh2_ref[...]).reshape(rows, D_MODEL).astype(_BF16)
    for c in range(D_FF // FF_TILE):
        cols = slice(c * FF_TILE, (c + 1) * FF_TILE)
        hid_scr[:, cols] = jnp.square(jnp.maximum(_dot(h2, w1_ref[:, cols]), 0.0)).astype(_BF16)
    ff = _dot(hid_scr[...], w2_ref[...])
    out = x1 + g2_ref[...] * ff.reshape(bb, tl, D_MODEL)
    if final_norm:
        out = out * lax.rsqrt(jnp.mean(out * out, axis=-1, keepdims=True) + NORM_EPS) * fg_ref[...]
    o_ref[...] = out


def _out_mlp(x, y_ssd, y_rw, y_s5, mod, P, l, *, bb, tl, time_major_s5, final_norm):
    nb, L, _ = x.shape
    grid = (nb // bb, L // tl)
    xmap = lambda i, t: (i, t, 0)
    cmap = lambda i, t: (0, 0)
    wmap = lambda i, t: (l, 0, 0)
    once = pl.Buffered(1)

    def modspec(j):
        return pl.BlockSpec((bb, 1, D_MODEL), lambda i, t: (i, 0, j))

    s5_spec = (pl.BlockSpec((tl, S5_WIDTH), lambda i, t: (t, i)) if time_major_s5
               else pl.BlockSpec((bb, tl, S5_WIDTH), xmap))
    return pl.pallas_call(
        functools.partial(_out_mlp_kernel, final_norm=final_norm),
        grid=grid,
        in_specs=[pl.BlockSpec((bb, tl, D_MODEL), xmap), pl.BlockSpec((bb, tl, SSD_WIDTH), xmap),
                  pl.BlockSpec((bb, tl, RWKV_WIDTH), xmap), s5_spec,
                  modspec(2), modspec(3), modspec(4), modspec(5),
                  pl.BlockSpec((None, 1, D_MODEL), wmap), pl.BlockSpec((1, D_MODEL), cmap),
                  pl.BlockSpec((None, D_MODEL, D_MODEL), wmap, pipeline_mode=once),
                  pl.BlockSpec((None, D_MODEL, D_FF), wmap, pipeline_mode=once),
                  pl.BlockSpec((None, D_FF, D_MODEL), wmap, pipeline_mode=once)],
        out_specs=pl.BlockSpec((bb, tl, D_MODEL), xmap),
        out_shape=jax.ShapeDtypeStruct(x.shape, _F32),
        scratch_shapes=[pltpu.VMEM((bb * tl, D_FF), _BF16)],
        compiler_params=_params("arbitrary", "arbitrary"),
    )(x, y_ssd, y_rw, y_s5, mod, mod, mod, mod, P['norm2_g'], P['final_g'], P['w_out'], P['mlp_w1'], P['mlp_w2'])


def _trunk(x, mod, states, P, valid, *, prompt):
    st_ssd, st_conv, st_rwkv, st_shift, st_re, st_im = states
    nb, L, _ = x.shape
    if prompt:
        bb, tl, mlp_tl = 1, min(IN_ROW_TILE, L), min(MLP_ROW_TILE, L)
        ssd_t, rwkv_t, s5_tl = min(SSD_CHUNK, L), min(RWKV_CHUNK, L), min(S5_TILE, L)
    else:
        bb, tl, mlp_tl = min(nb, ROW_TILE // L), L, L
        ssd_t = rwkv_t = s5_tl = L
    new = ([], [], [], [], [], [])
    for l in range(DEPTH):
        z, xbc, frw, us5, dt = _in_proj(x, mod[l], P, l, bb=bb, tl=tl, time_major_s5=prompt)
        v = valid
        last = l == DEPTH - 1
        e_ssd, e_rwkv = (tuple(new[0]), tuple(new[2])) if last else ((), ())
        if prompt:
            y_ssd, n_ssd, n_conv, y_rw, n_rwkv, n_shift = _mix(
                z, xbc, dt, frw, st_ssd, st_conv, st_rwkv, st_shift, P, l, bb=min(nb, MIX_PROMPT_SEQS), T=ssd_t,
                RT=rwkv_t, earlier_ssd=e_ssd, earlier_rwkv=e_rwkv)
        else:
            sample_bb = min(nb, SAMPLE_SEQS)
            y_ssd, n_ssd, n_conv = _ssd(z, xbc, dt, st_ssd, st_conv, P, l, bb=sample_bb, T=ssd_t, valid=v,
                                        earlier=e_ssd)
            y_rw, n_rwkv, n_shift = _rwkv(frw, st_rwkv, st_shift, P, l, bb=sample_bb, T=rwkv_t, valid=v,
                                          earlier=e_rwkv)
        if prompt:
            u_tm = us5.reshape(L * nb, S5_WIDTH)
        else:
            u_tm = jnp.transpose(us5, (1, 0, 2)).reshape(L * nb, S5_WIDTH)
        y_s5, n_re, n_im = _s5(u_tm, st_re, st_im, P, l, nb=nb, tl=s5_tl,
                               final_step=(s5_tl - 1) if prompt else (v - 1))
        if prompt:
            y_s5 = y_s5.reshape(L, nb * S5_WIDTH)
        else:
            y_s5 = jnp.transpose(y_s5.reshape(L, nb, S5_WIDTH), (1, 0, 2))
        x = _out_mlp(x, y_ssd, y_rw, y_s5, mod[l], P, l, bb=bb, tl=mlp_tl, time_major_s5=prompt,
                     final_norm=(l == DEPTH - 1))
        for lst, s in zip(new, (n_ssd, n_conv, n_rwkv, n_shift, n_re, n_im)):
            lst.append(s)
    by_kernel = (0, 2) if DEPTH > 1 else ()
    return x, [lst[-1] if k in by_kernel else jnp.stack(lst) for k, lst in enumerate(new)]


def _pack_states(st_ssd, st_conv, st_rwkv, st_shift, st_re, st_im):
    d, nb = st_ssd.shape[:2]
    conv8 = jnp.pad(st_conv, ((0, 0), (0, 0), (SUBLANES - (SSD_CONV - 1), 0), (0, 0)))
    shift8 = jnp.pad(st_shift[:, :, None, :], ((0, 0), (0, 0), (SUBLANES - 1, 0), (0, 0)))
    return (st_ssd, conv8, st_rwkv, shift8, st_re.reshape(d, nb, S5_FLAT), st_im.reshape(d, nb, S5_FLAT))


def _unpack_states(n_ssd, n_conv8, n_rwkv, n_shift8, n_re, n_im):
    d, nb = n_ssd.shape[:2]
    return (n_ssd, n_conv8[:, :, SUBLANES - (SSD_CONV - 1):, :], n_rwkv, n_shift8[:, :, SUBLANES - 1, :],
            n_re.reshape(d, nb, S5_GROUPS, S5_STATE), n_im.reshape(d, nb, S5_GROUPS, S5_STATE))


def _prepare_params(norm1_g, norm2_g, w_in, ssd_conv_w, ssd_conv_b, ssd_dt_bias, ssd_a_log, ssd_d, ssd_norm_g, rwkv_mu,
                    rwkv_w0, rwkv_w2, rwkv_a0, rwkv_a2, rwkv_g2, rwkv_k_k, rwkv_k_a, rwkv_r_k, rwkv_ln_g, rwkv_ln_b,
                    s5_a_re, s5_a_im, s5_log_dt, s5_b_re, s5_b_im, s5_c_re, s5_c_im, s5_d, s5_glu_w, s5_glu_b, w_out,
                    mlp_w1, mlp_w2, final_g):
    rows = lambda a: a.reshape(DEPTH, 1, -1)
    head_pad = ((0, 0), (0, DT_PAD - SSD_HEADS))
    eye = jnp.eye(S5_GROUPS, dtype=_F32)
    zeros_lora = jnp.zeros((DEPTH, RWKV_A_LORA, RWKV_WIDTH), _F32)
    w_in_p = jnp.concatenate(
        [w_in[:, :, _Z0:_DT0], w_in[:, :, _FRW0:_IN_COLS], w_in[:, :, _DT0:_FRW0],
         jnp.zeros((DEPTH, D_MODEL, DT_PAD - SSD_HEADS), w_in.dtype)], axis=-1).astype(_BF16)
    return dict(
        norm1_g=rows(norm1_g), norm2_g=rows(norm2_g), final_g=final_g.reshape(1, D_MODEL), w_in_p=w_in_p,
        w_out=w_out.astype(_BF16), mlp_w1=mlp_w1.astype(_BF16), mlp_w2=mlp_w2.astype(_BF16),
        ssd_conv_w=ssd_conv_w, ssd_conv_b=rows(ssd_conv_b), ssd_dt_bias=rows(jnp.pad(ssd_dt_bias, head_pad)),
        ssd_a_log=rows(jnp.pad(ssd_a_log, head_pad)), ssd_d=rows(jnp.repeat(ssd_d, SSD_HEAD_DIM, axis=-1)),
        ssd_norm_g=rows(ssd_norm_g),
        ssd_head_expand=jnp.repeat(jnp.eye(DT_PAD, SSD_HEADS, dtype=_F32), SSD_HEAD_DIM, axis=1),
        rwkv_mu=rows(rwkv_mu), rwkv_w0=rows(rwkv_w0), rwkv_a0=rows(rwkv_a0), rwkv_k_k=rows(rwkv_k_k),
        rwkv_k_a=rows(rwkv_k_a), rwkv_r_k=rows(rwkv_r_k), rwkv_ln_g=rows(rwkv_ln_g), rwkv_ln_b=rows(rwkv_ln_b),
        rwkv_w2=jnp.concatenate([rwkv_w2, zeros_lora], axis=1),
        rwkv_a2=jnp.concatenate([zeros_lora, rwkv_a2], axis=1),
        rwkv_g2=rwkv_g2,
        s5_a_re=rows(s5_a_re), s5_a_im=rows(s5_a_im), s5_log_dt=rows(jnp.repeat(s5_log_dt, S5_STATE, axis=-1)),
        s5_b_re=jnp.einsum('lgpc,gh->lgchp', s5_b_re, eye).reshape(DEPTH, S5_WIDTH, S5_FLAT),
        s5_b_im=jnp.einsum('lgpc,gh->lgchp', s5_b_im, eye).reshape(DEPTH, S5_WIDTH, S5_FLAT),
        s5_c_re=jnp.einsum('lgcp,gh->lgphc', s5_c_re, eye).reshape(DEPTH, S5_FLAT, S5_WIDTH),
        s5_c_im=jnp.einsum('lgcp,gh->lgphc', s5_c_im, eye).reshape(DEPTH, S5_FLAT, S5_WIDTH),
        s5_d=rows(s5_d), s5_glu_w=s5_glu_w, s5_glu_b=rows(s5_glu_b))


def kernel(x_prompt, x_sample, c_prompt, c_sample, state_ssd, state_ssd_conv, state_rwkv, state_rwkv_shift,
           state_s5_re, state_s5_im, ada_w, ada_b, norm1_g, norm2_g, w_in, ssd_conv_w, ssd_conv_b, ssd_dt_bias,
           ssd_a_log, ssd_d, ssd_norm_g, rwkv_mu, rwkv_w0, rwkv_w2, rwkv_a0, rwkv_a2, rwkv_g2, rwkv_k_k, rwkv_k_a,
           rwkv_r_k, rwkv_ln_g, rwkv_ln_b, s5_a_re, s5_a_im, s5_log_dt, s5_b_re, s5_b_im, s5_c_re, s5_c_im, s5_d,
           s5_glu_w, s5_glu_b, w_out, mlp_w1, mlp_w2, final_g):
    bp, lp, _ = x_prompt.shape
    bs, ls, _ = x_sample.shape
    ls_pad = -(-ls // SUBLANES) * SUBLANES
    P = _prepare_params(norm1_g, norm2_g, w_in, ssd_conv_w, ssd_conv_b, ssd_dt_bias, ssd_a_log, ssd_d, ssd_norm_g,
                        rwkv_mu, rwkv_w0, rwkv_w2, rwkv_a0, rwkv_a2, rwkv_g2, rwkv_k_k, rwkv_k_a, rwkv_r_k, rwkv_ln_g,
                        rwkv_ln_b, s5_a_re, s5_a_im, s5_log_dt, s5_b_re, s5_b_im, s5_c_re, s5_c_im, s5_d, s5_glu_w,
                        s5_glu_b, w_out, mlp_w1, mlp_w2, final_g)
    mod = _ada(jnp.concatenate([c_prompt, c_sample], axis=0), ada_w, ada_b)
    mod_p = mod[:, :bp].reshape(DEPTH, bp, 1, 6 * D_MODEL)
    mod_s = mod[:, bp:].reshape(DEPTH, bs, 1, 6 * D_MODEL)

    zeros = lambda shape: jnp.zeros((DEPTH, bp) + shape, _F32)
    p_states = _pack_states(zeros((SSD_HEADS, SSD_HEAD_DIM, SSD_STATE)), zeros((SSD_CONV - 1, SSD_CONV_DIM)),
                            zeros((RWKV_HEADS, RWKV_HEAD_DIM, RWKV_HEAD_DIM)), zeros((RWKV_PROJ,)),
                            zeros((S5_GROUPS, S5_STATE)), zeros((S5_GROUPS, S5_STATE)))
    y_prompt, sp = _trunk(x_prompt, mod_p, p_states, P, ls, prompt=True)

    s_states = _pack_states(state_ssd, state_ssd_conv, state_rwkv, state_rwkv_shift, state_s5_re, state_s5_im)
    x_s = jnp.pad(x_sample, ((0, 0), (0, ls_pad - ls), (0, 0)))
    y_sample, ss = _trunk(x_s, mod_s, s_states, P, ls, prompt=False)
    return (y_prompt, y_sample[:, :ls]) + _unpack_states(*sp) + _unpack_states(*ss)
```

```python
import functools

import jax
import jax.numpy as jnp
from jax import lax
from jax.experimental import pallas as pl
from jax.experimental.pallas import tpu as pltpu

D_MODEL = 1024
DEPTH = 2
SSD_WIDTH = 512
SSD_HEAD_DIM = 64
SSD_HEADS = 8
SSD_GROUPS = 2
SSD_HEADS_PER_GROUP = SSD_HEADS // SSD_GROUPS
SSD_STATE = 128
SSD_CONV = 4
SSD_CONV_DIM = SSD_WIDTH + 2 * SSD_GROUPS * SSD_STATE
RWKV_WIDTH = 256
RWKV_HEAD_DIM = 64
RWKV_HEADS = 4
RWKV_DECAY_LORA = 64
RWKV_A_LORA = 64
RWKV_GATE_LORA = 128
RWKV_PROJ = 3 * RWKV_WIDTH + RWKV_DECAY_LORA + RWKV_A_LORA + RWKV_GATE_LORA
RWKV_LN_EPS = 64e-5
S5_WIDTH = 256
S5_GROUP_CH = 16
S5_GROUPS = 16
S5_STATE = 64
S5_FLAT = S5_GROUPS * S5_STATE
D_FF = 4 * D_MODEL
NORM_EPS = 1e-6
DT_PAD = 128
SUBLANES = 8

_Z0, _XBC0, _DT0, _FRW0, _US50 = 0, SSD_WIDTH, SSD_WIDTH + SSD_CONV_DIM, SSD_WIDTH + SSD_CONV_DIM + SSD_HEADS, \
    SSD_WIDTH + SSD_CONV_DIM + SSD_HEADS + RWKV_PROJ
_IN_COLS = _US50 + S5_WIDTH
_PZ, _PXBC, _PFRW, _PUS5, _PDT = 0, 512, 1536, 2560, 2816
_P_COLS = _PDT + DT_PAD

SSD_CHUNK = 128
RWKV_CHUNK = 64
MIX_PROMPT_SEQS = 4
SAMPLE_SEQS = 8
S5_TILE = 256
S5_BLOCK_STEPS = 32
ROW_TILE = 512
IN_ROW_TILE = 1024
MLP_ROW_TILE = 512
FF_TILE = 1024
VMEM_LIMIT = 56 * 1024 * 1024

_F32 = jnp.float32
_BF16 = jnp.bfloat16


_NN = (((1,), (0,)), ((), ()))
_NT = (((1,), (1,)), ((), ()))
_TN = (((0,), (0,)), ((), ()))


def _mm(a, b, dims):
    return lax.dot_general(a.astype(_BF16), b.astype(_BF16), dims, preferred_element_type=_F32)


def _dot(a, b):
    return _mm(a, b, _NN)


def _dot_nt(a, b):
    return _mm(a, b, _NT)


def _dot_tn(a, b):
    return _mm(a, b, _TN)


def _mm_sel(sel, x, dims, x_is_rhs=True):
    hi = x.astype(_BF16)
    rest = x - hi.astype(_F32)
    mid = rest.astype(_BF16)
    lo = (rest - mid.astype(_F32)).astype(_BF16)
    sel = sel.astype(_BF16)
    if x_is_rhs:
        dg = lambda y: lax.dot_general(sel, y, dims, preferred_element_type=_F32)
    else:
        dg = lambda y: lax.dot_general(y, sel, dims, preferred_element_type=_F32)
    return dg(hi) + (dg(mid) + dg(lo))


def _iota(shape, dim):
    return lax.broadcasted_iota(jnp.int32, shape, dim)


def _sigmoid(x):
    return 1.0 / (1.0 + jnp.exp(-x))


def _softplus(x):
    return jnp.maximum(x, 0.0) + jnp.log1p(jnp.exp(-jnp.abs(x)))


def _silu(x):
    return x * _sigmoid(x)


def _params(*sems):
    return pltpu.CompilerParams(dimension_semantics=sems, vmem_limit_bytes=VMEM_LIMIT)


def _ada_kernel(c_ref, w_ref, b_ref, o_ref):
    c = c_ref[...]
    s = _silu(c).astype(_BF16)
    o_ref[0] = _dot(s, w_ref[0].astype(_BF16)) + b_ref[0]


def _ada(c_all, ada_w, ada_b):
    rows = c_all.shape[0]
    tn = 1536
    return pl.pallas_call(
        _ada_kernel,
        grid=(DEPTH, 6 * D_MODEL // tn),
        in_specs=[pl.BlockSpec((rows, D_MODEL), lambda l, j: (0, 0)),
                  pl.BlockSpec((1, D_MODEL, tn), lambda l, j: (l, 0, j)),
                  pl.BlockSpec((1, 1, tn), lambda l, j: (l, 0, j))],
        out_specs=pl.BlockSpec((1, rows, tn), lambda l, j: (l, 0, j)),
        out_shape=jax.ShapeDtypeStruct((DEPTH, rows, 6 * D_MODEL), _F32),
        compiler_params=_params("arbitrary", "arbitrary"),
    )(c_all, ada_w, ada_b.reshape(DEPTH, 1, 6 * D_MODEL))


def _in_proj_kernel(x_ref, sh_ref, sc_ref, g_ref, w_ref, z_ref, xbc_ref, frw_ref, us5_ref, dt_ref):
    x = x_ref[...]
    bb, tl, _ = x.shape
    xn = x * lax.rsqrt(jnp.mean(x * x, axis=-1, keepdims=True) + NORM_EPS)
    h = xn * g_ref[...] * (1.0 + sc_ref[...]) + sh_ref[...]
    h = h.reshape(bb * tl, D_MODEL).astype(_BF16)
    for o_ref, lo, hi in ((z_ref, _PZ, _PXBC), (xbc_ref, _PXBC, _PFRW), (frw_ref, _PFRW, _PUS5),
                          (us5_ref, _PUS5, _PDT), (dt_ref, _PDT, _P_COLS)):
        o_ref[...] = _dot(h, w_ref[:, lo:hi]).reshape(o_ref.shape)


def _in_proj(x, mod, P, l, *, bb, tl, time_major_s5):
    nb, L, _ = x.shape
    grid = (nb // bb, L // tl)
    xmap = lambda i, t: (i, t, 0)

    def modspec(j):
        return pl.BlockSpec((bb, 1, D_MODEL), lambda i, t: (i, 0, j))

    def out(n):
        return jax.ShapeDtypeStruct((nb, L, n), _F32), pl.BlockSpec((bb, tl, n), xmap)

    outs = [out(SSD_WIDTH), out(SSD_CONV_DIM), out(RWKV_PROJ), out(S5_WIDTH), out(DT_PAD)]
    if time_major_s5:
        assert bb == 1
        outs[3] = (jax.ShapeDtypeStruct((L, nb * S5_WIDTH), _F32), pl.BlockSpec((tl, S5_WIDTH), lambda i, t: (t, i)))
    return pl.pallas_call(
        _in_proj_kernel,
        grid=grid,
        in_specs=[pl.BlockSpec((bb, tl, D_MODEL), xmap), modspec(0), modspec(1),
                  pl.BlockSpec((None, 1, D_MODEL), lambda i, t: (l, 0, 0)),
                  pl.BlockSpec((None, D_MODEL, _P_COLS), lambda i, t: (l, 0, 0))],
        out_specs=[o[1] for o in outs],
        out_shape=[o[0] for o in outs],
        compiler_params=_params("arbitrary", "arbitrary"),
    )(x, mod, mod, P['norm1_g'], P['w_in_p'])


def _ssd_seq(j, z_ref, xbc_ref, dt_ref, cw_ref, cb_ref, dtb_ref, alog_ref, dexp_ref, ng_ref, eexp_ref,
             y_ref, sto_ref, cvo_ref, ext_scr, ybuf_scr, T, valid):
    u = xbc_ref[j]
    ext_scr[j, 0:SUBLANES, :] = cvo_ref[j]
    ext_scr[j, SUBLANES:SUBLANES + T, :] = u
    conv = cb_ref[...] + cw_ref[3:4, :] * u
    for back in (1, 2, 3):
        conv = conv + cw_ref[3 - back:4 - back, :] * ext_scr[j, SUBLANES - back:SUBLANES - back + T, :]
    cvo_ref[j] = ext_scr[j, valid:valid + SUBLANES, :]
    xbc = _silu(conv)
    xs = xbc[:, :SSD_WIDTH]
    row = _iota((T, 1), 0)
    dtv = _softplus(dt_ref[j] + dtb_ref[...])
    if valid < T:
        dtv = jnp.where(row < valid, dtv, 0.0)
    a_row = jnp.where(_iota((1, DT_PAD), 1) < SSD_HEADS, -jnp.exp(alog_ref[...]), 0.0)
    a = dtv * a_row
    yield
    tri_b = _iota((T, T), 0) >= _iota((T, T), 1)
    cs = _mm_sel(tri_b, a, _NN)
    eye = _iota((DT_PAD, DT_PAD), 0) == _iota((DT_PAD, DT_PAD), 1)
    xdt = xs * _mm_sel(eexp_ref[...], dtv, _NN, x_is_rhs=False)
    yield
    cs_t = _mm_sel(eye, cs, _NT)
    ecs = jnp.exp(cs)
    cs_last = cs[T - 1:T, :]
    dte = jnp.exp(cs_last - cs)
    elast = jnp.exp(cs_last)
    for g in range(SSD_GROUPS):
        bm = xbc[:, SSD_WIDTH + g * SSD_STATE:SSD_WIDTH + (g + 1) * SSD_STATE]
        cm = xbc[:, SSD_WIDTH + (SSD_GROUPS + g) * SSD_STATE:SSD_WIDTH + (SSD_GROUPS + g + 1) * SSD_STATE]
        cb = _dot_nt(cm, bm)
        for r in range(SSD_HEADS_PER_GROUP):
            if r == 0 or T >= 64:
                yield
            h = g * SSD_HEADS_PER_GROUP + r
            lo, hi = h * SSD_HEAD_DIM, (h + 1) * SSD_HEAD_DIM
            decay = jnp.exp(jnp.where(tri_b, cs[:, h:h + 1] - cs_t[h:h + 1, :], -jnp.inf))
            xdt_h = xdt[:, lo:hi]
            st = sto_ref[j, h]
            y_h = _dot(cb * decay, xdt_h) + ecs[:, h:h + 1] * _dot_nt(cm, st)
            ybuf_scr[j, :, lo:hi] = y_h
            sto_ref[j, h] = elast[:, h:h + 1] * st + _dot_tn(xdt_h * dte[:, h:h + 1], bm)
    yield
    y = ybuf_scr[j] + xs * dexp_ref[...]
    y = y * _silu(z_ref[j])
    y = y * lax.rsqrt(jnp.mean(y * y, axis=-1, keepdims=True) + NORM_EPS) * ng_ref[...]
    y_ref[j] = y


def _ssd_kernel(z_ref, xbc_ref, dt_ref, st_ref, cv_ref, cw_ref, cb_ref, dtb_ref, alog_ref, dexp_ref, ng_ref,
                eexp_ref, *rest, bb, T, valid, n_earlier):
    earlier = rest[:n_earlier]
    y_ref, sto_all, cvo_ref, ext_scr, ybuf_scr = rest[n_earlier:]
    sto_ref = sto_all.at[n_earlier] if n_earlier else sto_all

    @pl.when(pl.program_id(1) == 0)
    def _():
        sto_ref[...] = st_ref[...]
        cvo_ref[...] = cv_ref[...]
        for k, e_ref in enumerate(earlier):
            sto_all[k] = e_ref[...]

    seq = functools.partial(_ssd_seq, z_ref=z_ref, xbc_ref=xbc_ref, dt_ref=dt_ref, cw_ref=cw_ref, cb_ref=cb_ref,
                            dtb_ref=dtb_ref, alog_ref=alog_ref, dexp_ref=dexp_ref, ng_ref=ng_ref, eexp_ref=eexp_ref,
                            y_ref=y_ref, sto_ref=sto_ref, cvo_ref=cvo_ref, ext_scr=ext_scr, ybuf_scr=ybuf_scr,
                            T=T, valid=valid)
    _interleave(seq(j) for j in range(bb))


def _ssd(z, xbc, dt, st_ssd, st_conv8, P, l, *, bb, T, valid, earlier=()):
    nb, L, _ = z.shape
    grid = (nb // bb, L // T)
    tmap = lambda i, c: (i, c, 0)
    smap = lambda i, c: (i, 0, 0, 0)
    cvmap = lambda i, c: (i, 0, 0)
    layer = lambda r, n: pl.BlockSpec((None, r, n), lambda i, c: (l, 0, 0))
    state_blk = (bb, SSD_HEADS, SSD_HEAD_DIM, SSD_STATE)
    n_e = len(earlier)
    if n_e:
        state_out = (pl.BlockSpec((n_e + 1,) + state_blk, lambda i, c: (0, i, 0, 0, 0)),
                     jax.ShapeDtypeStruct((n_e + 1,) + st_ssd.shape[1:], _F32))
    else:
        state_out = (pl.BlockSpec(state_blk, smap), jax.ShapeDtypeStruct(st_ssd.shape[1:], _F32))
    return pl.pallas_call(
        functools.partial(_ssd_kernel, bb=bb, T=T, valid=valid, n_earlier=n_e),
        grid=grid,
        in_specs=[pl.BlockSpec((bb, T, SSD_WIDTH), tmap), pl.BlockSpec((bb, T, SSD_CONV_DIM), tmap),
                  pl.BlockSpec((bb, T, DT_PAD), tmap),
                  pl.BlockSpec((None, bb, SSD_HEADS, SSD_HEAD_DIM, SSD_STATE), lambda i, c: (l, i, 0, 0, 0)),
                  pl.BlockSpec((None, bb, SUBLANES, SSD_CONV_DIM), lambda i, c: (l, i, 0, 0)),
                  layer(SSD_CONV, SSD_CONV_DIM), layer(1, SSD_CONV_DIM), layer(1, DT_PAD), layer(1, DT_PAD),
                  layer(1, SSD_WIDTH), layer(1, SSD_WIDTH),
                  pl.BlockSpec((DT_PAD, SSD_WIDTH), lambda i, c: (0, 0))] + [pl.BlockSpec(state_blk, smap)] * n_e,
        out_specs=[pl.BlockSpec((bb, T, SSD_WIDTH), tmap), state_out[0],
                   pl.BlockSpec((bb, SUBLANES, SSD_CONV_DIM), cvmap)],
        out_shape=[jax.ShapeDtypeStruct((nb, L, SSD_WIDTH), _F32), state_out[1],
                   jax.ShapeDtypeStruct(st_conv8.shape[1:], _F32)],
        scratch_shapes=[pltpu.VMEM((bb, T + SUBLANES, SSD_CONV_DIM), _F32), pltpu.VMEM((bb, T, SSD_WIDTH), _F32)],
        compiler_params=_params("arbitrary", "arbitrary"),
    )(z, xbc, dt, st_ssd, st_conv8, P['ssd_conv_w'], P['ssd_conv_b'], P['ssd_dt_bias'], P['ssd_a_log'],
      P['ssd_d'], P['ssd_norm_g'], P['ssd_head_expand'], *earlier)


def _stack_heads(x):
    return jnp.concatenate([x[:, h * RWKV_HEAD_DIM:(h + 1) * RWKV_HEAD_DIM] for h in range(RWKV_HEADS)], axis=0)


def _unstack_heads(x, T):
    return jnp.concatenate([x[h * T:(h + 1) * T, :] for h in range(RWKV_HEADS)], axis=1)


def _interleave(gens):
    gens = list(gens)
    while gens:
        alive = []
        for g in gens:
            try:
                next(g)
                alive.append(g)
            except StopIteration:
                pass
        gens = alive


def _rwkv_masks(T):
    HT = RWKV_HEADS * T
    rr = _iota((HT, HT), 0)
    cc = _iota((HT, HT), 1)
    same = (rr // T) == (cc // T)
    rt, ct = _iota((T, T), 0), _iota((T, T), 1)
    head_of_row = _iota((HT, RWKV_WIDTH), 0) // T
    head_of_col = _iota((HT, RWKV_WIDTH), 1) // RWKV_HEAD_DIM
    return dict(strict=jnp.logical_and(same, rr > cc), incl=jnp.logical_and(same, rr >= cc),
                eye=(rr == cc).astype(_F32), hmask=head_of_row == head_of_col,
                strict_t=rt > ct, incl_t=rt >= ct, eye_t=(rt == ct).astype(_F32), tri=(rt >= ct).astype(_BF16))


def _rwkv_inverse(eye, mats, T):
    inv = [eye + n for n in mats]
    p = list(mats)
    for _ in range(max((T - 1).bit_length() - 1, 0)):
        yield
        p = [_dot(x, x) for x in p]
        yield
        inv = [i + _dot(i, x) for i, x in zip(inv, p)]
    return inv


def _rwkv_core_per_head(j, masks, so_ref, T, at, rt, bt, kt, v4f, w_end):
    cast = lambda x: x.astype(_BF16)
    a4, r4, b4, k4 = (cast(_stack_heads(t)) for t in (at, rt, bt, kt))
    v4 = cast(v4f)
    bw4, kw4 = (cast(_stack_heads(t * w_end)) for t in (bt, kt))
    heads = range(RWKV_HEADS)
    blk = lambda x, h: x[h * T:(h + 1) * T]
    yield
    n_ab = [jnp.where(masks['strict_t'], _dot_nt(blk(a4, h), blk(b4, h)), 0.0) for h in heads]
    n_ak = [cast(jnp.where(masks['strict_t'], _dot_nt(blk(a4, h), blk(k4, h)), 0.0)) for h in heads]
    m_rb = [cast(jnp.where(masks['incl_t'], _dot_nt(blk(r4, h), blk(b4, h)), 0.0)) for h in heads]
    m_rk = [cast(jnp.where(masks['incl_t'], _dot_nt(blk(r4, h), blk(k4, h)), 0.0)) for h in heads]
    inv = yield from _rwkv_inverse(masks['eye_t'], n_ab, T)
    yield
    s0 = [so_ref[j, h] for h in heads]
    rhs = [_dot_nt(blk(a4, h), s0[h]) + _dot(n_ak[h], blk(v4, h)) for h in heads]
    o_own = [_dot_nt(blk(r4, h), s0[h]) + _dot(m_rk[h], blk(v4, h)) for h in heads]
    yield
    u = [cast(_dot(inv[h], rhs[h])) for h in heads]
    yield
    o4 = jnp.concatenate([o_own[h] + _dot(m_rb[h], u[h]) for h in heads], axis=0)
    for h in heads:
        so_ref[j, h] = (s0[h] * w_end[:, h * RWKV_HEAD_DIM:(h + 1) * RWKV_HEAD_DIM]
                        + _dot_tn(u[h], blk(bw4, h)) + _dot_tn(blk(v4, h), blk(kw4, h)))
    yield
    return o4


def _rwkv_core_stacked(j, masks, so_ref, T, at, rt, bt, kt, v4f, w_end):
    strict, incl, hmask = masks['strict'], masks['incl'], masks['hmask']
    a4, r4, b4, k4 = (_stack_heads(t) for t in (at, rt, bt, kt))
    yield
    n_ab = jnp.where(strict, _dot_nt(a4, b4), 0.0)
    n_ak = jnp.where(strict, _dot_nt(a4, k4), 0.0)
    m_rb = jnp.where(incl, _dot_nt(r4, b4), 0.0)
    m_rk = jnp.where(incl, _dot_nt(r4, k4), 0.0)
    (inv,) = yield from _rwkv_inverse(masks['eye'], [n_ab], T)

    def wide(x):
        return jnp.where(hmask, jnp.concatenate([x] * RWKV_HEADS, axis=0), 0.0)

    yield
    s0 = jnp.concatenate([so_ref[j, h] for h in range(RWKV_HEADS)], axis=1)
    rhs = _dot_nt(wide(at), s0) + _dot(n_ak, v4f)
    o4 = _dot_nt(wide(rt), s0) + _dot(m_rk, v4f)
    yield
    u4 = _dot(inv, rhs)
    yield
    o4 = o4 + _dot(m_rb, u4)
    s1 = (s0 * w_end + _dot_tn(u4, wide(bt * w_end)) + _dot_tn(v4f, wide(kt * w_end)))
    for h in range(RWKV_HEADS):
        so_ref[j, h] = s1[:, h * RWKV_HEAD_DIM:(h + 1) * RWKV_HEAD_DIM]
    yield
    return o4


def _rwkv_seq(j, masks, f_ref, mu_ref, w0_ref, w2_ref, a0_ref, a2_ref, g2_ref, kk_ref, ka_ref, rk_ref, lng_ref,
              lnb_ref, y_ref, so_ref, sho_ref, ext_scr, T, valid, row0=0):
    f = f_ref[j, row0:row0 + T, :]
    ext_scr[j, 0:SUBLANES, :] = sho_ref[j]
    ext_scr[j, SUBLANES:SUBLANES + T, :] = f
    prev = ext_scr[j, SUBLANES - 1:SUBLANES - 1 + T, :]
    sho_ref[j] = ext_scr[j, valid:valid + SUBLANES, :]
    fm = f + (prev - f) * mu_ref[...]
    r = fm[:, 0:256]
    k = fm[:, 256:512]
    v = fm[:, 512:768]
    lora = fm[:, 768:896]
    gl = fm[:, 896:1024]
    w_log = -_softplus(-(w0_ref[...] + _dot(jnp.tanh(lora), w2_ref[...]))) - 0.5
    logw = -jnp.exp(w_log)
    a = _sigmoid(a0_ref[...] + _dot(lora, a2_ref[...]))
    g = _dot(_sigmoid(gl), g2_ref[...])
    k2 = k * (1.0 + (a - 1.0) * ka_ref[...])
    kk4 = _stack_heads(k * kk_ref[...])
    kk4 = kk4 * lax.rsqrt(jnp.maximum(jnp.sum(kk4 * kk4, axis=-1, keepdims=True), 1e-24))
    kkn = _unstack_heads(kk4, T)
    if valid < T:
        live = _iota((T, 1), 0) < valid
        logw = jnp.where(live, logw, 0.0)
        kkn = jnp.where(live, kkn, 0.0)
        k2 = jnp.where(live, k2, 0.0)
    yield
    cw = _mm_sel(masks['tri'], logw, _NN)
    w_inc = jnp.exp(cw)
    w_exc = jnp.exp(cw - logw)
    w_inv = jnp.exp(-cw)
    w_end = w_inc[T - 1:T, :]
    at = -kkn * w_exc
    rt = r * w_inc
    bt = kkn * a * w_inv
    kt = k2 * w_inv
    v4f = _stack_heads(v)
    core = _rwkv_core_per_head if T % 16 == 0 else _rwkv_core_stacked
    o4 = yield from core(j, masks, so_ref, T, at, rt, bt, kt, v4f, w_end)

    mean = jnp.mean(o4, axis=-1, keepdims=True)
    var = jnp.mean(jnp.square(o4 - mean), axis=-1, keepdims=True)
    o4 = (o4 - mean) * lax.rsqrt(var + RWKV_LN_EPS)
    bonus4 = jnp.sum(_stack_heads(r * k2 * rk_ref[...]), axis=-1, keepdims=True) * v4f
    o = _unstack_heads(o4, T) * lng_ref[...] + lnb_ref[...] + _unstack_heads(bonus4, T)
    y_ref[j, row0:row0 + T, :] = o * g


def _rwkv_kernel(f_ref, s_ref, sh_ref, mu_ref, w0_ref, w2_ref, a0_ref, a2_ref, g2_ref, kk_ref, ka_ref, rk_ref,
                 lng_ref, lnb_ref, *rest, bb, T, valid, n_earlier):
    earlier = rest[:n_earlier]
    y_ref, so_all, sho_ref, ext_scr = rest[n_earlier:]
    so_ref = so_all.at[n_earlier] if n_earlier else so_all

    @pl.when(pl.program_id(1) == 0)
    def _():
        so_ref[...] = s_ref[...]
        sho_ref[...] = sh_ref[...]
        for k, e_ref in enumerate(earlier):
            so_all[k] = e_ref[...]

    seq = functools.partial(_rwkv_seq, f_ref=f_ref, mu_ref=mu_ref, w0_ref=w0_ref, w2_ref=w2_ref, a0_ref=a0_ref,
                            a2_ref=a2_ref, g2_ref=g2_ref, kk_ref=kk_ref, ka_ref=ka_ref, rk_ref=rk_ref,
                            lng_ref=lng_ref, lnb_ref=lnb_ref, y_ref=y_ref, so_ref=so_ref, sho_ref=sho_ref,
                            ext_scr=ext_scr, T=T, valid=valid)
    masks = _rwkv_masks(T)
    _interleave(seq(j, masks) for j in range(bb))


def _rwkv(f, st_rwkv, shift8, P, l, *, bb, T, valid, earlier=()):
    nb, L, _ = f.shape
    hd = RWKV_HEAD_DIM
    state_blk = (bb, RWKV_HEADS, hd, hd)
    smap = lambda i, c: (i, 0, 0, 0)
    n_e = len(earlier)
    if n_e:
        state_out = (pl.BlockSpec((n_e + 1,) + state_blk, lambda i, c: (0, i, 0, 0, 0)),
                     jax.ShapeDtypeStruct((n_e + 1,) + st_rwkv.shape[1:], _F32))
    else:
        state_out = (pl.BlockSpec(state_blk, smap), jax.ShapeDtypeStruct(st_rwkv.shape[1:], _F32))
    grid = (nb // bb, L // T)
    tmap = lambda i, c: (i, c, 0)
    bmap = lambda i, c: (i, 0, 0)
    layer = lambda r, n: pl.BlockSpec((None, r, n), lambda i, c: (l, 0, 0))
    row = lambda n: layer(1, n)
    lora_w = RWKV_DECAY_LORA + RWKV_A_LORA
    return pl.pallas_call(
        functools.partial(_rwkv_kernel, bb=bb, T=T, valid=valid, n_earlier=n_e),
        grid=grid,
        in_specs=[pl.BlockSpec((bb, T, RWKV_PROJ), tmap),
                  pl.BlockSpec((None, bb, RWKV_HEADS, hd, hd), lambda i, c: (l, i, 0, 0, 0)),
                  pl.BlockSpec((None, bb, SUBLANES, RWKV_PROJ), lambda i, c: (l, i, 0, 0)),
                  row(RWKV_PROJ), row(RWKV_WIDTH), layer(lora_w, RWKV_WIDTH),
                  row(RWKV_WIDTH), layer(lora_w, RWKV_WIDTH), layer(RWKV_GATE_LORA, RWKV_WIDTH),
                  row(RWKV_WIDTH), row(RWKV_WIDTH), row(RWKV_WIDTH), row(RWKV_WIDTH), row(RWKV_WIDTH)]
        + [pl.BlockSpec(state_blk, smap)] * n_e,
        out_specs=[pl.BlockSpec((bb, T, RWKV_WIDTH), tmap), state_out[0],
                   pl.BlockSpec((bb, SUBLANES, RWKV_PROJ), bmap)],
        out_shape=[jax.ShapeDtypeStruct((nb, L, RWKV_WIDTH), _F32), state_out[1],
                   jax.ShapeDtypeStruct(shift8.shape[1:], _F32)],
        scratch_shapes=[pltpu.VMEM((bb, T + SUBLANES, RWKV_PROJ), _F32)],
        compiler_params=_params("arbitrary", "arbitrary"),
    )(f, st_rwkv, shift8, P['rwkv_mu'], P['rwkv_w0'], P['rwkv_w2'], P['rwkv_a0'], P['rwkv_a2'], P['rwkv_g2'],
      P['rwkv_k_k'], P['rwkv_k_a'], P['rwkv_r_k'], P['rwkv_ln_g'], P['rwkv_ln_b'], *earlier)


def _mix_kernel(z_ref, xbc_ref, dt_ref, st_ref, cv_ref, cw_ref, cb_ref, dtb_ref, alog_ref, dexp_ref, ng_ref, eexp_ref,
                f_ref, s_ref, sh_ref, mu_ref, w0_ref, w2_ref, a0_ref, a2_ref, g2_ref, kk_ref, ka_ref, rk_ref,
                lng_ref, lnb_ref, *rest, bb, T, RT, n_earlier):
    e_ssd, e_rw = rest[:n_earlier], rest[n_earlier:2 * n_earlier]
    y_ref, sto_all, cvo_ref, yrw_ref, so_all, sho_ref, ext_scr, ybuf_scr, rext_scr = rest[2 * n_earlier:]
    sto_ref = sto_all.at[n_earlier] if n_earlier else sto_all
    so_ref = so_all.at[n_earlier] if n_earlier else so_all

    @pl.when(pl.program_id(1) == 0)
    def _():
        sto_ref[...] = st_ref[...]
        cvo_ref[...] = cv_ref[...]
        so_ref[...] = s_ref[...]
        sho_ref[...] = sh_ref[...]
        for k in range(n_earlier):
            sto_all[k] = e_ssd[k][...]
            so_all[k] = e_rw[k][...]

    ssd = functools.partial(_ssd_seq, z_ref=z_ref, xbc_ref=xbc_ref, dt_ref=dt_ref, cw_ref=cw_ref, cb_ref=cb_ref,
                            dtb_ref=dtb_ref, alog_ref=alog_ref, dexp_ref=dexp_ref, ng_ref=ng_ref, eexp_ref=eexp_ref,
                            y_ref=y_ref, sto_ref=sto_ref, cvo_ref=cvo_ref, ext_scr=ext_scr, ybuf_scr=ybuf_scr,
                            T=T, valid=T)
    rwkv = functools.partial(_rwkv_seq, f_ref=f_ref, mu_ref=mu_ref, w0_ref=w0_ref, w2_ref=w2_ref, a0_ref=a0_ref,
                             a2_ref=a2_ref, g2_ref=g2_ref, kk_ref=kk_ref, ka_ref=ka_ref, rk_ref=rk_ref,
                             lng_ref=lng_ref, lnb_ref=lnb_ref, y_ref=yrw_ref, so_ref=so_ref, sho_ref=sho_ref,
                             ext_scr=rext_scr, T=RT, valid=RT)
    masks = _rwkv_masks(RT)

    def rwkv_chunks(j):
        for c in range(T // RT):
            yield from rwkv(j, masks, row0=c * RT)

    _interleave([rwkv_chunks(j) for j in range(bb)] + [ssd(j) for j in range(bb)])


def _mix(z, xbc, dt, f, st_ssd, st_conv8, st_rwkv, shift8, P, l, *, bb, T, RT, earlier_ssd=(), earlier_rwkv=()):
    assert T % RT == 0
    nb, L, _ = z.shape
    hd = RWKV_HEAD_DIM
    grid = (nb // bb, L // T)
    tmap = lambda i, c: (i, c, 0)
    bmap = lambda i, c: (i, 0, 0)
    smap = lambda i, c: (i, 0, 0, 0)
    layer = lambda r, n: pl.BlockSpec((None, r, n), lambda i, c: (l, 0, 0))
    row = lambda n: layer(1, n)
    lora_w = RWKV_DECAY_LORA + RWKV_A_LORA
    ssd_blk = (bb, SSD_HEADS, SSD_HEAD_DIM, SSD_STATE)
    rw_blk = (bb, RWKV_HEADS, hd, hd)
    n_e = len(earlier_ssd)
    assert len(earlier_rwkv) == n_e

    def state_out(blk, full_shape):
        if n_e:
            return (pl.BlockSpec((n_e + 1,) + blk, lambda i, c: (0, i, 0, 0, 0)),
                    jax.ShapeDtypeStruct((n_e + 1,) + full_shape, _F32))
        return pl.BlockSpec(blk, smap), jax.ShapeDtypeStruct(full_shape, _F32)

    ssd_out, rw_out = state_out(ssd_blk, st_ssd.shape[1:]), state_out(rw_blk, st_rwkv.shape[1:])
    outs = pl.pallas_call(
        functools.partial(_mix_kernel, bb=bb, T=T, RT=RT, n_earlier=n_e),
        grid=grid,
        in_specs=[pl.BlockSpec((bb, T, SSD_WIDTH), tmap), pl.BlockSpec((bb, T, SSD_CONV_DIM), tmap),
                  pl.BlockSpec((bb, T, DT_PAD), tmap),
                  pl.BlockSpec((None,) + ssd_blk, lambda i, c: (l, i, 0, 0, 0)),
                  pl.BlockSpec((None, bb, SUBLANES, SSD_CONV_DIM), lambda i, c: (l, i, 0, 0)),
                  layer(SSD_CONV, SSD_CONV_DIM), layer(1, SSD_CONV_DIM), layer(1, DT_PAD), layer(1, DT_PAD),
                  layer(1, SSD_WIDTH), layer(1, SSD_WIDTH), pl.BlockSpec((DT_PAD, SSD_WIDTH), lambda i, c: (0, 0)),
                  pl.BlockSpec((bb, T, RWKV_PROJ), tmap),
                  pl.BlockSpec((None,) + rw_blk, lambda i, c: (l, i, 0, 0, 0)),
                  pl.BlockSpec((None, bb, SUBLANES, RWKV_PROJ), lambda i, c: (l, i, 0, 0)),
                  row(RWKV_PROJ), row(RWKV_WIDTH), layer(lora_w, RWKV_WIDTH),
                  row(RWKV_WIDTH), layer(lora_w, RWKV_WIDTH), layer(RWKV_GATE_LORA, RWKV_WIDTH),
                  row(RWKV_WIDTH), row(RWKV_WIDTH), row(RWKV_WIDTH), row(RWKV_WIDTH), row(RWKV_WIDTH)]
        + [pl.BlockSpec(ssd_blk, smap)] * n_e + [pl.BlockSpec(rw_blk, smap)] * n_e,
        out_specs=[pl.BlockSpec((bb, T, SSD_WIDTH), tmap), ssd_out[0],
                   pl.BlockSpec((bb, SUBLANES, SSD_CONV_DIM), bmap),
                   pl.BlockSpec((bb, T, RWKV_WIDTH), tmap), rw_out[0],
                   pl.BlockSpec((bb, SUBLANES, RWKV_PROJ), bmap)],
        out_shape=[jax.ShapeDtypeStruct((nb, L, SSD_WIDTH), _F32), ssd_out[1],
                   jax.ShapeDtypeStruct(st_conv8.shape[1:], _F32),
                   jax.ShapeDtypeStruct((nb, L, RWKV_WIDTH), _F32), rw_out[1],
                   jax.ShapeDtypeStruct(shift8.shape[1:], _F32)],
        scratch_shapes=[pltpu.VMEM((bb, T + SUBLANES, SSD_CONV_DIM), _F32), pltpu.VMEM((bb, T, SSD_WIDTH), _F32),
                        pltpu.VMEM((bb, RT + SUBLANES, RWKV_PROJ), _F32)],
        compiler_params=_params("arbitrary", "arbitrary"),
    )(z, xbc, dt, st_ssd, st_conv8, P['ssd_conv_w'], P['ssd_conv_b'], P['ssd_dt_bias'], P['ssd_a_log'],
      P['ssd_d'], P['ssd_norm_g'], P['ssd_head_expand'],
      f, st_rwkv, shift8, P['rwkv_mu'], P['rwkv_w0'], P['rwkv_w2'], P['rwkv_a0'], P['rwkv_a2'], P['rwkv_g2'],
      P['rwkv_k_k'], P['rwkv_k_a'], P['rwkv_r_k'], P['rwkv_ln_g'], P['rwkv_ln_b'], *earlier_ssd, *earlier_rwkv)
    return outs


def _s5_kernel(u_ref, sre_ref, sim_ref, lre_ref, lim_ref, dt_ref, bre_ref, bim_ref, cre_ref, cim_ref, d_ref,
               gw_ref, gb_ref, y_ref, hre_ref, him_ref, hre_scr, him_scr, *, nb, tl, final_step):
    @pl.when(pl.program_id(0) == 0)
    def _():
        hre_ref[...] = sre_ref[...]
        him_ref[...] = sim_ref[...]

    lre, lim, dt = lre_ref[...], lim_ref[...], jnp.exp(dt_ref[...])
    mag = jnp.exp(lre * dt)
    ang = lim * dt
    ab_re, ab_im = mag * jnp.cos(ang), mag * jnp.sin(ang)
    den = jnp.square(lre) + jnp.square(lim)
    q_re = ((ab_re - 1.0) * lre + ab_im * lim) / den
    q_im = (ab_im * lre - (ab_re - 1.0) * lim) / den
    b_re, b_im = bre_ref[...], bim_ref[...]
    bb_re = q_re * b_re - q_im * b_im
    bb_im = q_re * b_im + q_im * b_re
    ar = jnp.broadcast_to(ab_re, (nb, S5_FLAT))
    ai = jnp.broadcast_to(ab_im, (nb, S5_FLAT))

    def readout(h_re, h_im, ub):
        y = _dot(h_re, cre_ref[...]) - _dot(h_im, cim_ref[...]) + d_ref[...] * ub
        y = 0.5 * y * (1.0 + jnp.tanh(0.7978845608028654 * (y + 0.044715 * (y * y * y))))
        return y * _sigmoid(_dot(y, gw_ref[...]) + gb_ref[...])

    if nb <= SUBLANES and tl % (2 * S5_BLOCK_STEPS) == 0:
        half = tl // 2
        bb_re, bb_im = bb_re.astype(_BF16), bb_im.astype(_BF16)
        hrows = lambda t0, t1: slice((t0 + 1) * nb, (t1 + 1) * nb)
        urows = lambda t0, t1: slice(t0 * nb, t1 * nb)
        carry = [(hre_ref[...], him_ref[...])]

        def project(t0, t1):
            for a in range(t0, t1, S5_BLOCK_STEPS):
                ub = u_ref[urows(a, a + S5_BLOCK_STEPS), :]
                hre_scr[hrows(a, a + S5_BLOCK_STEPS), :] = _dot(ub, bb_re)
                him_scr[hrows(a, a + S5_BLOCK_STEPS), :] = _dot(ub, bb_im)
                yield

        def scan(t0, t1):
            pr, pi = carry[0]
            for s in range(t0, t1):
                cur = hrows(s, s + 1)
                nr = ar * pr - ai * pi + hre_scr[cur, :]
                ni = ar * pi + ai * pr + him_scr[cur, :]
                hre_scr[cur, :] = nr
                him_scr[cur, :] = ni
                pr, pi = nr, ni
                if (s + 1 - t0) % S5_BLOCK_STEPS == 0:
                    yield
            carry[0] = (pr, pi)

        def emit(t0, t1):
            for a in range(t0, t1, S5_BLOCK_STEPS):
                hr, ur = hrows(a, a + S5_BLOCK_STEPS), urows(a, a + S5_BLOCK_STEPS)
                y_ref[ur, :] = readout(hre_scr[hr, :], him_scr[hr, :], u_ref[ur, :])
                yield

        _interleave([project(0, half)])
        _interleave([scan(0, half), project(half, tl)])
        _interleave([scan(half, tl), emit(0, half)])
        _interleave([emit(half, tl)])
    else:
        u = u_ref[...]
        hre_scr[0:nb, :] = hre_ref[...]
        him_scr[0:nb, :] = him_ref[...]
        hre_scr[nb:, :] = _dot(u, bb_re)
        him_scr[nb:, :] = _dot(u, bb_im)

        def step(s, carry):
            prev = pl.ds(pl.multiple_of(s * nb, nb), nb)
            cur = pl.ds(pl.multiple_of((s + 1) * nb, nb), nb)
            pr, pi = hre_scr[prev, :], him_scr[prev, :]
            hre_scr[cur, :] = ar * pr - ai * pi + hre_scr[cur, :]
            him_scr[cur, :] = ar * pi + ai * pr + him_scr[cur, :]
            return carry

        lax.fori_loop(0, tl, step, 0)
        y_ref[...] = readout(hre_scr[nb:, :], him_scr[nb:, :], u)
    hre_ref[...] = hre_scr[(final_step + 1) * nb:(final_step + 2) * nb, :]
    him_ref[...] = him_scr[(final_step + 1) * nb:(final_step + 2) * nb, :]


def _s5(u_tm, st_re, st_im, P, l, *, nb, tl, final_step):
    rows = u_tm.shape[0]
    L = rows // nb
    layer = lambda r, n: pl.BlockSpec((None, r, n), lambda t: (l, 0, 0))
    row = lambda n: layer(1, n)
    full = lambda a, b: pl.BlockSpec((a, b), lambda t: (0, 0))
    state = pl.BlockSpec((None, nb, S5_FLAT), lambda t: (l, 0, 0))
    return pl.pallas_call(
        functools.partial(_s5_kernel, nb=nb, tl=tl, final_step=final_step),
        grid=(L // tl,),
        in_specs=[pl.BlockSpec((tl * nb, S5_WIDTH), lambda t: (t, 0)), state, state,
                  row(S5_FLAT), row(S5_FLAT), row(S5_FLAT),
                  layer(S5_WIDTH, S5_FLAT), layer(S5_WIDTH, S5_FLAT), layer(S5_FLAT, S5_WIDTH),
                  layer(S5_FLAT, S5_WIDTH), row(S5_WIDTH), layer(S5_WIDTH, S5_WIDTH), row(S5_WIDTH)],
        out_specs=[pl.BlockSpec((tl * nb, S5_WIDTH), lambda t: (t, 0)), full(nb, S5_FLAT), full(nb, S5_FLAT)],
        out_shape=[jax.ShapeDtypeStruct((rows, S5_WIDTH), _F32),
                   jax.ShapeDtypeStruct((nb, S5_FLAT), _F32), jax.ShapeDtypeStruct((nb, S5_FLAT), _F32)],
        scratch_shapes=[pltpu.VMEM(((tl + 1) * nb, S5_FLAT), _F32), pltpu.VMEM(((tl + 1) * nb, S5_FLAT), _F32)],
        compiler_params=_params("arbitrary"),
    )(u_tm, st_re, st_im, P['s5_a_re'], P['s5_a_im'], P['s5_log_dt'], P['s5_b_re'], P['s5_b_im'], P['s5_c_re'],
      P['s5_c_im'], P['s5_d'], P['s5_glu_w'], P['s5_glu_b'])


def _out_mlp_kernel(x_ref, yssd_ref, yrw_ref, ys5_ref, g1_ref, sh2_ref, sc2_ref, g2_ref, n2g_ref, fg_ref,
                    wo_ref, w1_ref, w2_ref, o_ref, hid_scr, *, final_norm):
    bb, tl, _ = x_ref.shape
    rows = bb * tl
    flat = lambda ref: ref[...].reshape(rows, ref.shape[-1]).astype(_BF16)
    mix = (_dot(flat(yssd_ref), wo_ref[0:512, :]) + _dot(flat(yrw_ref), wo_ref[512:768, :])
           + _dot(flat(ys5_ref), wo_ref[768:1024, :]))
    x1 = x_ref[...] + g1_ref[...] * mix.reshape(bb, tl, D_MODEL)
    xn = x1 * lax.rsqrt(jnp.mean(x1 * x1, axis=-1, keepdims=True) + NORM_EPS)
    h2 = (xn * n2g_ref[...] * (1.0 + sc2_ref[...]) + sh2_ref[...]).reshape(rows, D_MODEL).astype(_BF16)
    for c in range(D_FF // FF_TILE):
        cols = slice(c * FF_TILE, (c + 1) * FF_TILE)
        hid_scr[:, cols] = jnp.square(jnp.maximum(_dot(h2, w1_ref[:, cols]), 0.0)).astype(_BF16)
    ff = _dot(hid_scr[...], w2_ref[...])
    out = x1 + g2_ref[...] * ff.reshape(bb, tl, D_MODEL)
    if final_norm:
        out = out * lax.rsqrt(jnp.mean(out * out, axis=-1, keepdims=True) + NORM_EPS) * fg_ref[...]
    o_ref[...] = out


def _out_mlp(x, y_ssd, y_rw, y_s5, mod, P, l, *, bb, tl, time_major_s5, final_norm):
    nb, L, _ = x.shape
    grid = (nb // bb, L // tl)
    xmap = lambda i, t: (i, t, 0)
    cmap = lambda i, t: (0, 0)
    wmap = lambda i, t: (l, 0, 0)
    once = pl.Buffered(1)

    def modspec(j):
        return pl.BlockSpec((bb, 1, D_MODEL), lambda i, t: (i, 0, j))

    s5_spec = (pl.BlockSpec((tl, S5_WIDTH), lambda i, t: (t, i)) if time_major_s5
               else pl.BlockSpec((bb, tl, S5_WIDTH), xmap))
    return pl.pallas_call(
        functools.partial(_out_mlp_kernel, final_norm=final_norm),
        grid=grid,
        in_specs=[pl.BlockSpec((bb, tl, D_MODEL), xmap), pl.BlockSpec((bb, tl, SSD_WIDTH), xmap),
                  pl.BlockSpec((bb, tl, RWKV_WIDTH), xmap), s5_spec,
                  modspec(2), modspec(3), modspec(4), modspec(5),
                  pl.BlockSpec((None, 1, D_MODEL), wmap), pl.BlockSpec((1, D_MODEL), cmap),
                  pl.BlockSpec((None, D_MODEL, D_MODEL), wmap, pipeline_mode=once),
                  pl.BlockSpec((None, D_MODEL, D_FF), wmap, pipeline_mode=once),
                  pl.BlockSpec((None, D_FF, D_MODEL), wmap, pipeline_mode=once)],
        out_specs=pl.BlockSpec((bb, tl, D_MODEL), xmap),
        out_shape=jax.ShapeDtypeStruct(x.shape, _F32),
        scratch_shapes=[pltpu.VMEM((bb * tl, D_FF), _BF16)],
        compiler_params=_params("arbitrary", "arbitrary"),
    )(x, y_ssd, y_rw, y_s5, mod, mod, mod, mod, P['norm2_g'], P['final_g'], P['w_out'], P['mlp_w1'], P['mlp_w2'])


def _trunk(x, mod, states, P, valid, *, prompt):
    st_ssd, st_conv, st_rwkv, st_shift, st_re, st_im = states
    nb, L, _ = x.shape
    if prompt:
        bb, tl, mlp_tl = 1, min(IN_ROW_TILE, L), min(MLP_ROW_TILE, L)
        ssd_t, rwkv_t, s5_tl = min(SSD_CHUNK, L), min(RWKV_CHUNK, L), min(S5_TILE, L)
    else:
        bb, tl, mlp_tl = min(nb, ROW_TILE // L), L, L
        ssd_t = rwkv_t = s5_tl = L
    new = ([], [], [], [], [], [])
    for l in range(DEPTH):
        z, xbc, frw, us5, dt = _in_proj(x, mod[l], P, l, bb=bb, tl=tl, time_major_s5=prompt)
        v = valid
        last = l == DEPTH - 1
        e_ssd, e_rwkv = (tuple(new[0]), tuple(new[2])) if last else ((), ())
        if prompt:
            y_ssd, n_ssd, n_conv, y_rw, n_rwkv, n_shift = _mix(
                z, xbc, dt, frw, st_ssd, st_conv, st_rwkv, st_shift, P, l, bb=min(nb, MIX_PROMPT_SEQS), T=ssd_t,
                RT=rwkv_t, earlier_ssd=e_ssd, earlier_rwkv=e_rwkv)
        else:
            sample_bb = min(nb, SAMPLE_SEQS)
            y_ssd, n_ssd, n_conv = _ssd(z, xbc, dt, st_ssd, st_conv, P, l, bb=sample_bb, T=ssd_t, valid=v,
                                        earlier=e_ssd)
            y_rw, n_rwkv, n_shift = _rwkv(frw, st_rwkv, st_shift, P, l, bb=sample_bb, T=rwkv_t, valid=v,
                                          earlier=e_rwkv)
        if prompt:
            u_tm = us5.reshape(L * nb, S5_WIDTH)
        else:
            u_tm = jnp.transpose(us5, (1, 0, 2)).reshape(L * nb, S5_WIDTH)
        y_s5, n_re, n_im = _s5(u_tm, st_re, st_im, P, l, nb=nb, tl=s5_tl,
                               final_step=(s5_tl - 1) if prompt else (v - 1))
        if prompt:
            y_s5 = y_s5.reshape(L, nb * S5_WIDTH)
        else:
            y_s5 = jnp.transpose(y_s5.reshape(L, nb, S5_WIDTH), (1, 0, 2))
        x = _out_mlp(x, y_ssd, y_rw, y_s5, mod[l], P, l, bb=bb, tl=mlp_tl, time_major_s5=prompt,
                     final_norm=(l == DEPTH - 1))
        for lst, s in zip(new, (n_ssd, n_conv, n_rwkv, n_shift, n_re, n_im)):
            lst.append(s)
    by_kernel = (0, 2) if DEPTH > 1 else ()
    return x, [lst[-1] if k in by_kernel else jnp.stack(lst) for k, lst in enumerate(new)]


def _pack_states(st_ssd, st_conv, st_rwkv, st_shift, st_re, st_im):
    d, nb = st_ssd.shape[:2]
    conv8 = jnp.pad(st_conv, ((0, 0), (0, 0), (SUBLANES - (SSD_CONV - 1), 0), (0, 0)))
    shift8 = jnp.pad(st_shift[:, :, None, :], ((0, 0), (0, 0), (SUBLANES - 1, 0), (0, 0)))
    return (st_ssd, conv8, st_rwkv, shift8, st_re.reshape(d, nb, S5_FLAT), st_im.reshape(d, nb, S5_FLAT))


def _unpack_states(n_ssd, n_conv8, n_rwkv, n_shift8, n_re, n_im):
    d, nb = n_ssd.shape[:2]
    return (n_ssd, n_conv8[:, :, SUBLANES - (SSD_CONV - 1):, :], n_rwkv, n_shift8[:, :, SUBLANES - 1, :],
            n_re.reshape(d, nb, S5_GROUPS, S5_STATE), n_im.reshape(d, nb, S5_GROUPS, S5_STATE))


def _prepare_params(norm1_g, norm2_g, w_in, ssd_conv_w, ssd_conv_b, ssd_dt_bias, ssd_a_log, ssd_d, ssd_norm_g, rwkv_mu,
                    rwkv_w0, rwkv_w2, rwkv_a0, rwkv_a2, rwkv_g2, rwkv_k_k, rwkv_k_a, rwkv_r_k, rwkv_ln_g, rwkv_ln_b,
                    s5_a_re, s5_a_im, s5_log_dt, s5_b_re, s5_b_im, s5_c_re, s5_c_im, s5_d, s5_glu_w, s5_glu_b, w_out,
                    mlp_w1, mlp_w2, final_g):
    rows = lambda a: a.reshape(DEPTH, 1, -1)
    head_pad = ((0, 0), (0, DT_PAD - SSD_HEADS))
    eye = jnp.eye(S5_GROUPS, dtype=_F32)
    zeros_lora = jnp.zeros((DEPTH, RWKV_A_LORA, RWKV_WIDTH), _F32)
    w_in_p = jnp.concatenate(
        [w_in[:, :, _Z0:_DT0], w_in[:, :, _FRW0:_IN_COLS], w_in[:, :, _DT0:_FRW0],
         jnp.zeros((DEPTH, D_MODEL, DT_PAD - SSD_HEADS), w_in.dtype)], axis=-1).astype(_BF16)
    return dict(
        norm1_g=rows(norm1_g), norm2_g=rows(norm2_g), final_g=final_g.reshape(1, D_MODEL), w_in_p=w_in_p,
        w_out=w_out.astype(_BF16), mlp_w1=mlp_w1.astype(_BF16), mlp_w2=mlp_w2.astype(_BF16),
        ssd_conv_w=ssd_conv_w, ssd_conv_b=rows(ssd_conv_b), ssd_dt_bias=rows(jnp.pad(ssd_dt_bias, head_pad)),
        ssd_a_log=rows(jnp.pad(ssd_a_log, head_pad)), ssd_d=rows(jnp.repeat(ssd_d, SSD_HEAD_DIM, axis=-1)),
        ssd_norm_g=rows(ssd_norm_g),
        ssd_head_expand=jnp.repeat(jnp.eye(DT_PAD, SSD_HEADS, dtype=_F32), SSD_HEAD_DIM, axis=1),
        rwkv_mu=rows(rwkv_mu), rwkv_w0=rows(rwkv_w0), rwkv_a0=rows(rwkv_a0), rwkv_k_k=rows(rwkv_k_k),
        rwkv_k_a=rows(rwkv_k_a), rwkv_r_k=rows(rwkv_r_k), rwkv_ln_g=rows(rwkv_ln_g), rwkv_ln_b=rows(rwkv_ln_b),
        rwkv_w2=jnp.concatenate([rwkv_w2, zeros_lora], axis=1),
        rwkv_a2=jnp.concatenate([zeros_lora, rwkv_a2], axis=1),
        rwkv_g2=rwkv_g2,
        s5_a_re=rows(s5_a_re), s5_a_im=rows(s5_a_im), s5_log_dt=rows(jnp.repeat(s5_log_dt, S5_STATE, axis=-1)),
        s5_b_re=jnp.einsum('lgpc,gh->lgchp', s5_b_re, eye).reshape(DEPTH, S5_WIDTH, S5_FLAT),
        s5_b_im=jnp.einsum('lgpc,gh->lgchp', s5_b_im, eye).reshape(DEPTH, S5_WIDTH, S5_FLAT),
        s5_c_re=jnp.einsum('lgcp,gh->lgphc', s5_c_re, eye).reshape(DEPTH, S5_FLAT, S5_WIDTH),
        s5_c_im=jnp.einsum('lgcp,gh->lgphc', s5_c_im, eye).reshape(DEPTH, S5_FLAT, S5_WIDTH),
        s5_d=rows(s5_d), s5_glu_w=s5_glu_w, s5_glu_b=rows(s5_glu_b))


def kernel(x_prompt, x_sample, c_prompt, c_sample, state_ssd, state_ssd_conv, state_rwkv, state_rwkv_shift,
           state_s5_re, state_s5_im, ada_w, ada_b, norm1_g, norm2_g, w_in, ssd_conv_w, ssd_conv_b, ssd_dt_bias,
           ssd_a_log, ssd_d, ssd_norm_g, rwkv_mu, rwkv_w0, rwkv_w2, rwkv_a0, rwkv_a2, rwkv_g2, rwkv_k_k, rwkv_k_a,
           rwkv_r_k, rwkv_ln_g, rwkv_ln_b, s5_a_re, s5_a_im, s5_log_dt, s5_b_re, s5_b_im, s5_c_re, s5_c_im, s5_d,
           s5_glu_w, s5_glu_b, w_out, mlp_w1, mlp_w2, final_g):
    bp, lp, _ = x_prompt.shape
    bs, ls, _ = x_sample.shape
    ls_pad = -(-ls // SUBLANES) * SUBLANES
    P = _prepare_params(norm1_g, norm2_g, w_in, ssd_conv_w, ssd_conv_b, ssd_dt_bias, ssd_a_log, ssd_d, ssd_norm_g,
                        rwkv_mu, rwkv_w0, rwkv_w2, rwkv_a0, rwkv_a2, rwkv_g2, rwkv_k_k, rwkv_k_a, rwkv_r_k, rwkv_ln_g,
                        rwkv_ln_b, s5_a_re, s5_a_im, s5_log_dt, s5_b_re, s5_b_im, s5_c_re, s5_c_im, s5_d, s5_glu_w,
                        s5_glu_b, w_out, mlp_w1, mlp_w2, final_g)
    mod = _ada(jnp.concatenate([c_prompt, c_sample], axis=0), ada_w, ada_b)
    mod_p = mod[:, :bp].reshape(DEPTH, bp, 1, 6 * D_MODEL)
    mod_s = mod[:, bp:].reshape(DEPTH, bs, 1, 6 * D_MODEL)

    zeros = lambda shape: jnp.zeros((DEPTH, bp) + shape, _F32)
    p_states = _pack_states(zeros((SSD_HEADS, SSD_HEAD_DIM, SSD_STATE)), zeros((SSD_CONV - 1, SSD_CONV_DIM)),
                            zeros((RWKV_HEADS, RWKV_HEAD_DIM, RWKV_HEAD_DIM)), zeros((RWKV_PROJ,)),
                            zeros((S5_GROUPS, S5_STATE)), zeros((S5_GROUPS, S5_STATE)))
    y_prompt, sp = _trunk(x_prompt, mod_p, p_states, P, ls, prompt=True)

    s_states = _pack_states(state_ssd, state_ssd_conv, state_rwkv, state_rwkv_shift, state_s5_re, state_s5_im)
    x_s = jnp.pad(x_sample, ((0, 0), (0, ls_pad - ls), (0, 0)))
    y_sample, ss = _trunk(x_s, mod_s, s_states, P, ls, prompt=False)
    return (y_prompt, y_sample[:, :ls]) + _unpack_states(*sp) + _unpack_states(*ss)
```

```python
import functools

import jax
import jax.numpy as jnp
from jax import lax
from jax.experimental import pallas as pl
from jax.experimental.pallas import tpu as pltpu

D_MODEL = 1024
DEPTH = 2
SSD_WIDTH = 512
SSD_HEAD_DIM = 64
SSD_HEADS = 8
SSD_GROUPS = 2
SSD_HEADS_PER_GROUP = SSD_HEADS // SSD_GROUPS
SSD_STATE = 128
SSD_CONV = 4
SSD_CONV_DIM = SSD_WIDTH + 2 * SSD_GROUPS * SSD_STATE
RWKV_WIDTH = 256
RWKV_HEAD_DIM = 64
RWKV_HEADS = 4
RWKV_DECAY_LORA = 64
RWKV_A_LORA = 64
RWKV_GATE_LORA = 128
RWKV_PROJ = 3 * RWKV_WIDTH + RWKV_DECAY_LORA + RWKV_A_LORA + RWKV_GATE_LORA
RWKV_LN_EPS = 64e-5
S5_WIDTH = 256
S5_GROUP_CH = 16
S5_GROUPS = 16
S5_STATE = 64
S5_FLAT = S5_GROUPS * S5_STATE
D_FF = 4 * D_MODEL
NORM_EPS = 1e-6
DT_PAD = 128
SUBLANES = 8

_Z0, _XBC0, _DT0, _FRW0, _US50 = 0, SSD_WIDTH, SSD_WIDTH + SSD_CONV_DIM, SSD_WIDTH + SSD_CONV_DIM + SSD_HEADS, \
    SSD_WIDTH + SSD_CONV_DIM + SSD_HEADS + RWKV_PROJ
_IN_COLS = _US50 + S5_WIDTH
_PZ, _PXBC, _PFRW, _PUS5, _PDT = 0, 512, 1536, 2560, 2816
_P_COLS = _PDT + DT_PAD

SSD_CHUNK = 128
RWKV_CHUNK = 64
MIX_PROMPT_SEQS = 4
SAMPLE_SEQS = 8
S5_TILE = 256
S5_BLOCK_STEPS = 64
ROW_TILE = 512
IN_ROW_TILE = 1024
MLP_ROW_TILE = 512
FF_TILE = 1024
VMEM_LIMIT = 56 * 1024 * 1024

_F32 = jnp.float32
_BF16 = jnp.bfloat16


_NN = (((1,), (0,)), ((), ()))
_NT = (((1,), (1,)), ((), ()))
_TN = (((0,), (0,)), ((), ()))


def _mm(a, b, dims):
    return lax.dot_general(a.astype(_BF16), b.astype(_BF16), dims, preferred_element_type=_F32)


def _dot(a, b):
    return _mm(a, b, _NN)


def _dot_nt(a, b):
    return _mm(a, b, _NT)


def _dot_tn(a, b):
    return _mm(a, b, _TN)


def _mm_sel(sel, x, dims, x_is_rhs=True):
    hi = x.astype(_BF16)
    rest = x - hi.astype(_F32)
    mid = rest.astype(_BF16)
    lo = (rest - mid.astype(_F32)).astype(_BF16)
    sel = sel.astype(_BF16)
    if x_is_rhs:
        dg = lambda y: lax.dot_general(sel, y, dims, preferred_element_type=_F32)
    else:
        dg = lambda y: lax.dot_general(y, sel, dims, preferred_element_type=_F32)
    return dg(hi) + (dg(mid) + dg(lo))


def _iota(shape, dim):
    return lax.broadcasted_iota(jnp.int32, shape, dim)


def _sigmoid(x):
    return 1.0 / (1.0 + jnp.exp(-x))


def _softplus(x):
    return jnp.maximum(x, 0.0) + jnp.log1p(jnp.exp(-jnp.abs(x)))


def _silu(x):
    return x * _sigmoid(x)


def _params(*sems):
    return pltpu.CompilerParams(dimension_semantics=sems, vmem_limit_bytes=VMEM_LIMIT)


def _ada_kernel(c_ref, w_ref, b_ref, o_ref):
    c = c_ref[...]
    s = _silu(c).astype(_BF16)
    o_ref[0] = _dot(s, w_ref[0].astype(_BF16)) + b_ref[0]


def _ada(c_all, ada_w, ada_b):
    rows = c_all.shape[0]
    tn = 1536
    return pl.pallas_call(
        _ada_kernel,
        grid=(DEPTH, 6 * D_MODEL // tn),
        in_specs=[pl.BlockSpec((rows, D_MODEL), lambda l, j: (0, 0)),
                  pl.BlockSpec((1, D_MODEL, tn), lambda l, j: (l, 0, j)),
                  pl.BlockSpec((1, 1, tn), lambda l, j: (l, 0, j))],
        out_specs=pl.BlockSpec((1, rows, tn), lambda l, j: (l, 0, j)),
        out_shape=jax.ShapeDtypeStruct((DEPTH, rows, 6 * D_MODEL), _F32),
        compiler_params=_params("arbitrary", "arbitrary"),
    )(c_all, ada_w, ada_b.reshape(DEPTH, 1, 6 * D_MODEL))


def _in_proj_kernel(x_ref, sh_ref, sc_ref, g_ref, w_ref, z_ref, xbc_ref, frw_ref, us5_ref, dt_ref):
    x = x_ref[...]
    bb, tl, _ = x.shape
    xn = x * lax.rsqrt(jnp.mean(x * x, axis=-1, keepdims=True) + NORM_EPS)
    h = xn * g_ref[...] * (1.0 + sc_ref[...]) + sh_ref[...]
    h = h.reshape(bb * tl, D_MODEL).astype(_BF16)
    for o_ref, lo, hi in ((z_ref, _PZ, _PXBC), (xbc_ref, _PXBC, _PFRW), (frw_ref, _PFRW, _PUS5),
                          (us5_ref, _PUS5, _PDT), (dt_ref, _PDT, _P_COLS)):
        o_ref[...] = _dot(h, w_ref[:, lo:hi]).reshape(o_ref.shape)


def _in_proj(x, mod, P, l, *, bb, tl, time_major_s5):
    nb, L, _ = x.shape
    grid = (nb // bb, L // tl)
    xmap = lambda i, t: (i, t, 0)

    def modspec(j):
        return pl.BlockSpec((bb, 1, D_MODEL), lambda i, t: (i, 0, j))

    def out(n):
        return jax.ShapeDtypeStruct((nb, L, n), _F32), pl.BlockSpec((bb, tl, n), xmap)

    outs = [out(SSD_WIDTH), out(SSD_CONV_DIM), out(RWKV_PROJ), out(S5_WIDTH), out(DT_PAD)]
    if time_major_s5:
        assert bb == 1
        outs[3] = (jax.ShapeDtypeStruct((L, nb * S5_WIDTH), _F32), pl.BlockSpec((tl, S5_WIDTH), lambda i, t: (t, i)))
    return pl.pallas_call(
        _in_proj_kernel,
        grid=grid,
        in_specs=[pl.BlockSpec((bb, tl, D_MODEL), xmap), modspec(0), modspec(1),
                  pl.BlockSpec((None, 1, D_MODEL), lambda i, t: (l, 0, 0)),
                  pl.BlockSpec((None, D_MODEL, _P_COLS), lambda i, t: (l, 0, 0))],
        out_specs=[o[1] for o in outs],
        out_shape=[o[0] for o in outs],
        compiler_params=_params("arbitrary", "arbitrary"),
    )(x, mod, mod, P['norm1_g'], P['w_in_p'])


def _ssd_seq(j, z_ref, xbc_ref, dt_ref, cw_ref, cb_ref, dtb_ref, alog_ref, dexp_ref, ng_ref, eexp_ref,
             y_ref, sto_ref, cvo_ref, ext_scr, ybuf_scr, T, valid):
    u = xbc_ref[j]
    ext_scr[j, 0:SUBLANES, :] = cvo_ref[j]
    ext_scr[j, SUBLANES:SUBLANES + T, :] = u
    conv = cb_ref[...] + cw_ref[3:4, :] * u
    for back in (1, 2, 3):
        conv = conv + cw_ref[3 - back:4 - back, :] * ext_scr[j, SUBLANES - back:SUBLANES - back + T, :]
    cvo_ref[j] = ext_scr[j, valid:valid + SUBLANES, :]
    xbc = _silu(conv)
    xs = xbc[:, :SSD_WIDTH]
    row = _iota((T, 1), 0)
    dtv = _softplus(dt_ref[j] + dtb_ref[...])
    if valid < T:
        dtv = jnp.where(row < valid, dtv, 0.0)
    a_row = jnp.where(_iota((1, DT_PAD), 1) < SSD_HEADS, -jnp.exp(alog_ref[...]), 0.0)
    a = dtv * a_row
    yield
    tri_b = _iota((T, T), 0) >= _iota((T, T), 1)
    cs = _mm_sel(tri_b, a, _NN)
    eye = _iota((DT_PAD, DT_PAD), 0) == _iota((DT_PAD, DT_PAD), 1)
    xdt = xs * _mm_sel(eexp_ref[...], dtv, _NN, x_is_rhs=False)
    yield
    cs_t = _mm_sel(eye, cs, _NT)
    ecs = jnp.exp(cs)
    cs_last = cs[T - 1:T, :]
    dte = jnp.exp(cs_last - cs)
    elast = jnp.exp(cs_last)
    for g in range(SSD_GROUPS):
        bm = xbc[:, SSD_WIDTH + g * SSD_STATE:SSD_WIDTH + (g + 1) * SSD_STATE]
        cm = xbc[:, SSD_WIDTH + (SSD_GROUPS + g) * SSD_STATE:SSD_WIDTH + (SSD_GROUPS + g + 1) * SSD_STATE]
        cb = _dot_nt(cm, bm)
        for r in range(SSD_HEADS_PER_GROUP):
            if r == 0 or T >= 64:
                yield
            h = g * SSD_HEADS_PER_GROUP + r
            lo, hi = h * SSD_HEAD_DIM, (h + 1) * SSD_HEAD_DIM
            decay = jnp.exp(jnp.where(tri_b, cs[:, h:h + 1] - cs_t[h:h + 1, :], -jnp.inf))
            xdt_h = xdt[:, lo:hi]
            st = sto_ref[j, h]
            y_h = _dot(cb * decay, xdt_h) + ecs[:, h:h + 1] * _dot_nt(cm, st)
            ybuf_scr[j, :, lo:hi] = y_h
            sto_ref[j, h] = elast[:, h:h + 1] * st + _dot_tn(xdt_h * dte[:, h:h + 1], bm)
    yield
    y = ybuf_scr[j] + xs * dexp_ref[...]
    y = y * _silu(z_ref[j])
    y = y * lax.rsqrt(jnp.mean(y * y, axis=-1, keepdims=True) + NORM_EPS) * ng_ref[...]
    y_ref[j] = y


def _ssd_kernel(z_ref, xbc_ref, dt_ref, st_ref, cv_ref, cw_ref, cb_ref, dtb_ref, alog_ref, dexp_ref, ng_ref,
                eexp_ref, *rest, bb, T, valid, n_earlier):
    earlier = rest[:n_earlier]
    y_ref, sto_all, cvo_ref, ext_scr, ybuf_scr = rest[n_earlier:]
    sto_ref = sto_all.at[n_earlier] if n_earlier else sto_all

    @pl.when(pl.program_id(1) == 0)
    def _():
        sto_ref[...] = st_ref[...]
        cvo_ref[...] = cv_ref[...]
        for k, e_ref in enumerate(earlier):
            sto_all[k] = e_ref[...]

    seq = functools.partial(_ssd_seq, z_ref=z_ref, xbc_ref=xbc_ref, dt_ref=dt_ref, cw_ref=cw_ref, cb_ref=cb_ref,
                            dtb_ref=dtb_ref, alog_ref=alog_ref, dexp_ref=dexp_ref, ng_ref=ng_ref, eexp_ref=eexp_ref,
                            y_ref=y_ref, sto_ref=sto_ref, cvo_ref=cvo_ref, ext_scr=ext_scr, ybuf_scr=ybuf_scr,
                            T=T, valid=valid)
    _interleave(seq(j) for j in range(bb))


def _ssd(z, xbc, dt, st_ssd, st_conv8, P, l, *, bb, T, valid, earlier=()):
    nb, L, _ = z.shape
    grid = (nb // bb, L // T)
    tmap = lambda i, c: (i, c, 0)
    smap = lambda i, c: (i, 0, 0, 0)
    cvmap = lambda i, c: (i, 0, 0)
    layer = lambda r, n: pl.BlockSpec((None, r, n), lambda i, c: (l, 0, 0))
    state_blk = (bb, SSD_HEADS, SSD_HEAD_DIM, SSD_STATE)
    n_e = len(earlier)
    if n_e:
        state_out = (pl.BlockSpec((n_e + 1,) + state_blk, lambda i, c: (0, i, 0, 0, 0)),
                     jax.ShapeDtypeStruct((n_e + 1,) + st_ssd.shape[1:], _F32))
    else:
        state_out = (pl.BlockSpec(state_blk, smap), jax.ShapeDtypeStruct(st_ssd.shape[1:], _F32))
    return pl.pallas_call(
        functools.partial(_ssd_kernel, bb=bb, T=T, valid=valid, n_earlier=n_e),
        grid=grid,
        in_specs=[pl.BlockSpec((bb, T, SSD_WIDTH), tmap), pl.BlockSpec((bb, T, SSD_CONV_DIM), tmap),
                  pl.BlockSpec((bb, T, DT_PAD), tmap),
                  pl.BlockSpec((None, bb, SSD_HEADS, SSD_HEAD_DIM, SSD_STATE), lambda i, c: (l, i, 0, 0, 0)),
                  pl.BlockSpec((None, bb, SUBLANES, SSD_CONV_DIM), lambda i, c: (l, i, 0, 0)),
                  layer(SSD_CONV, SSD_CONV_DIM), layer(1, SSD_CONV_DIM), layer(1, DT_PAD), layer(1, DT_PAD),
                  layer(1, SSD_WIDTH), layer(1, SSD_WIDTH),
                  pl.BlockSpec((DT_PAD, SSD_WIDTH), lambda i, c: (0, 0))] + [pl.BlockSpec(state_blk, smap)] * n_e,
        out_specs=[pl.BlockSpec((bb, T, SSD_WIDTH), tmap), state_out[0],
                   pl.BlockSpec((bb, SUBLANES, SSD_CONV_DIM), cvmap)],
        out_shape=[jax.ShapeDtypeStruct((nb, L, SSD_WIDTH), _F32), state_out[1],
                   jax.ShapeDtypeStruct(st_conv8.shape[1:], _F32)],
        scratch_shapes=[pltpu.VMEM((bb, T + SUBLANES, SSD_CONV_DIM), _F32), pltpu.VMEM((bb, T, SSD_WIDTH), _F32)],
        compiler_params=_params("arbitrary", "arbitrary"),
    )(z, xbc, dt, st_ssd, st_conv8, P['ssd_conv_w'], P['ssd_conv_b'], P['ssd_dt_bias'], P['ssd_a_log'],
      P['ssd_d'], P['ssd_norm_g'], P['ssd_head_expand'], *earlier)


def _stack_heads(x):
    return jnp.concatenate([x[:, h * RWKV_HEAD_DIM:(h + 1) * RWKV_HEAD_DIM] for h in range(RWKV_HEADS)], axis=0)


def _unstack_heads(x, T):
    return jnp.concatenate([x[h * T:(h + 1) * T, :] for h in range(RWKV_HEADS)], axis=1)


def _interleave(gens):
    gens = list(gens)
    while gens:
        alive = []
        for g in gens:
            try:
                next(g)
                alive.append(g)
            except StopIteration:
                pass
        gens = alive


def _rwkv_masks(T):
    HT = RWKV_HEADS * T
    rr = _iota((HT, HT), 0)
    cc = _iota((HT, HT), 1)
    same = (rr // T) == (cc // T)
    rt, ct = _iota((T, T), 0), _iota((T, T), 1)
    head_of_row = _iota((HT, RWKV_WIDTH), 0) // T
    head_of_col = _iota((HT, RWKV_WIDTH), 1) // RWKV_HEAD_DIM
    return dict(strict=jnp.logical_and(same, rr > cc), incl=jnp.logical_and(same, rr >= cc),
                eye=(rr == cc).astype(_F32), hmask=head_of_row == head_of_col,
                strict_t=rt > ct, incl_t=rt >= ct, eye_t=(rt == ct).astype(_F32), tri=(rt >= ct).astype(_BF16))


def _rwkv_inverse(eye, mats, T):
    inv = [eye + n for n in mats]
    p = list(mats)
    for _ in range(max((T - 1).bit_length() - 1, 0)):
        yield
        p = [_dot(x, x) for x in p]
        yield
        inv = [i + _dot(i, x) for i, x in zip(inv, p)]
    return inv


def _rwkv_core_per_head(j, masks, so_ref, T, at, rt, bt, kt, v4f, w_end):
    cast = lambda x: x.astype(_BF16)
    a4, r4, b4, k4 = (cast(_stack_heads(t)) for t in (at, rt, bt, kt))
    v4 = cast(v4f)
    bw4, kw4 = (cast(_stack_heads(t * w_end)) for t in (bt, kt))
    heads = range(RWKV_HEADS)
    blk = lambda x, h: x[h * T:(h + 1) * T]
    yield
    n_ab = [jnp.where(masks['strict_t'], _dot_nt(blk(a4, h), blk(b4, h)), 0.0) for h in heads]
    n_ak = [cast(jnp.where(masks['strict_t'], _dot_nt(blk(a4, h), blk(k4, h)), 0.0)) for h in heads]
    m_rb = [cast(jnp.where(masks['incl_t'], _dot_nt(blk(r4, h), blk(b4, h)), 0.0)) for h in heads]
    m_rk = [cast(jnp.where(masks['incl_t'], _dot_nt(blk(r4, h), blk(k4, h)), 0.0)) for h in heads]
    inv = yield from _rwkv_inverse(masks['eye_t'], n_ab, T)
    yield
    s0 = [so_ref[j, h] for h in heads]
    rhs = [_dot_nt(blk(a4, h), s0[h]) + _dot(n_ak[h], blk(v4, h)) for h in heads]
    o_own = [_dot_nt(blk(r4, h), s0[h]) + _dot(m_rk[h], blk(v4, h)) for h in heads]
    yield
    u = [cast(_dot(inv[h], rhs[h])) for h in heads]
    yield
    o4 = jnp.concatenate([o_own[h] + _dot(m_rb[h], u[h]) for h in heads], axis=0)
    for h in heads:
        so_ref[j, h] = (s0[h] * w_end[:, h * RWKV_HEAD_DIM:(h + 1) * RWKV_HEAD_DIM]
                        + _dot_tn(u[h], blk(bw4, h)) + _dot_tn(blk(v4, h), blk(kw4, h)))
    yield
    return o4


def _rwkv_core_stacked(j, masks, so_ref, T, at, rt, bt, kt, v4f, w_end):
    strict, incl, hmask = masks['strict'], masks['incl'], masks['hmask']
    a4, r4, b4, k4 = (_stack_heads(t) for t in (at, rt, bt, kt))
    yield
    n_ab = jnp.where(strict, _dot_nt(a4, b4), 0.0)
    n_ak = jnp.where(strict, _dot_nt(a4, k4), 0.0)
    m_rb = jnp.where(incl, _dot_nt(r4, b4), 0.0)
    m_rk = jnp.where(incl, _dot_nt(r4, k4), 0.0)
    (inv,) = yield from _rwkv_inverse(masks['eye'], [n_ab], T)

    def wide(x):
        return jnp.where(hmask, jnp.concatenate([x] * RWKV_HEADS, axis=0), 0.0)

    yield
    s0 = jnp.concatenate([so_ref[j, h] for h in range(RWKV_HEADS)], axis=1)
    rhs = _dot_nt(wide(at), s0) + _dot(n_ak, v4f)
    o4 = _dot_nt(wide(rt), s0) + _dot(m_rk, v4f)
    yield
    u4 = _dot(inv, rhs)
    yield
    o4 = o4 + _dot(m_rb, u4)
    s1 = (s0 * w_end + _dot_tn(u4, wide(bt * w_end)) + _dot_tn(v4f, wide(kt * w_end)))
    for h in range(RWKV_HEADS):
        so_ref[j, h] = s1[:, h * RWKV_HEAD_DIM:(h + 1) * RWKV_HEAD_DIM]
    yield
    return o4


def _rwkv_seq(j, masks, f_ref, mu_ref, w0_ref, w2_ref, a0_ref, a2_ref, g2_ref, kk_ref, ka_ref, rk_ref, lng_ref,
              lnb_ref, y_ref, so_ref, sho_ref, ext_scr, T, valid, row0=0):
    f = f_ref[j, row0:row0 + T, :]
    ext_scr[j, 0:SUBLANES, :] = sho_ref[j]
    ext_scr[j, SUBLANES:SUBLANES + T, :] = f
    prev = ext_scr[j, SUBLANES - 1:SUBLANES - 1 + T, :]
    sho_ref[j] = ext_scr[j, valid:valid + SUBLANES, :]
    fm = f + (prev - f) * mu_ref[...]
    r = fm[:, 0:256]
    k = fm[:, 256:512]
    v = fm[:, 512:768]
    lora = fm[:, 768:896]
    gl = fm[:, 896:1024]
    w_log = -_softplus(-(w0_ref[...] + _dot(jnp.tanh(lora), w2_ref[...]))) - 0.5
    logw = -jnp.exp(w_log)
    a = _sigmoid(a0_ref[...] + _dot(lora, a2_ref[...]))
    g = _dot(_sigmoid(gl), g2_ref[...])
    k2 = k * (1.0 + (a - 1.0) * ka_ref[...])
    kk4 = _stack_heads(k * kk_ref[...])
    kk4 = kk4 * lax.rsqrt(jnp.maximum(jnp.sum(kk4 * kk4, axis=-1, keepdims=True), 1e-24))
    kkn = _unstack_heads(kk4, T)
    if valid < T:
        live = _iota((T, 1), 0) < valid
        logw = jnp.where(live, logw, 0.0)
        kkn = jnp.where(live, kkn, 0.0)
        k2 = jnp.where(live, k2, 0.0)
    yield
    cw = _mm_sel(masks['tri'], logw, _NN)
    w_inc = jnp.exp(cw)
    w_exc = jnp.exp(cw - logw)
    w_inv = jnp.exp(-cw)
    w_end = w_inc[T - 1:T, :]
    at = -kkn * w_exc
    rt = r * w_inc
    bt = kkn * a * w_inv
    kt = k2 * w_inv
    v4f = _stack_heads(v)
    core = _rwkv_core_per_head if T % 16 == 0 else _rwkv_core_stacked
    o4 = yield from core(j, masks, so_ref, T, at, rt, bt, kt, v4f, w_end)

    mean = jnp.mean(o4, axis=-1, keepdims=True)
    var = jnp.mean(jnp.square(o4 - mean), axis=-1, keepdims=True)
    o4 = (o4 - mean) * lax.rsqrt(var + RWKV_LN_EPS)
    bonus4 = jnp.sum(_stack_heads(r * k2 * rk_ref[...]), axis=-1, keepdims=True) * v4f
    o = _unstack_heads(o4, T) * lng_ref[...] + lnb_ref[...] + _unstack_heads(bonus4, T)
    y_ref[j, row0:row0 + T, :] = o * g


def _rwkv_kernel(f_ref, s_ref, sh_ref, mu_ref, w0_ref, w2_ref, a0_ref, a2_ref, g2_ref, kk_ref, ka_ref, rk_ref,
                 lng_ref, lnb_ref, *rest, bb, T, valid, n_earlier):
    earlier = rest[:n_earlier]
    y_ref, so_all, sho_ref, ext_scr = rest[n_earlier:]
    so_ref = so_all.at[n_earlier] if n_earlier else so_all

    @pl.when(pl.program_id(1) == 0)
    def _():
        so_ref[...] = s_ref[...]
        sho_ref[...] = sh_ref[...]
        for k, e_ref in enumerate(earlier):
            so_all[k] = e_ref[...]

    seq = functools.partial(_rwkv_seq, f_ref=f_ref, mu_ref=mu_ref, w0_ref=w0_ref, w2_ref=w2_ref, a0_ref=a0_ref,
                            a2_ref=a2_ref, g2_ref=g2_ref, kk_ref=kk_ref, ka_ref=ka_ref, rk_ref=rk_ref,
                            lng_ref=lng_ref, lnb_ref=lnb_ref, y_ref=y_ref, so_ref=so_ref, sho_ref=sho_ref,
                            ext_scr=ext_scr, T=T, valid=valid)
    masks = _rwkv_masks(T)
    _interleave(seq(j, masks) for j in range(bb))


def _rwkv(f, st_rwkv, shift8, P, l, *, bb, T, valid, earlier=()):
    nb, L, _ = f.shape
    hd = RWKV_HEAD_DIM
    state_blk = (bb, RWKV_HEADS, hd, hd)
    smap = lambda i, c: (i, 0, 0, 0)
    n_e = len(earlier)
    if n_e:
        state_out = (pl.BlockSpec((n_e + 1,) + state_blk, lambda i, c: (0, i, 0, 0, 0)),
                     jax.ShapeDtypeStruct((n_e + 1,) + st_rwkv.shape[1:], _F32))
    else:
        state_out = (pl.BlockSpec(state_blk, smap), jax.ShapeDtypeStruct(st_rwkv.shape[1:], _F32))
    grid = (nb // bb, L // T)
    tmap = lambda i, c: (i, c, 0)
    bmap = lambda i, c: (i, 0, 0)
    layer = lambda r, n: pl.BlockSpec((None, r, n), lambda i, c: (l, 0, 0))
    row = lambda n: layer(1, n)
    lora_w = RWKV_DECAY_LORA + RWKV_A_LORA
    return pl.pallas_call(
        functools.partial(_rwkv_kernel, bb=bb, T=T, valid=valid, n_earlier=n_e),
        grid=grid,
        in_specs=[pl.BlockSpec((bb, T, RWKV_PROJ), tmap),
                  pl.BlockSpec((None, bb, RWKV_HEADS, hd, hd), lambda i, c: (l, i, 0, 0, 0)),
                  pl.BlockSpec((None, bb, SUBLANES, RWKV_PROJ), lambda i, c: (l, i, 0, 0)),
                  row(RWKV_PROJ), row(RWKV_WIDTH), layer(lora_w, RWKV_WIDTH),
                  row(RWKV_WIDTH), layer(lora_w, RWKV_WIDTH), layer(RWKV_GATE_LORA, RWKV_WIDTH),
                  row(RWKV_WIDTH), row(RWKV_WIDTH), row(RWKV_WIDTH), row(RWKV_WIDTH), row(RWKV_WIDTH)]
        + [pl.BlockSpec(state_blk, smap)] * n_e,
        out_specs=[pl.BlockSpec((bb, T, RWKV_WIDTH), tmap), state_out[0],
                   pl.BlockSpec((bb, SUBLANES, RWKV_PROJ), bmap)],
        out_shape=[jax.ShapeDtypeStruct((nb, L, RWKV_WIDTH), _F32), state_out[1],
                   jax.ShapeDtypeStruct(shift8.shape[1:], _F32)],
        scratch_shapes=[pltpu.VMEM((bb, T + SUBLANES, RWKV_PROJ), _F32)],
        compiler_params=_params("arbitrary", "arbitrary"),
    )(f, st_rwkv, shift8, P['rwkv_mu'], P['rwkv_w0'], P['rwkv_w2'], P['rwkv_a0'], P['rwkv_a2'], P['rwkv_g2'],
      P['rwkv_k_k'], P['rwkv_k_a'], P['rwkv_r_k'], P['rwkv_ln_g'], P['rwkv_ln_b'], *earlier)


def _mix_kernel(z_ref, xbc_ref, dt_ref, st_ref, cv_ref, cw_ref, cb_ref, dtb_ref, alog_ref, dexp_ref, ng_ref, eexp_ref,
                f_ref, s_ref, sh_ref, mu_ref, w0_ref, w2_ref, a0_ref, a2_ref, g2_ref, kk_ref, ka_ref, rk_ref,
                lng_ref, lnb_ref, *rest, bb, T, RT, n_earlier):
    e_ssd, e_rw = rest[:n_earlier], rest[n_earlier:2 * n_earlier]
    y_ref, sto_all, cvo_ref, yrw_ref, so_all, sho_ref, ext_scr, ybuf_scr, rext_scr = rest[2 * n_earlier:]
    sto_ref = sto_all.at[n_earlier] if n_earlier else sto_all
    so_ref = so_all.at[n_earlier] if n_earlier else so_all

    @pl.when(pl.program_id(1) == 0)
    def _():
        sto_ref[...] = st_ref[...]
        cvo_ref[...] = cv_ref[...]
        so_ref[...] = s_ref[...]
        sho_ref[...] = sh_ref[...]
        for k in range(n_earlier):
            sto_all[k] = e_ssd[k][...]
            so_all[k] = e_rw[k][...]

    ssd = functools.partial(_ssd_seq, z_ref=z_ref, xbc_ref=xbc_ref, dt_ref=dt_ref, cw_ref=cw_ref, cb_ref=cb_ref,
                            dtb_ref=dtb_ref, alog_ref=alog_ref, dexp_ref=dexp_ref, ng_ref=ng_ref, eexp_ref=eexp_ref,
                            y_ref=y_ref, sto_ref=sto_ref, cvo_ref=cvo_ref, ext_scr=ext_scr, ybuf_scr=ybuf_scr,
                            T=T, valid=T)
    rwkv = functools.partial(_rwkv_seq, f_ref=f_ref, mu_ref=mu_ref, w0_ref=w0_ref, w2_ref=w2_ref, a0_ref=a0_ref,
                             a2_ref=a2_ref, g2_ref=g2_ref, kk_ref=kk_ref, ka_ref=ka_ref, rk_ref=rk_ref,
                             lng_ref=lng_ref, lnb_ref=lnb_ref, y_ref=yrw_ref, so_ref=so_ref, sho_ref=sho_ref,
                             ext_scr=rext_scr, T=RT, valid=RT)
    masks = _rwkv_masks(RT)

    def rwkv_chunks(j):
        for c in range(T // RT):
            yield from rwkv(j, masks, row0=c * RT)

    _interleave([rwkv_chunks(j) for j in range(bb)] + [ssd(j) for j in range(bb)])


def _mix(z, xbc, dt, f, st_ssd, st_conv8, st_rwkv, shift8, P, l, *, bb, T, RT, earlier_ssd=(), earlier_rwkv=()):
    assert T % RT == 0
    nb, L, _ = z.shape
    hd = RWKV_HEAD_DIM
    grid = (nb // bb, L // T)
    tmap = lambda i, c: (i, c, 0)
    bmap = lambda i, c: (i, 0, 0)
    smap = lambda i, c: (i, 0, 0, 0)
    layer = lambda r, n: pl.BlockSpec((None, r, n), lambda i, c: (l, 0, 0))
    row = lambda n: layer(1, n)
    lora_w = RWKV_DECAY_LORA + RWKV_A_LORA
    ssd_blk = (bb, SSD_HEADS, SSD_HEAD_DIM, SSD_STATE)
    rw_blk = (bb, RWKV_HEADS, hd, hd)
    n_e = len(earlier_ssd)
    assert len(earlier_rwkv) == n_e

    def state_out(blk, full_shape):
        if n_e:
            return (pl.BlockSpec((n_e + 1,) + blk, lambda i, c: (0, i, 0, 0, 0)),
                    jax.ShapeDtypeStruct((n_e + 1,) + full_shape, _F32))
        return pl.BlockSpec(blk, smap), jax.ShapeDtypeStruct(full_shape, _F32)

    ssd_out, rw_out = state_out(ssd_blk, st_ssd.shape[1:]), state_out(rw_blk, st_rwkv.shape[1:])
    outs = pl.pallas_call(
        functools.partial(_mix_kernel, bb=bb, T=T, RT=RT, n_earlier=n_e),
        grid=grid,
        in_specs=[pl.BlockSpec((bb, T, SSD_WIDTH), tmap), pl.BlockSpec((bb, T, SSD_CONV_DIM), tmap),
                  pl.BlockSpec((bb, T, DT_PAD), tmap),
                  pl.BlockSpec((None,) + ssd_blk, lambda i, c: (l, i, 0, 0, 0)),
                  pl.BlockSpec((None, bb, SUBLANES, SSD_CONV_DIM), lambda i, c: (l, i, 0, 0)),
                  layer(SSD_CONV, SSD_CONV_DIM), layer(1, SSD_CONV_DIM), layer(1, DT_PAD), layer(1, DT_PAD),
                  layer(1, SSD_WIDTH), layer(1, SSD_WIDTH), pl.BlockSpec((DT_PAD, SSD_WIDTH), lambda i, c: (0, 0)),
                  pl.BlockSpec((bb, T, RWKV_PROJ), tmap),
                  pl.BlockSpec((None,) + rw_blk, lambda i, c: (l, i, 0, 0, 0)),
                  pl.BlockSpec((None, bb, SUBLANES, RWKV_PROJ), lambda i, c: (l, i, 0, 0)),
                  row(RWKV_PROJ), row(RWKV_WIDTH), layer(lora_w, RWKV_WIDTH),
                  row(RWKV_WIDTH), layer(lora_w, RWKV_WIDTH), layer(RWKV_GATE_LORA, RWKV_WIDTH),
                  row(RWKV_WIDTH), row(RWKV_WIDTH), row(RWKV_WIDTH), row(RWKV_WIDTH), row(RWKV_WIDTH)]
        + [pl.BlockSpec(ssd_blk, smap)] * n_e + [pl.BlockSpec(rw_blk, smap)] * n_e,
        out_specs=[pl.BlockSpec((bb, T, SSD_WIDTH), tmap), ssd_out[0],
                   pl.BlockSpec((bb, SUBLANES, SSD_CONV_DIM), bmap),
                   pl.BlockSpec((bb, T, RWKV_WIDTH), tmap), rw_out[0],
                   pl.BlockSpec((bb, SUBLANES, RWKV_PROJ), bmap)],
        out_shape=[jax.ShapeDtypeStruct((nb, L, SSD_WIDTH), _F32), ssd_out[1],
                   jax.ShapeDtypeStruct(st_conv8.shape[1:], _F32),
                   jax.ShapeDtypeStruct((nb, L, RWKV_WIDTH), _F32), rw_out[1],
                   jax.ShapeDtypeStruct(shift8.shape[1:], _F32)],
        scratch_shapes=[pltpu.VMEM((bb, T + SUBLANES, SSD_CONV_DIM), _F32), pltpu.VMEM((bb, T, SSD_WIDTH), _F32),
                        pltpu.VMEM((bb, RT + SUBLANES, RWKV_PROJ), _F32)],
        compiler_params=_params("arbitrary", "arbitrary"),
    )(z, xbc, dt, st_ssd, st_conv8, P['ssd_conv_w'], P['ssd_conv_b'], P['ssd_dt_bias'], P['ssd_a_log'],
      P['ssd_d'], P['ssd_norm_g'], P['ssd_head_expand'],
      f, st_rwkv, shift8, P['rwkv_mu'], P['rwkv_w0'], P['rwkv_w2'], P['rwkv_a0'], P['rwkv_a2'], P['rwkv_g2'],
      P['rwkv_k_k'], P['rwkv_k_a'], P['rwkv_r_k'], P['rwkv_ln_g'], P['rwkv_ln_b'], *earlier_ssd, *earlier_rwkv)
    return outs


def _s5_kernel(u_ref, sre_ref, sim_ref, lre_ref, lim_ref, dt_ref, bre_ref, bim_ref, cre_ref, cim_ref, d_ref,
               gw_ref, gb_ref, y_ref, hre_ref, him_ref, hre_scr, him_scr, *, nb, tl, final_step):
    @pl.when(pl.program_id(0) == 0)
    def _():
        hre_ref[...] = sre_ref[...]
        him_ref[...] = sim_ref[...]

    lre, lim, dt = lre_ref[...], lim_ref[...], jnp.exp(dt_ref[...])
    mag = jnp.exp(lre * dt)
    ang = lim * dt
    ab_re, ab_im = mag * jnp.cos(ang), mag * jnp.sin(ang)
    den = jnp.square(lre) + jnp.square(lim)
    q_re = ((ab_re - 1.0) * lre + ab_im * lim) / den
    q_im = (ab_im * lre - (ab_re - 1.0) * lim) / den
    b_re, b_im = bre_ref[...], bim_ref[...]
    bb_re = q_re * b_re - q_im * b_im
    bb_im = q_re * b_im + q_im * b_re
    ar = jnp.broadcast_to(ab_re, (nb, S5_FLAT))
    ai = jnp.broadcast_to(ab_im, (nb, S5_FLAT))

    def readout(h_re, h_im, ub):
        y = _dot(h_re, cre_ref[...]) - _dot(h_im, cim_ref[...]) + d_ref[...] * ub
        y = 0.5 * y * (1.0 + jnp.tanh(0.7978845608028654 * (y + 0.044715 * (y * y * y))))
        return y * _sigmoid(_dot(y, gw_ref[...]) + gb_ref[...])

    if nb <= SUBLANES and tl % (2 * S5_BLOCK_STEPS) == 0:
        half = tl // 2
        bb_re, bb_im = bb_re.astype(_BF16), bb_im.astype(_BF16)
        hrows = lambda t0, t1: slice((t0 + 1) * nb, (t1 + 1) * nb)
        urows = lambda t0, t1: slice(t0 * nb, t1 * nb)
        carry = [(hre_ref[...], him_ref[...])]

        def project(t0, t1):
            for a in range(t0, t1, S5_BLOCK_STEPS):
                ub = u_ref[urows(a, a + S5_BLOCK_STEPS), :]
                hre_scr[hrows(a, a + S5_BLOCK_STEPS), :] = _dot(ub, bb_re)
                him_scr[hrows(a, a + S5_BLOCK_STEPS), :] = _dot(ub, bb_im)
                yield

        def scan(t0, t1):
            pr, pi = carry[0]
            for s in range(t0, t1):
                cur = hrows(s, s + 1)
                nr = ar * pr - ai * pi + hre_scr[cur, :]
                ni = ar * pi + ai * pr + him_scr[cur, :]
                hre_scr[cur, :] = nr
                him_scr[cur, :] = ni
                pr, pi = nr, ni
                if (s + 1 - t0) % S5_BLOCK_STEPS == 0:
                    yield
            carry[0] = (pr, pi)

        def emit(t0, t1):
            for a in range(t0, t1, S5_BLOCK_STEPS):
                hr, ur = hrows(a, a + S5_BLOCK_STEPS), urows(a, a + S5_BLOCK_STEPS)
                y_ref[ur, :] = readout(hre_scr[hr, :], him_scr[hr, :], u_ref[ur, :])
                yield

        _interleave([project(0, half)])
        _interleave([scan(0, half), project(half, tl)])
        _interleave([scan(half, tl), emit(0, half)])
        _interleave([emit(half, tl)])
    else:
        u = u_ref[...]
        hre_scr[0:nb, :] = hre_ref[...]
        him_scr[0:nb, :] = him_ref[...]
        hre_scr[nb:, :] = _dot(u, bb_re)
        him_scr[nb:, :] = _dot(u, bb_im)

        def step(s, carry):
            prev = pl.ds(pl.multiple_of(s * nb, nb), nb)
            cur = pl.ds(pl.multiple_of((s + 1) * nb, nb), nb)
            pr, pi = hre_scr[prev, :], him_scr[prev, :]
            hre_scr[cur, :] = ar * pr - ai * pi + hre_scr[cur, :]
            him_scr[cur, :] = ar * pi + ai * pr + him_scr[cur, :]
            return carry

        lax.fori_loop(0, tl, step, 0)
        y_ref[...] = readout(hre_scr[nb:, :], him_scr[nb:, :], u)
    hre_ref[...] = hre_scr[(final_step + 1) * nb:(final_step + 2) * nb, :]
    him_ref[...] = him_scr[(final_step + 1) * nb:(final_step + 2) * nb, :]


def _s5(u_tm, st_re, st_im, P, l, *, nb, tl, final_step):
    rows = u_tm.shape[0]
    L = rows // nb
    layer = lambda r, n: pl.BlockSpec((None, r, n), lambda t: (l, 0, 0))
    row = lambda n: layer(1, n)
    full = lambda a, b: pl.BlockSpec((a, b), lambda t: (0, 0))
    state = pl.BlockSpec((None, nb, S5_FLAT), lambda t: (l, 0, 0))
    return pl.pallas_call(
        functools.partial(_s5_kernel, nb=nb, tl=tl, final_step=final_step),
        grid=(L // tl,),
        in_specs=[pl.BlockSpec((tl * nb, S5_WIDTH), lambda t: (t, 0)), state, state,
                  row(S5_FLAT), row(S5_FLAT), row(S5_FLAT),
                  layer(S5_WIDTH, S5_FLAT), layer(S5_WIDTH, S5_FLAT), layer(S5_FLAT, S5_WIDTH),
                  layer(S5_FLAT, S5_WIDTH), row(S5_WIDTH), layer(S5_WIDTH, S5_WIDTH), row(S5_WIDTH)],
        out_specs=[pl.BlockSpec((tl * nb, S5_WIDTH), lambda t: (t, 0)), full(nb, S5_FLAT), full(nb, S5_FLAT)],
        out_shape=[jax.ShapeDtypeStruct((rows, S5_WIDTH), _F32),
                   jax.ShapeDtypeStruct((nb, S5_FLAT), _F32), jax.ShapeDtypeStruct((nb, S5_FLAT), _F32)],
        scratch_shapes=[pltpu.VMEM(((tl + 1) * nb, S5_FLAT), _F32), pltpu.VMEM(((tl + 1) * nb, S5_FLAT), _F32)],
        compiler_params=_params("arbitrary"),
    )(u_tm, st_re, st_im, P['s5_a_re'], P['s5_a_im'], P['s5_log_dt'], P['s5_b_re'], P['s5_b_im'], P['s5_c_re'],
      P['s5_c_im'], P['s5_d'], P['s5_glu_w'], P['s5_glu_b'])


def _out_mlp_kernel(x_ref, yssd_ref, yrw_ref, ys5_ref, g1_ref, sh2_ref, sc2_ref, g2_ref, n2g_ref, fg_ref,
                    wo_ref, w1_ref, w2_ref, o_ref, hid_scr, *, final_norm):
    bb, tl, _ = x_ref.shape
    rows = bb * tl
    flat = lambda ref: ref[...].reshape(rows, ref.shape[-1]).astype(_BF16)
    mix = (_dot(flat(yssd_ref), wo_ref[0:512, :]) + _dot(flat(yrw_ref), wo_ref[512:768, :])
           + _dot(flat(ys5_ref), wo_ref[768:1024, :]))
    x1 = x_ref[...] + g1_ref[...] * mix.reshape(bb, tl, D_MODEL)
    xn = x1 * lax.rsqrt(jnp.mean(x1 * x1, axis=-1, keepdims=True) + NORM_EPS)
    h2 = (xn * n2g_ref[...] * (1.0 + sc2_ref[...]) + sh2_ref[...]).reshape(rows, D_MODEL).astype(_BF16)
    for c in range(D_FF // FF_TILE):
        cols = slice(c * FF_TILE, (c + 1) * FF_TILE)
        hid_scr[:, cols] = jnp.square(jnp.maximum(_dot(h2, w1_ref[:, cols]), 0.0)).astype(_BF16)
    ff = _dot(hid_scr[...], w2_ref[...])
    out = x1 + g2_ref[...] * ff.reshape(bb, tl, D_MODEL)
    if final_norm:
        out = out * lax.rsqrt(jnp.mean(out * out, axis=-1, keepdims=True) + NORM_EPS) * fg_ref[...]
    o_ref[...] = out


def _out_mlp(x, y_ssd, y_rw, y_s5, mod, P, l, *, bb, tl, time_major_s5, final_norm):
    nb, L, _ = x.shape
    grid = (nb // bb, L // tl)
    xmap = lambda i, t: (i, t, 0)
    cmap = lambda i, t: (0, 0)
    wmap = lambda i, t: (l, 0, 0)
    once = pl.Buffered(1)

    def modspec(j):
        return pl.BlockSpec((bb, 1, D_MODEL), lambda i, t: (i, 0, j))

    s5_spec = (pl.BlockSpec((tl, S5_WIDTH), lambda i, t: (t, i)) if time_major_s5
               else pl.BlockSpec((bb, tl, S5_WIDTH), xmap))
    return pl.pallas_call(
        functools.partial(_out_mlp_kernel, final_norm=final_norm),
        grid=grid,
        in_specs=[pl.BlockSpec((bb, tl, D_MODEL), xmap), pl.BlockSpec((bb, tl, SSD_WIDTH), xmap),
                  pl.BlockSpec((bb, tl, RWKV_WIDTH), xmap), s5_spec,
                  modspec(2), modspec(3), modspec(4), modspec(5),
                  pl.BlockSpec((None, 1, D_MODEL), wmap), pl.BlockSpec((1, D_MODEL), cmap),
                  pl.BlockSpec((None, D_MODEL, D_MODEL), wmap, pipeline_mode=once),
                  pl.BlockSpec((None, D_MODEL, D_FF), wmap, pipeline_mode=once),
                  pl.BlockSpec((None, D_FF, D_MODEL), wmap, pipeline_mode=once)],
        out_specs=pl.BlockSpec((bb, tl, D_MODEL), xmap),
        out_shape=jax.ShapeDtypeStruct(x.shape, _F32),
        scratch_shapes=[pltpu.VMEM((bb * tl, D_FF), _BF16)],
        compiler_params=_params("arbitrary", "arbitrary"),
    )(x, y_ssd, y_rw, y_s5, mod, mod, mod, mod, P['norm2_g'], P['final_g'], P['w_out'], P['mlp_w1'], P['mlp_w2'])


def _trunk(x, mod, states, P, valid, *, prompt):
    st_ssd, st_conv, st_rwkv, st_shift, st_re, st_im = states
    nb, L, _ = x.shape
    if prompt:
        bb, tl, mlp_tl = 1, min(IN_ROW_TILE, L), min(MLP_ROW_TILE, L)
        ssd_t, rwkv_t, s5_tl = min(SSD_CHUNK, L), min(RWKV_CHUNK, L), min(S5_TILE, L)
    else:
        bb, tl, mlp_tl = min(nb, ROW_TILE // L), L, L
        ssd_t = rwkv_t = s5_tl = L
    new = ([], [], [], [], [], [])
    for l in range(DEPTH):
        z, xbc, frw, us5, dt = _in_proj(x, mod[l], P, l, bb=bb, tl=tl, time_major_s5=prompt)
        v = valid
        last = l == DEPTH - 1
        e_ssd, e_rwkv = (tuple(new[0]), tuple(new[2])) if last else ((), ())
        if prompt:
            y_ssd, n_ssd, n_conv, y_rw, n_rwkv, n_shift = _mix(
                z, xbc, dt, frw, st_ssd, st_conv, st_rwkv, st_shift, P, l, bb=min(nb, MIX_PROMPT_SEQS), T=ssd_t,
                RT=rwkv_t, earlier_ssd=e_ssd, earlier_rwkv=e_rwkv)
        else:
            sample_bb = min(nb, SAMPLE_SEQS)
            y_ssd, n_ssd, n_conv = _ssd(z, xbc, dt, st_ssd, st_conv, P, l, bb=sample_bb, T=ssd_t, valid=v,
                                        earlier=e_ssd)
            y_rw, n_rwkv, n_shift = _rwkv(frw, st_rwkv, st_shift, P, l, bb=sample_bb, T=rwkv_t, valid=v,
                                          earlier=e_rwkv)
        if prompt:
            u_tm = us5.reshape(L * nb, S5_WIDTH)
        else:
            u_tm = jnp.transpose(us5, (1, 0, 2)).reshape(L * nb, S5_WIDTH)
        y_s5, n_re, n_im = _s5(u_tm, st_re, st_im, P, l, nb=nb, tl=s5_tl,
                               final_step=(s5_tl - 1) if prompt else (v - 1))
        if prompt:
            y_s5 = y_s5.reshape(L, nb * S5_WIDTH)
        else:
            y_s5 = jnp.transpose(y_s5.reshape(L, nb, S5_WIDTH), (1, 0, 2))
        x = _out_mlp(x, y_ssd, y_rw, y_s5, mod[l], P, l, bb=bb, tl=mlp_tl, time_major_s5=prompt,
                     final_norm=(l == DEPTH - 1))
        for lst, s in zip(new, (n_ssd, n_conv, n_rwkv, n_shift, n_re, n_im)):
            lst.append(s)
    by_kernel = (0, 2) if DEPTH > 1 else ()
    return x, [lst[-1] if k in by_kernel else jnp.stack(lst) for k, lst in enumerate(new)]


def _pack_states(st_ssd, st_conv, st_rwkv, st_shift, st_re, st_im):
    d, nb = st_ssd.shape[:2]
    conv8 = jnp.pad(st_conv, ((0, 0), (0, 0), (SUBLANES - (SSD_CONV - 1), 0), (0, 0)))
    shift8 = jnp.pad(st_shift[:, :, None, :], ((0, 0), (0, 0), (SUBLANES - 1, 0), (0, 0)))
    return (st_ssd, conv8, st_rwkv, shift8, st_re.reshape(d, nb, S5_FLAT), st_im.reshape(d, nb, S5_FLAT))


def _unpack_states(n_ssd, n_conv8, n_rwkv, n_shift8, n_re, n_im):
    d, nb = n_ssd.shape[:2]
    return (n_ssd, n_conv8[:, :, SUBLANES - (SSD_CONV - 1):, :], n_rwkv, n_shift8[:, :, SUBLANES - 1, :],
            n_re.reshape(d, nb, S5_GROUPS, S5_STATE), n_im.reshape(d, nb, S5_GROUPS, S5_STATE))


def _prepare_params(norm1_g, norm2_g, w_in, ssd_conv_w, ssd_conv_b, ssd_dt_bias, ssd_a_log, ssd_d, ssd_norm_g, rwkv_mu,
                    rwkv_w0, rwkv_w2, rwkv_a0, rwkv_a2, rwkv_g2, rwkv_k_k, rwkv_k_a, rwkv_r_k, rwkv_ln_g, rwkv_ln_b,
                    s5_a_re, s5_a_im, s5_log_dt, s5_b_re, s5_b_im, s5_c_re, s5_c_im, s5_d, s5_glu_w, s5_glu_b, w_out,
                    mlp_w1, mlp_w2, final_g):
    rows = lambda a: a.reshape(DEPTH, 1, -1)
    head_pad = ((0, 0), (0, DT_PAD - SSD_HEADS))
    eye = jnp.eye(S5_GROUPS, dtype=_F32)
    zeros_lora = jnp.zeros((DEPTH, RWKV_A_LORA, RWKV_WIDTH), _F32)
    w_in_p = jnp.concatenate(
        [w_in[:, :, _Z0:_DT0], w_in[:, :, _FRW0:_IN_COLS], w_in[:, :, _DT0:_FRW0],
         jnp.zeros((DEPTH, D_MODEL, DT_PAD - SSD_HEADS), w_in.dtype)], axis=-1).astype(_BF16)
    return dict(
        norm1_g=rows(norm1_g), norm2_g=rows(norm2_g), final_g=final_g.reshape(1, D_MODEL), w_in_p=w_in_p,
        w_out=w_out.astype(_BF16), mlp_w1=mlp_w1.astype(_BF16), mlp_w2=mlp_w2.astype(_BF16),
        ssd_conv_w=ssd_conv_w, ssd_conv_b=rows(ssd_conv_b), ssd_dt_bias=rows(jnp.pad(ssd_dt_bias, head_pad)),
        ssd_a_log=rows(jnp.pad(ssd_a_log, head_pad)), ssd_d=rows(jnp.repeat(ssd_d, SSD_HEAD_DIM, axis=-1)),
        ssd_norm_g=rows(ssd_norm_g),
        ssd_head_expand=jnp.repeat(jnp.eye(DT_PAD, SSD_HEADS, dtype=_F32), SSD_HEAD_DIM, axis=1),
        rwkv_mu=rows(rwkv_mu), rwkv_w0=rows(rwkv_w0), rwkv_a0=rows(rwkv_a0), rwkv_k_k=rows(rwkv_k_k),
        rwkv_k_a=rows(rwkv_k_a), rwkv_r_k=rows(rwkv_r_k), rwkv_ln_g=rows(rwkv_ln_g), rwkv_ln_b=rows(rwkv_ln_b),
        rwkv_w2=jnp.concatenate([rwkv_w2, zeros_lora], axis=1),
        rwkv_a2=jnp.concatenate([zeros_lora, rwkv_a2], axis=1),
        rwkv_g2=rwkv_g2,
        s5_a_re=rows(s5_a_re), s5_a_im=rows(s5_a_im), s5_log_dt=rows(jnp.repeat(s5_log_dt, S5_STATE, axis=-1)),
        s5_b_re=jnp.einsum('lgpc,gh->lgchp', s5_b_re, eye).reshape(DEPTH, S5_WIDTH, S5_FLAT),
        s5_b_im=jnp.einsum('lgpc,gh->lgchp', s5_b_im, eye).reshape(DEPTH, S5_WIDTH, S5_FLAT),
        s5_c_re=jnp.einsum('lgcp,gh->lgphc', s5_c_re, eye).reshape(DEPTH, S5_FLAT, S5_WIDTH),
        s5_c_im=jnp.einsum('lgcp,gh->lgphc', s5_c_im, eye).reshape(DEPTH, S5_FLAT, S5_WIDTH),
        s5_d=rows(s5_d), s5_glu_w=s5_glu_w, s5_glu_b=rows(s5_glu_b))


def kernel(x_prompt, x_sample, c_prompt, c_sample, state_ssd, state_ssd_conv, state_rwkv, state_rwkv_shift,
           state_s5_re, state_s5_im, ada_w, ada_b, norm1_g, norm2_g, w_in, ssd_conv_w, ssd_conv_b, ssd_dt_bias,
           ssd_a_log, ssd_d, ssd_norm_g, rwkv_mu, rwkv_w0, rwkv_w2, rwkv_a0, rwkv_a2, rwkv_g2, rwkv_k_k, rwkv_k_a,
           rwkv_r_k, rwkv_ln_g, rwkv_ln_b, s5_a_re, s5_a_im, s5_log_dt, s5_b_re, s5_b_im, s5_c_re, s5_c_im, s5_d,
           s5_glu_w, s5_glu_b, w_out, mlp_w1, mlp_w2, final_g):
    bp, lp, _ = x_prompt.shape
    bs, ls, _ = x_sample.shape
    ls_pad = -(-ls // SUBLANES) * SUBLANES
    P = _prepare_params(norm1_g, norm2_g, w_in, ssd_conv_w, ssd_conv_b, ssd_dt_bias, ssd_a_log, ssd_d, ssd_norm_g,
                        rwkv_mu, rwkv_w0, rwkv_w2, rwkv_a0, rwkv_a2, rwkv_g2, rwkv_k_k, rwkv_k_a, rwkv_r_k, rwkv_ln_g,
                        rwkv_ln_b, s5_a_re, s5_a_im, s5_log_dt, s5_b_re, s5_b_im, s5_c_re, s5_c_im, s5_d, s5_glu_w,
                        s5_glu_b, w_out, mlp_w1, mlp_w2, final_g)
    mod = _ada(jnp.concatenate([c_prompt, c_sample], axis=0), ada_w, ada_b)
    mod_p = mod[:, :bp].reshape(DEPTH, bp, 1, 6 * D_MODEL)
    mod_s = mod[:, bp:].reshape(DEPTH, bs, 1, 6 * D_MODEL)

    zeros = lambda shape: jnp.zeros((DEPTH, bp) + shape, _F32)
    p_states = _pack_states(zeros((SSD_HEADS, SSD_HEAD_DIM, SSD_STATE)), zeros((SSD_CONV - 1, SSD_CONV_DIM)),
                            zeros((RWKV_HEADS, RWKV_HEAD_DIM, RWKV_HEAD_DIM)), zeros((RWKV_PROJ,)),
                            zeros((S5_GROUPS, S5_STATE)), zeros((S5_GROUPS, S5_STATE)))
    y_prompt, sp = _trunk(x_prompt, mod_p, p_states, P, ls, prompt=True)

    s_states = _pack_states(state_ssd, state_ssd_conv, state_rwkv, state_rwkv_shift, state_s5_re, state_s5_im)
    x_s = jnp.pad(x_sample, ((0, 0), (0, ls_pad - ls), (0, 0)))
    y_sample, ss = _trunk(x_s, mod_s, s_states, P, ls, prompt=False)
    return (y_prompt, y_sample[:, :ls]) + _unpack_states(*sp) + _unpack_states(*ss)
```
